```python
import math
import jax
import jax.numpy as jnp
from jax import lax
import numpy as np

D_MODEL = 2048
BATCH = 4
SEQ = 4096
DEPTH = 2

PLE_DIM = 256
NORM_EPS = 1e-6
ROPE_THETA = 500000.0
ROPE_FRACTION = 4
NEG_INF = -1e30

SSD_HEAD_DIM = 64
SSD_INNER = D_MODEL
SSD_HEADS = SSD_INNER // SSD_HEAD_DIM
SSD_GROUPS = 8
SSD_STATE = 128
SSD_CONV_WIDTH = 4
SSD_CONV_CH = SSD_INNER + 2 * SSD_GROUPS * SSD_STATE
SSD_CHUNK = 128
SC_CHANNELS = D_MODEL
SC_WIDTH = 3
EVEN_IN_SPLITS = (SSD_INNER, SSD_CONV_CH, SSD_HEADS, SC_CHANNELS, SC_CHANNELS, SC_CHANNELS)
EVEN_IN = sum(EVEN_IN_SPLITS)
EVEN_MIX = SSD_INNER + SC_CHANNELS

HEAD_DIM = 128
NSA_HEADS = 8
NSA_KV_HEADS = 2
NSA_GROUP = NSA_HEADS // NSA_KV_HEADS
CMP_BLOCK = 32
CMP_STRIDE = 16
SEL_BLOCK = 64
SEL_TOPK = 16
SEL_FORCE = 1e4
WINDOW = 512
ATTN_QBLOCK = 128
SEL_QBLOCK = 64
DIFF_HEADS = 4
DIFF_QK_DIM = 128
DIFF_V_DIM = 256
ODD_IN_SPLITS = (NSA_HEADS * HEAD_DIM,) + (NSA_KV_HEADS * HEAD_DIM,) * 6 + (
    3 * NSA_HEADS, DIFF_HEADS * 2 * DIFF_QK_DIM, DIFF_HEADS * 2 * DIFF_QK_DIM, DIFF_HEADS * DIFF_V_DIM)
ODD_IN = sum(ODD_IN_SPLITS)
ODD_MIX = NSA_HEADS * HEAD_DIM + DIFF_HEADS * DIFF_V_DIM

MOE_GROUPS = 4
MOE_EXPERTS_PER_GROUP = 8
MOE_EXPERTS = MOE_GROUPS * MOE_EXPERTS_PER_GROUP
MOE_TOP_K = 2
MOE_FF = D_MODEL // 2
MOE_BLOCK = 128

kernel_name = 'hybrid_ssd_conv_nsa_diffattn_hmoe'


def rms_norm(x, g):
    xf = x.astype(jnp.float32)
    y = xf * lax.rsqrt(jnp.mean(xf * xf, axis=-1, keepdims=True) + NORM_EPS)
    return (y * g.astype(jnp.float32)).astype(x.dtype)


def _split(x, sizes):
    return jnp.split(x, [int(v) for v in np.cumsum(sizes)[:-1]], axis=-1)


def partial_rope(x, pos):
    rot = x.shape[-1] // ROPE_FRACTION
    half = rot // 2
    inv_freq = ROPE_THETA ** (-jnp.arange(half, dtype=jnp.float32) / half)
    ang = pos.astype(jnp.float32)[:, None] * inv_freq[None, :]
    cos, sin = jnp.cos(ang)[:, None, :], jnp.sin(ang)[:, None, :]
    xf = x.astype(jnp.float32)
    x1, x2, rest = xf[..., :half], xf[..., half:rot], xf[..., rot:]
    return jnp.concatenate([x1 * cos - x2 * sin, x2 * cos + x1 * sin, rest], axis=-1).astype(x.dtype)


def causal_dwconv(x, w):
    k, c = w.shape
    return lax.conv_general_dilated(x, w[:, None, :].astype(x.dtype), window_strides=(1,),
                                    padding=[(k - 1, 0)], dimension_numbers=('NWC', 'WIO', 'NWC'),
                                    feature_group_count=c)


def ssd_chunked(xh, dt, a, bm, cm):
    b, s, h, p = xh.shape
    g, n = bm.shape[2], bm.shape[3]
    r = h // g
    q = SSD_CHUNK
    c = s // q
    xdt = (xh * dt[..., None]).reshape(b, c, q, g, r, p)
    a_cum = jnp.cumsum((dt * a).reshape(b, c, q, g, r), axis=2)
    bc = bm.reshape(b, c, q, g, n)
    cc = cm.reshape(b, c, q, g, n)
    causal = jnp.tril(jnp.ones((q, q), bool))[None, None, :, :, None, None]
    seg = a_cum[:, :, :, None] - a_cum[:, :, None, :]
    decay = jnp.exp(jnp.where(causal, seg, -jnp.inf))
    cb = jnp.einsum('bctgn,bcsgn->bctsg', cc, bc)
    y_diag = jnp.einsum('bctsgr,bcsgrp->bctgrp', cb[..., None] * decay, xdt)
    decay_end = jnp.exp(a_cum[:, :, -1:] - a_cum)
    chunk_states = jnp.einsum('bcsgn,bcsgrp->bcgrpn', bc, xdt * decay_end[..., None])
    chunk_decay = jnp.exp(a_cum[:, :, -1])

    def step(state, inp):
        st, dec = inp
        return state * dec[..., None, None] + st, state

    _, prev = lax.scan(step, jnp.zeros_like(chunk_states[:, 0]),
                       (jnp.moveaxis(chunk_states, 1, 0), jnp.moveaxis(chunk_decay, 1, 0)))
    prev = jnp.moveaxis(prev, 0, 1)
    y_off = jnp.einsum('bctgn,bcgrpn->bctgrp', cc, prev) * jnp.exp(a_cum)[..., None]
    return (y_diag + y_off).reshape(b, s, h, p)


def ssd_shortconv_mixer(hn, w_in, conv_w, conv_b, dt_bias, a_log, d_skip, gate_norm, sc_w, w_out):
    b, s, _ = hn.shape
    f32 = jnp.float32
    z, xbc, dt, sc_b, sc_c, sc_h = _split(hn @ w_in, EVEN_IN_SPLITS)
    xbc = jax.nn.silu(causal_dwconv(xbc, conv_w) + conv_b)
    xs, bm, cm = _split(xbc, (SSD_INNER, SSD_GROUPS * SSD_STATE, SSD_GROUPS * SSD_STATE))
    xh = xs.reshape(b, s, SSD_HEADS, SSD_HEAD_DIM).astype(f32)
    dt = jax.nn.softplus(dt.astype(f32) + dt_bias.astype(f32))
    a = -jnp.exp(a_log.astype(f32))
    y = ssd_chunked(xh, dt, a,
                    bm.reshape(b, s, SSD_GROUPS, SSD_STATE).astype(f32),
                    cm.reshape(b, s, SSD_GROUPS, SSD_STATE).astype(f32))
    y = y + d_skip.astype(f32)[:, None] * xh
    y = y.reshape(b, s, SSD_INNER) * jax.nn.silu(z.astype(f32))
    y = rms_norm(y.reshape(b, s, SSD_GROUPS, -1), gate_norm.reshape(SSD_GROUPS, -1))
    y = y.reshape(b, s, SSD_INNER).astype(hn.dtype)
    y_sc = sc_b * causal_dwconv(sc_c * sc_h, sc_w)
    return jnp.concatenate([y, y_sc], axis=-1) @ w_out


def compress_blocks(k, pos_emb, w1, w2):
    b, s, g, dh = k.shape
    span = CMP_BLOCK // CMP_STRIDE
    n_chunk = s // CMP_STRIDE
    n_cmp = n_chunk - span + 1
    kc = k.reshape(b, n_chunk, CMP_STRIDE, g, dh)
    blocks = jnp.concatenate([kc[:, j:j + n_cmp] for j in range(span)], axis=2)
    blocks = blocks + pos_emb[None, None, :, None, :]
    flat = jnp.moveaxis(blocks, 3, 2).reshape(b, n_cmp, g, CMP_BLOCK * dh)
    return jax.nn.silu(flat @ w1) @ w2


def nsa_diff_mixer(hn, w_in, cmp_pos, cmp_w1, cmp_w2, lam, subln, w_out, lambda_init):
    b, s, _ = hn.shape
    f32 = jnp.float32
    G, R, Dh = NSA_KV_HEADS, NSA_GROUP, HEAD_DIM
    pos = jnp.arange(s)
    (q, k_cmp, v_cmp, k_sel, v_sel, k_win, v_win, gates, dq, dk, dv) = _split(hn @ w_in, ODD_IN_SPLITS)

    def kv(t):
        return t.reshape(b, s, G, Dh)

    q = partial_rope(q.reshape(b, s, NSA_HEADS, Dh), pos).reshape(b, s, G, R, Dh)
    scale = Dh ** -0.5

    kc = compress_blocks(kv(k_cmp), cmp_pos[0], cmp_w1[0], cmp_w2[0])
    vc = compress_blocks(kv(v_cmp), cmp_pos[1], cmp_w1[1], cmp_w2[1])
    n_cmp = kc.shape[1]
    cmp_end = jnp.arange(n_cmp) * CMP_STRIDE + CMP_BLOCK - 1
    kc = partial_rope(kc, cmp_end)
    cmp_mask = cmp_end[None, :] <= pos[:, None]
    sc = jnp.einsum('bsgrd,bngd->bgrsn', q, kc).astype(f32) * scale
    p_cmp = jnp.where(cmp_mask, jax.nn.softmax(jnp.where(cmp_mask, sc, NEG_INF), axis=-1), 0.0)
    o_cmp = jnp.einsum('bgrsn,bngd->bsgrd', p_cmp.astype(vc.dtype), vc)

    n_sel = s // SEL_BLOCK
    per_sel = SEL_BLOCK // CMP_STRIDE
    per_cmp = CMP_BLOCK // CMP_STRIDE
    imp = jnp.pad(p_cmp.sum(axis=2), ((0, 0), (0, 0), (0, 0), (per_cmp - 1, n_sel * per_sel - n_cmp)))
    p_sel = imp[..., 0:per_sel * (n_sel - 1) + 1:per_sel]
    for o in range(1, per_sel + per_cmp - 1):
        p_sel = p_sel + imp[..., o:o + per_sel * (n_sel - 1) + 1:per_sel]
    blk = jnp.arange(n_sel)
    cur = pos // SEL_BLOCK
    forced = (blk[None] == 0) | (blk[None] == cur[:, None]) | (blk[None] == cur[:, None] - 1)
    valid = blk[None] * SEL_BLOCK <= pos[:, None]
    sel_score = jnp.where(forced, SEL_FORCE, jnp.where(valid, p_sel, NEG_INF))
    n_top = min(SEL_TOPK, n_sel)
    _, sel_idx = lax.top_k(sel_score, n_top)
    ks = partial_rope(kv(k_sel), pos).reshape(b, n_sel, SEL_BLOCK, G, Dh).transpose(0, 3, 1, 2, 4)
    vs = kv(v_sel).reshape(b, n_sel, SEL_BLOCK, G, Dh).transpose(0, 3, 1, 2, 4)
    bi = jnp.arange(b)[:, None, None, None]
    gi = jnp.arange(G)[None, :, None, None]
    in_blk = jnp.arange(SEL_BLOCK)

    def sel_attend(args):
        qb, ib, tb = args
        kg = ks[bi, gi, ib]
        vg = vs[bi, gi, ib]
        scs = jnp.einsum('btgrd,bgtkud->bgrtku', qb, kg).astype(f32) * scale
        kpos = ib[..., None] * SEL_BLOCK + in_blk
        m = (kpos <= tb[None, None, :, None, None])[:, :, None]
        scs = jnp.where(m, scs, NEG_INF)
        t_, k_ = ib.shape[2], ib.shape[3]
        pr = jax.nn.softmax(scs.reshape(b, G, R, t_, k_ * SEL_BLOCK), axis=-1).reshape(scs.shape)
        return jnp.einsum('bgrtku,bgtkud->btgrd', pr.astype(vg.dtype), vg)

    nq = s // SEL_QBLOCK
    o_sel = lax.map(sel_attend, (jnp.moveaxis(q.reshape(b, nq, SEL_QBLOCK, G, R, Dh), 1, 0),
                                 jnp.moveaxis(sel_idx.reshape(b, G, nq, SEL_QBLOCK, n_top), 2, 0),
                                 pos.reshape(nq, SEL_QBLOCK)))
    o_sel = jnp.moveaxis(o_sel, 0, 1).reshape(b, s, G, R, Dh)

    qbk = ATTN_QBLOCK
    nb = s // qbk
    nw = WINDOW // qbk

    def banded(t):
        tp = jnp.pad(t, ((0, 0), (WINDOW, 0), (0, 0), (0, 0))).reshape(b, nb + nw, qbk, G, Dh)
        return jnp.concatenate([tp[:, j:j + nb] for j in range(nw + 1)], axis=2)

    kwb = banded(partial_rope(kv(k_win), pos))
    vwb = banded(kv(v_win))
    qpos = pos.reshape(nb, qbk)
    kpos_w = jnp.arange(nb)[:, None] * qbk - WINDOW + jnp.arange((nw + 1) * qbk)[None, :]
    kq = kpos_w[:, None, :]
    tq = qpos[:, :, None]
    wmask = (kq <= tq) & (kq > tq - WINDOW) & (kq >= 0)
    scw = jnp.einsum('bnqgrd,bnkgd->bngrqk', q.reshape(b, nb, qbk, G, R, Dh), kwb).astype(f32) * scale
    pw = jax.nn.softmax(jnp.where(wmask[None, :, None, None], scw, NEG_INF), axis=-1)
    o_win = jnp.einsum('bngrqk,bnkgd->bnqgrd', pw.astype(vwb.dtype), vwb).reshape(b, s, G, R, Dh)

    g = jax.nn.sigmoid(gates.astype(f32)).reshape(b, s, G, R, 3)
    o_nsa = g[..., 0:1] * o_cmp + g[..., 1:2] * o_sel + g[..., 2:3] * o_win
    o_nsa = o_nsa.astype(hn.dtype).reshape(b, s, NSA_HEADS * Dh)

    Hd, Dd = DIFF_HEADS, DIFF_QK_DIM
    dq = partial_rope(dq.reshape(b, s, Hd * 2, Dd), pos).reshape(b, s, Hd, 2, Dd)
    dk = partial_rope(dk.reshape(b, s, Hd * 2, Dd), pos).reshape(b, s, Hd, 2, Dd)
    dv = dv.reshape(b, s, Hd, DIFF_V_DIM)
    lamf = lam.astype(f32)
    lam_full = jnp.exp(jnp.sum(lamf[0] * lamf[1])) - jnp.exp(jnp.sum(lamf[2] * lamf[3])) + lambda_init
    dscale = Dd ** -0.5

    def diff_attend(args):
        qb, tb = args
        scd = jnp.einsum('bthmd,bshmd->bhmts', qb, dk).astype(f32) * dscale
        scd = jnp.where(pos[None, :] <= tb[:, None], scd, NEG_INF)
        pr = jax.nn.softmax(scd, axis=-1)
        w = pr[:, :, 0] - lam_full * pr[:, :, 1]
        return jnp.einsum('bhts,bshe->bthe', w.astype(dv.dtype), dv)

    o_diff = lax.map(diff_attend, (jnp.moveaxis(dq.reshape(b, nb, qbk, Hd, 2, Dd), 1, 0), qpos))
    o_diff = jnp.moveaxis(o_diff, 0, 1).reshape(b, s, Hd, DIFF_V_DIM)
    o_diff = (rms_norm(o_diff, subln) * (1.0 - lambda_init)).astype(hn.dtype).reshape(b, s, Hd * DIFF_V_DIM)
    return jnp.concatenate([o_nsa, o_diff], axis=-1) @ w_out


def hier_moe(hn, w_group, b_group, w_expert, b_expert, w_gate, w_up, w_down):
    b, s, d = hn.shape
    f32 = jnp.float32
    xt = hn.reshape(b * s, d)
    n = xt.shape[0]
    g_logits = (xt @ w_group).astype(f32) + b_group.astype(f32)
    g_prob = jax.nn.softmax(g_logits, axis=-1)
    g_top = jnp.argmax(g_logits, axis=-1)
    e_logits = ((xt @ w_expert).astype(f32) + b_expert.astype(f32)).reshape(n, MOE_GROUPS, MOE_EXPERTS_PER_GROUP)
    e_logits = jnp.take_along_axis(e_logits, g_top[:, None, None], axis=1)[:, 0]
    top_p, top_i = lax.top_k(jax.nn.softmax(e_logits, axis=-1), MOE_TOP_K)
    gate_w = jnp.take_along_axis(g_prob, g_top[:, None], axis=1) * top_p / jnp.sum(top_p, axis=-1, keepdims=True)
    expert_id = (g_top[:, None] * MOE_EXPERTS_PER_GROUP + top_i).reshape(-1)
    gate_w = gate_w.reshape(-1)
    n_assign = n * MOE_TOP_K
    token_id = jnp.arange(n_assign) // MOE_TOP_K
    order = jnp.argsort(expert_id)
    e_sorted = expert_id[order]
    counts = jnp.bincount(expert_id, length=MOE_EXPERTS)
    padded = (counts + MOE_BLOCK - 1) // MOE_BLOCK * MOE_BLOCK
    pad_end = jnp.cumsum(padded)
    slot = (pad_end - padded)[e_sorted] + jnp.arange(n_assign) - (jnp.cumsum(counts) - counts)[e_sorted]
    n_blocks = -(-(n_assign + MOE_EXPERTS * (MOE_BLOCK - 1)) // MOE_BLOCK)
    cap = n_blocks * MOE_BLOCK
    slot_tok = jnp.zeros((cap,), jnp.int32).at[slot].set(token_id[order])
    slot_w = jnp.zeros((cap,), f32).at[slot].set(gate_w[order])
    blk_expert = jnp.minimum(jnp.searchsorted(pad_end, jnp.arange(n_blocks) * MOE_BLOCK, side='right'),
                             MOE_EXPERTS - 1)

    def run(args):
        xb, e = args
        return (jax.nn.silu(xb @ w_gate[e]) * (xb @ w_up[e])) @ w_down[e]

    yb = lax.map(run, (xt[slot_tok].reshape(n_blocks, MOE_BLOCK, d), blk_expert)).reshape(cap, d)
    y = jnp.zeros_like(xt).at[slot_tok].add(yb * slot_w[:, None].astype(yb.dtype))
    return y.reshape(b, s, d)


def setup_inputs(seed: int = 0) -> dict:
    key = jax.random.key(seed)
    keys = iter(jax.random.split(key, 64))
    f32 = jnp.float32

    def normal(shape, scale):
        return jax.random.normal(next(keys), shape, f32) * scale

    def gain(shape):
        return 1.0 + normal(shape, 0.01)

    ne, no = (DEPTH + 1) // 2, DEPTH // 2
    dt0 = jnp.exp(jax.random.uniform(next(keys), (ne, SSD_HEADS), f32, math.log(1e-3), math.log(1e-1)))
    a0 = jax.random.uniform(next(keys), (ne, SSD_HEADS), f32, 1.0, 16.0)
    return {
        'x': normal((BATCH, SEQ, D_MODEL), 1.0),
        'p': normal((DEPTH, BATCH, SEQ, PLE_DIM), 1.0),
        'norm_mix': gain((DEPTH, D_MODEL)),
        'norm_ffn': gain((DEPTH, D_MODEL)),
        'norm_ple': gain((DEPTH, D_MODEL)),
        'norm_final': gain((D_MODEL,)),
        'ev_w_in': normal((ne, D_MODEL, EVEN_IN), D_MODEL ** -0.5),
        'ev_conv_w': normal((ne, SSD_CONV_WIDTH, SSD_CONV_CH), SSD_CONV_WIDTH ** -0.5),
        'ev_conv_b': normal((ne, SSD_CONV_CH), 0.01),
        'ev_dt_bias': dt0 + jnp.log(-jnp.expm1(-dt0)),
        'ev_a_log': jnp.log(a0),
        'ev_d_skip': gain((ne, SSD_HEADS)),
        'ev_gate_norm': gain((ne, SSD_INNER)),
        'ev_sc_w': normal((ne, SC_WIDTH, SC_CHANNELS), SC_WIDTH ** -0.5),
        'ev_w_out': normal((ne, EVEN_MIX, D_MODEL), EVEN_MIX ** -0.5),
        'od_w_in': normal((no, D_MODEL, ODD_IN), D_MODEL ** -0.5),
        'od_cmp_pos': normal((no, 2, CMP_BLOCK, HEAD_DIM), 0.02),
        'od_cmp_w1': normal((no, 2, CMP_BLOCK * HEAD_DIM, HEAD_DIM), (CMP_BLOCK * HEAD_DIM) ** -0.5),
        'od_cmp_w2': normal((no, 2, HEAD_DIM, HEAD_DIM), HEAD_DIM ** -0.5),
        'od_lambda': normal((no, 4, DIFF_QK_DIM), 0.1),
        'od_subln': gain((no, DIFF_V_DIM)),
        'od_w_out': normal((no, ODD_MIX, D_MODEL), ODD_MIX ** -0.5),
        'moe_w_group': normal((DEPTH, D_MODEL, MOE_GROUPS), D_MODEL ** -0.5),
        'moe_b_group': normal((DEPTH, MOE_GROUPS), 0.01),
        'moe_w_expert': normal((DEPTH, D_MODEL, MOE_EXPERTS), D_MODEL ** -0.5),
        'moe_b_expert': normal((DEPTH, MOE_EXPERTS), 0.01),
        'moe_w_gate': normal((DEPTH, MOE_EXPERTS, D_MODEL, MOE_FF), D_MODEL ** -0.5),
        'moe_w_up': normal((DEPTH, MOE_EXPERTS, D_MODEL, MOE_FF), D_MODEL ** -0.5),
        'moe_w_down': normal((DEPTH, MOE_EXPERTS, MOE_FF, D_MODEL), MOE_FF ** -0.5),
        'ple_gate': normal((DEPTH, D_MODEL, D_MODEL), D_MODEL ** -0.5),
        'ple_proj': normal((DEPTH, PLE_DIM, D_MODEL), PLE_DIM ** -0.5),
    }


def reference(x, p, norm_mix, norm_ffn, norm_ple, norm_final,
              ev_w_in, ev_conv_w, ev_conv_b, ev_dt_bias, ev_a_log, ev_d_skip, ev_gate_norm, ev_sc_w, ev_w_out,
              od_w_in, od_cmp_pos, od_cmp_w1, od_cmp_w2, od_lambda, od_subln, od_w_out,
              moe_w_group, moe_b_group, moe_w_expert, moe_b_expert, moe_w_gate, moe_w_up, moe_w_down,
              ple_gate, ple_proj):
    h = x
    for i in range(DEPTH):
        j = i // 2
        hn = rms_norm(h, norm_mix[i])
        if i % 2 == 0:
            mix = ssd_shortconv_mixer(hn, ev_w_in[j], ev_conv_w[j], ev_conv_b[j], ev_dt_bias[j], ev_a_log[j],
                                      ev_d_skip[j], ev_gate_norm[j], ev_sc_w[j], ev_w_out[j])
        else:
            lambda_init = 0.8 - 0.6 * math.exp(-0.3 * i)
            mix = nsa_diff_mixer(hn, od_w_in[j], od_cmp_pos[j], od_cmp_w1[j], od_cmp_w2[j], od_lambda[j],
                                 od_subln[j], od_w_out[j], lambda_init)
        h = h + mix
        h = h + hier_moe(rms_norm(h, norm_ffn[i]), moe_w_group[i], moe_b_group[i], moe_w_expert[i],
                         moe_b_expert[i], moe_w_gate[i], moe_w_up[i], moe_w_down[i])
        gate = jax.nn.sigmoid(rms_norm(h, norm_ple[i]) @ ple_gate[i])
        h = h + gate * (p[i] @ ple_proj[i])
    return rms_norm(h, norm_final)
```

```python
import functools
import math

import jax
import jax.numpy as jnp
from jax import lax
from jax.experimental import pallas as pl
from jax.experimental.pallas import tpu as pltpu

F32 = jnp.float32
BF16 = jnp.bfloat16
HIGHEST = lax.Precision.HIGHEST

LANES = 128
D_MODEL = 2048
NORM_EPS = 1e-6
ROPE_THETA = 500000.0
ROPE_HALF = 16
NEG_INF = -1e30

SSD_HEADS = 32
SSD_HEAD_DIM = 64
SSD_GROUPS = 8
SSD_STATE = 128
SSD_CHUNK = 128
SSD_INNER = 2048
CONV_HALO = 8

HEAD_DIM = 128
NSA_HEADS = 8
NSA_GROUPS = 2
NSA_REP = 4
CMP_STRIDE = 16
CMP_CHUNKS = 2
SEL_BLOCK = 64
SEL_TOPK = 16
SEL_FORCE = 1e4
WINDOW = 512
DIFF_HEADS = 4
DIFF_V = 256

MOE_GROUPS = 4
MOE_PER_GROUP = 8
MOE_EXPERTS = 32
MOE_FF = 1024
MOE_ROWS = 256
ROUTER_TILE = 512
PLE_DIM = 256

VMEM_LIMIT = 56 * 1024 * 1024


def _cparams(sem, vmem=VMEM_LIMIT):
    return pltpu.CompilerParams(dimension_semantics=sem, vmem_limit_bytes=vmem)


def _nt_dot(a, b):
    return lax.dot_general(a, b, (((1,), (1,)), ((), ())), preferred_element_type=F32)


def _silu(x):
    return x * jax.nn.sigmoid(x)


def _rmsnorm_kernel(x_ref, g_ref, o_ref):
    x = x_ref[...]
    ms = jnp.mean(x * x, axis=-1, keepdims=True)
    o_ref[...] = (x * lax.rsqrt(ms + NORM_EPS) * g_ref[...]).astype(o_ref.dtype)


def _rmsnorm(x, g, out_dtype, tm=512):
    n, d = x.shape
    return pl.pallas_call(
        _rmsnorm_kernel,
        grid=(n // tm,),
        in_specs=[pl.BlockSpec((tm, d), lambda i: (i, 0)), pl.BlockSpec((1, d), lambda i: (0, 0))],
        out_specs=pl.BlockSpec((tm, d), lambda i: (i, 0)),
        out_shape=jax.ShapeDtypeStruct((n, d), out_dtype),
        compiler_params=_cparams(("parallel",)),
        name="rmsnorm",
    )(x, g.reshape(1, d))


def _rope_apply(x, cosf, sinf, lane):
    swapped = jnp.where(lane < ROPE_HALF, pltpu.roll(x, LANES - ROPE_HALF, 1), pltpu.roll(x, ROPE_HALF, 1))
    return x * cosf + swapped * sinf


def _mm_kernel(a_ref, w_ref, o_ref):
    o_ref[...] = jnp.dot(a_ref[...], w_ref[...], preferred_element_type=F32).astype(o_ref.dtype)


def _mm_rope_kernel(a_ref, w_ref, cos_ref, sin_ref, o_ref):
    acc = jnp.dot(a_ref[...], w_ref[...], preferred_element_type=F32)
    tm, tn = acc.shape
    cosf = cos_ref[...]
    sinf = sin_ref[...]
    lane = lax.broadcasted_iota(jnp.int32, (tm, LANES), 1)
    for h in range(tn // LANES):
        sl = slice(h * LANES, (h + 1) * LANES)
        o_ref[:, sl] = _rope_apply(acc[:, sl], cosf, sinf, lane).astype(o_ref.dtype)


def _matmul(a, w, out_dtype, tm, tn, rope=None, seq=None):
    m, k = a.shape
    n = w.shape[1]
    in_specs = [pl.BlockSpec((tm, k), lambda i, j: (i, 0)), pl.BlockSpec((k, tn), lambda i, j: (0, j))]
    args = [a, w]
    body = _mm_kernel
    if rope is not None:
        per_seq = seq // tm
        in_specs += [pl.BlockSpec((tm, LANES), lambda i, j: (i % per_seq, 0))] * 2
        args += list(rope)
        body = _mm_rope_kernel
    return pl.pallas_call(
        body,
        grid=(m // tm, n // tn),
        in_specs=in_specs,
        out_specs=pl.BlockSpec((tm, tn), lambda i, j: (i, j)),
        out_shape=jax.ShapeDtypeStruct((m, n), out_dtype),
        compiler_params=_cparams(("parallel", "parallel")),
        name="proj_rope" if rope is not None else "proj",
    )(*args)


def _even_core_kernel(z_ref, xs_ref, bc_ref, scb_ref, scc_ref, sch_ref,
                      xs_h_ref, bc_h_ref, scc_h_ref, sch_h_ref, dt_ref,
                      cwx_ref, cwbc_ref, cbx_ref, cbbc_ref, dtb_ref, alog_ref, dskip_ref, gnorm_ref, scw_ref,
                      o_ref, state_ref, cbuf_ref, xsc_ref, bcc_ref, y_ref):
    q = SSD_CHUNK
    first = pl.program_id(1) == 0
    keep = jnp.where(first, 0.0, 1.0).astype(F32)

    @pl.when(first)
    def _():
        state_ref[...] = jnp.zeros_like(state_ref)

    strip = 512

    def causal_conv(load_main, load_halo, w_ref, width, finish):
        cbuf_ref[0:CONV_HALO, :] = load_halo() * keep
        cbuf_ref[CONV_HALO:CONV_HALO + q, :] = load_main()
        for c0 in range(0, SSD_INNER, strip):
            cs = slice(c0, c0 + strip)
            acc = None
            for k in range(width):
                r0 = CONV_HALO - (width - 1) + k
                term = w_ref[k:k + 1, cs] * cbuf_ref[r0:r0 + q, cs]
                acc = term if acc is None else acc + term
            finish(cs, acc)

    def fin_xs(cs, acc):
        xsc_ref[:, cs] = _silu(acc + cbx_ref[:, cs])

    def fin_bc(cs, acc):
        bcc_ref[:, cs] = _silu(acc + cbbc_ref[:, cs])

    causal_conv(lambda: xs_ref[...], lambda: xs_h_ref[...], cwx_ref, 4, fin_xs)
    causal_conv(lambda: bc_ref[...], lambda: bc_h_ref[...], cwbc_ref, 4, fin_bc)

    def fin_sc(cs, acc):
        o_ref[:, SSD_INNER + cs.start:SSD_INNER + cs.stop] = (scb_ref[:, cs] * acc).astype(o_ref.dtype)

    causal_conv(lambda: scc_ref[...] * sch_ref[...], lambda: scc_h_ref[...] * sch_h_ref[...], scw_ref, 3, fin_sc)

    dt = jax.nn.softplus(dt_ref[...] + dtb_ref[...])
    d_a = dt * (-jnp.exp(alog_ref[...]))
    row = lax.broadcasted_iota(jnp.int32, (q, q), 0)
    col = lax.broadcasted_iota(jnp.int32, (q, q), 1)
    causal = row >= col
    tril = jnp.where(causal, 1.0, 0.0).astype(F32)
    a_cum = jnp.dot(tril, d_a, preferred_element_type=F32, precision=HIGHEST)
    a_last = a_cum[q - 1:q, :]
    decay_end = jnp.exp(a_last - a_cum)
    chunk_decay = jnp.exp(a_last)
    exp_acum = jnp.exp(a_cum)
    a_cum_t = a_cum.T
    lane = lax.broadcasted_iota(jnp.int32, (q, LANES), 1)
    lo = lane < SSD_HEAD_DIM
    lane_row = lax.broadcasted_iota(jnp.int32, (1, LANES), 1)
    lo_row = lane_row < SSD_HEAD_DIM

    def pair_cols(mat, h0):
        return jnp.where(lo, mat[:, h0:h0 + 1], mat[:, h0 + 1:h0 + 2])

    for g in range(SSD_GROUPS):
        gs = slice(g * SSD_STATE, (g + 1) * SSD_STATE)
        b_g = bcc_ref[:, gs]
        c_g = bcc_ref[:, SSD_GROUPS * SSD_STATE + g * SSD_STATE:SSD_GROUPS * SSD_STATE + (g + 1) * SSD_STATE]
        b_gt = b_g.T.astype(BF16)
        c_gb = c_g.astype(BF16)
        cb = jnp.dot(c_gb, b_gt, preferred_element_type=F32)
        for hp in range(2):
            pr = g * 2 + hp
            h0 = 2 * pr
            ps = slice(pr * LANES, (pr + 1) * LANES)
            xp = xsc_ref[:, ps]
            xdt = xp * pair_cols(dt, h0)
            mats = []
            for h in (h0, h0 + 1):
                seg = a_cum[:, h:h + 1] - a_cum_t[h:h + 1, :]
                dec = jnp.where(causal, jnp.exp(jnp.minimum(seg, 0.0)), 0.0)
                mats.append((cb * dec).astype(BF16))
            lhs = jnp.concatenate(mats, axis=1)
            rhs = jnp.concatenate([jnp.where(lo, xdt, 0.0), jnp.where(lo, 0.0, xdt)], axis=0).astype(BF16)
            y = jnp.dot(lhs, rhs, preferred_element_type=F32)
            st = state_ref[pr]
            y = y + jnp.dot(c_gb, st.astype(BF16), preferred_element_type=F32) * pair_cols(exp_acum, h0)
            xw = (xdt * pair_cols(decay_end, h0)).astype(BF16)
            cd = jnp.where(lo_row, chunk_decay[:, h0:h0 + 1], chunk_decay[:, h0 + 1:h0 + 2])
            state_ref[pr] = st * cd + jnp.dot(b_gt, xw, preferred_element_type=F32)
            y_ref[:, ps] = y + dskip_ref[:, ps] * xp

    gw = SSD_INNER // SSD_GROUPS
    for g in range(SSD_GROUPS):
        cs = slice(g * gw, (g + 1) * gw)
        yg = y_ref[:, cs] * _silu(z_ref[:, cs])
        ms = jnp.mean(yg * yg, axis=-1, keepdims=True)
        o_ref[:, cs] = (yg * lax.rsqrt(ms + NORM_EPS) * gnorm_ref[:, cs]).astype(o_ref.dtype)


def _even_core(proj, dt_raw, batch, seq, conv_w, conv_b, dt_bias, a_log, d_skip, gate_norm, sc_w):
    q = SSD_CHUNK
    nc = seq // q
    w = SSD_INNER
    hb = q // CONV_HALO

    def main(cb):
        return pl.BlockSpec((q, w), lambda b, c: (b * nc + c, cb))

    def halo(cb):
        return pl.BlockSpec((CONV_HALO, w), lambda b, c: (jnp.maximum((b * nc + c) * hb - 1, 0), cb))

    def full(shape):
        return pl.BlockSpec(shape, lambda b, c: (0, 0))

    pad = LANES - SSD_HEADS
    dtb = jnp.pad(dt_bias, (0, pad)).reshape(1, LANES)
    alog = jnp.pad(a_log, (0, pad)).reshape(1, LANES)
    dskip = jnp.repeat(d_skip, SSD_HEAD_DIM).reshape(1, w)
    in_specs = [main(0), main(1), main(2), main(3), main(4), main(5),
                halo(1), halo(2), halo(4), halo(5),
                pl.BlockSpec((q, LANES), lambda b, c: (b * nc + c, 0)),
                full((4, w)), full((4, w)), full((1, w)), full((1, w)),
                full((1, LANES)), full((1, LANES)), full((1, w)), full((1, w)), full((3, w))]
    return pl.pallas_call(
        _even_core_kernel,
        grid=(batch, nc),
        in_specs=in_specs,
        out_specs=pl.BlockSpec((q, 2 * w), lambda b, c: (b * nc + c, 0)),
        out_shape=jax.ShapeDtypeStruct((batch * seq, 2 * w), BF16),
        scratch_shapes=[pltpu.VMEM((SSD_HEADS // 2, SSD_STATE, LANES), F32),
                        pltpu.VMEM((CONV_HALO + q, w), F32),
                        pltpu.VMEM((q, w), F32), pltpu.VMEM((q, w), F32), pltpu.VMEM((q, w), F32)],
        compiler_params=_cparams(("parallel", "arbitrary")),
        name="ssd_conv_core",
    )(proj, proj, proj, proj, proj, proj, proj, proj, proj, proj, dt_raw,
      conv_w[:, :w], conv_w[:, w:], conv_b[:w].reshape(1, w), conv_b[w:].reshape(1, w),
      dtb, alog, dskip, gate_norm.reshape(1, w), sc_w)


def _outproj_kernel(*refs, n_in):
    mix = refs[:n_in]
    ws = refs[n_in:2 * n_in]
    h_ref, g_ref, wr_ref, br_ref, hnew_ref, hn_ref, lg_ref = refs[2 * n_in:]
    acc = h_ref[...]
    for m_ref, w_ref in zip(mix, ws):
        acc = acc + jnp.dot(m_ref[...], w_ref[...], preferred_element_type=F32)
    hnew_ref[...] = acc
    ms = jnp.mean(acc * acc, axis=-1, keepdims=True)
    hn = acc * lax.rsqrt(ms + NORM_EPS) * g_ref[...]
    hn_ref[...] = hn.astype(hn_ref.dtype)
    lg_ref[...] = jnp.dot(hn, wr_ref[...], preferred_element_type=F32, precision=HIGHEST) + br_ref[...]


def _outproj(mixes, ws, h, g, wr, br, tm=256):
    n, d = h.shape
    n_in = len(mixes)
    in_specs = ([pl.BlockSpec((tm, m.shape[1]), lambda i: (i, 0)) for m in mixes]
                + [pl.BlockSpec(w.shape, lambda i: (0, 0)) for w in ws]
                + [pl.BlockSpec((tm, d), lambda i: (i, 0)), pl.BlockSpec((1, d), lambda i: (0, 0)),
                   pl.BlockSpec((d, LANES), lambda i: (0, 0)), pl.BlockSpec((1, LANES), lambda i: (0, 0))])
    return pl.pallas_call(
        functools.partial(_outproj_kernel, n_in=n_in),
        grid=(n // tm,),
        in_specs=in_specs,
        out_specs=[pl.BlockSpec((tm, d), lambda i: (i, 0)), pl.BlockSpec((tm, d), lambda i: (i, 0)),
                   pl.BlockSpec((tm, LANES), lambda i: (i, 0))],
        out_shape=[jax.ShapeDtypeStruct((n, d), F32), jax.ShapeDtypeStruct((n, d), BF16),
                   jax.ShapeDtypeStruct((n, LANES), F32)],
        compiler_params=_cparams(("parallel",)),
        name="outproj_norm_router",
    )(*mixes, *ws, h, g.reshape(1, d), wr, br)


ROUTE_E1, ROUTE_E2, ROUTE_W1, ROUTE_W2, ROUTE_R1, ROUTE_R2 = range(6)


def _router_kernel(lg_ref, out_ref, cnt_ref, run_ref):
    @pl.when(pl.program_id(0) == 0)
    def _():
        run_ref[...] = jnp.zeros_like(run_ref)

    x = lg_ref[...]
    t = x.shape[0]
    lane = lax.broadcasted_iota(jnp.int32, (t, LANES), 1).astype(F32)
    far = float(LANES)
    gmask = lane < MOE_GROUPS
    gl = jnp.where(gmask, x, NEG_INF)
    gmax = jnp.max(gl, axis=-1, keepdims=True)
    g_top = jnp.min(jnp.where(gl == gmax, lane, far), axis=-1, keepdims=True)
    gsum = jnp.sum(jnp.where(gmask, jnp.exp(gl - gmax), 0.0), axis=-1, keepdims=True)
    g_prob = 1.0 / gsum
    lo = MOE_GROUPS + MOE_PER_GROUP * g_top
    emask = (lane >= lo) & (lane < lo + MOE_PER_GROUP)
    el = jnp.where(emask, x, NEG_INF)
    emax = jnp.max(el, axis=-1, keepdims=True)
    ee = jnp.where(emask, jnp.exp(el - emax), 0.0)
    ep = jnp.where(emask, ee / jnp.sum(ee, axis=-1, keepdims=True), -1.0)
    p1 = jnp.max(ep, axis=-1, keepdims=True)
    i1 = jnp.min(jnp.where(ep == p1, lane, far), axis=-1, keepdims=True)
    ep2 = jnp.where(lane == i1, -1.0, ep)
    p2 = jnp.max(ep2, axis=-1, keepdims=True)
    i2 = jnp.min(jnp.where(ep2 == p2, lane, far), axis=-1, keepdims=True)
    den = p1 + p2
    w1 = g_prob * p1 / den
    w2 = g_prob * p2 / den
    oh1 = jnp.where(lane == i1, 1.0, 0.0)
    oh2 = jnp.where(lane == i2, 1.0, 0.0)
    oh = oh1 + oh2
    row = lax.broadcasted_iota(jnp.int32, (t, t), 0)
    col = lax.broadcasted_iota(jnp.int32, (t, t), 1)
    strict = jnp.where(row > col, 1.0, 0.0).astype(BF16)
    before = jnp.dot(strict, oh.astype(BF16), preferred_element_type=F32) + run_ref[...]
    r1 = jnp.sum(oh1 * before, axis=-1, keepdims=True)
    r2 = jnp.sum(oh2 * before, axis=-1, keepdims=True)
    run_ref[...] = run_ref[...] + jnp.sum(oh, axis=0, keepdims=True)
    cnt_ref[...] = run_ref[...]
    packed = jnp.zeros((t, LANES), F32)
    for k, v in ((ROUTE_E1, i1 - MOE_GROUPS), (ROUTE_E2, i2 - MOE_GROUPS), (ROUTE_W1, w1), (ROUTE_W2, w2),
                 (ROUTE_R1, r1), (ROUTE_R2, r2)):
        packed = jnp.where(lane == k, v, packed)
    out_ref[...] = packed


def _router(logits, tt=ROUTER_TILE):
    n = logits.shape[0]
    return pl.pallas_call(
        _router_kernel,
        grid=(n // tt,),
        in_specs=[pl.BlockSpec((tt, LANES), lambda i: (i, 0))],
        out_specs=[pl.BlockSpec((tt, LANES), lambda i: (i, 0)), pl.BlockSpec((1, LANES), lambda i: (0, 0))],
        out_shape=[jax.ShapeDtypeStruct((n, LANES), F32), jax.ShapeDtypeStruct((1, LANES), F32)],
        scratch_shapes=[pltpu.VMEM((1, LANES), F32)],
        compiler_params=_cparams(("arbitrary",)),
        name="router_rank",
    )(logits)


def _moe_kernel(be_ref, bv_ref, x_ref, wg_ref, wu_ref, wd_ref, o_ref):
    i = pl.program_id(0)

    @pl.when(bv_ref[i] > 0)
    def _():
        x = x_ref[...]
        step = 512
        for n0 in range(0, MOE_FF, step):
            gate = jnp.dot(x, wg_ref[:, n0:n0 + step], preferred_element_type=F32)
            up = jnp.dot(x, wu_ref[:, n0:n0 + step], preferred_element_type=F32)
            hid = (_silu(gate) * up).astype(BF16)
            part = jnp.dot(hid, wd_ref[n0:n0 + step, :], preferred_element_type=F32)
            if n0 == 0:
                o_ref[...] = part
            else:
                o_ref[...] += part

    @pl.when(bv_ref[i] == 0)
    def _():
        o_ref[...] = jnp.zeros_like(o_ref)


def _moe_ffn(xs, blk_expert, blk_valid, wg, wu, wd):
    cap, d = xs.shape
    nb = cap // MOE_ROWS
    ff = wg.shape[2]
    grid_spec = pltpu.PrefetchScalarGridSpec(
        num_scalar_prefetch=2,
        grid=(nb,),
        in_specs=[pl.BlockSpec((MOE_ROWS, d), lambda i, be, bv: (i, 0)),
                  pl.BlockSpec((None, d, ff), lambda i, be, bv: (be[i], 0, 0)),
                  pl.BlockSpec((None, d, ff), lambda i, be, bv: (be[i], 0, 0)),
                  pl.BlockSpec((None, ff, d), lambda i, be, bv: (be[i], 0, 0))],
        out_specs=pl.BlockSpec((MOE_ROWS, d), lambda i, be, bv: (i, 0)),
    )
    return pl.pallas_call(
        _moe_kernel,
        grid_spec=grid_spec,
        out_shape=jax.ShapeDtypeStruct((cap, d), F32),
        compiler_params=_cparams(("arbitrary",)),
        name="moe_ffn",
    )(blk_expert, blk_valid, xs, wg, wu, wd)


def _moe(hn, logits, wg, wu, wd):
    n, d = hn.shape
    routed, counts = _router(logits)
    e1 = routed[:, ROUTE_E1].astype(jnp.int32)
    e2 = routed[:, ROUTE_E2].astype(jnp.int32)
    counts = counts[0, MOE_GROUPS:MOE_GROUPS + MOE_EXPERTS].astype(jnp.int32)
    padded = (counts + MOE_ROWS - 1) // MOE_ROWS * MOE_ROWS
    pad_end = jnp.cumsum(padded)
    pad_start = pad_end - padded
    slot1 = pad_start[e1] + routed[:, ROUTE_R1].astype(jnp.int32)
    slot2 = pad_start[e2] + routed[:, ROUTE_R2].astype(jnp.int32)
    nb = -(-(2 * n + MOE_EXPERTS * (MOE_ROWS - 1)) // MOE_ROWS)
    cap = nb * MOE_ROWS
    tok = jnp.arange(n, dtype=jnp.int32)
    slot_tok = jnp.zeros((cap,), jnp.int32).at[slot1].set(tok).at[slot2].set(tok)
    blk_start = jnp.arange(nb, dtype=jnp.int32) * MOE_ROWS
    blk_valid = (blk_start < pad_end[-1]).astype(jnp.int32)
    last_valid = jnp.maximum(pad_end[-1] // MOE_ROWS - 1, 0)
    blk_expert = jnp.minimum(jnp.searchsorted(pad_end, blk_start, side="right"), MOE_EXPERTS - 1).astype(jnp.int32)
    blk_expert = jnp.where(blk_valid > 0, blk_expert, blk_expert[last_valid])
    xs = jnp.take(hn, slot_tok, axis=0)
    yb = _moe_ffn(xs, blk_expert, blk_valid, wg, wu, wd)
    y = (jnp.take(yb, slot1, axis=0) * routed[:, ROUTE_W1:ROUTE_W1 + 1]
         + jnp.take(yb, slot2, axis=0) * routed[:, ROUTE_W2:ROUTE_W2 + 1])
    return y


def _ple_kernel(h_ref, y_ref, p_ref, gp_ref, wg_ref, wp_ref, gn_ref, hout_ref, nout_ref):
    h = h_ref[...] + y_ref[...]
    ms = jnp.mean(h * h, axis=-1, keepdims=True)
    hn = (h * lax.rsqrt(ms + NORM_EPS) * gp_ref[...]).astype(BF16)
    gate = jax.nn.sigmoid(jnp.dot(hn, wg_ref[...], preferred_element_type=F32))
    emb = jnp.dot(p_ref[...].astype(BF16), wp_ref[...], preferred_element_type=F32)
    h = h + gate * emb
    hout_ref[...] = h
    ms = jnp.mean(h * h, axis=-1, keepdims=True)
    nout_ref[...] = (h * lax.rsqrt(ms + NORM_EPS) * gn_ref[...]).astype(nout_ref.dtype)


def _ple(h, y, p, g_ple, w_gate, w_proj, g_next, next_dtype, tm=256):
    n, d = h.shape
    row = lambda i: (i, 0)
    fixed = lambda i: (0, 0)
    return pl.pallas_call(
        _ple_kernel,
        grid=(n // tm,),
        in_specs=[pl.BlockSpec((tm, d), row), pl.BlockSpec((tm, d), row), pl.BlockSpec((tm, PLE_DIM), row),
                  pl.BlockSpec((1, d), fixed), pl.BlockSpec((d, d), fixed), pl.BlockSpec((PLE_DIM, d), fixed),
                  pl.BlockSpec((1, d), fixed)],
        out_specs=[pl.BlockSpec((tm, d), row), pl.BlockSpec((tm, d), row)],
        out_shape=[jax.ShapeDtypeStruct((n, d), F32), jax.ShapeDtypeStruct((n, d), next_dtype)],
        compiler_params=_cparams(("parallel",)),
        name="ple_norm",
    )(h, y, p, g_ple.reshape(1, d), w_gate, w_proj, g_next.reshape(1, d))


def _compress_kernel(x_ref, pos_ref, w1_ref, w2_ref, cos_ref, sin_ref, o_ref):
    nchunk = x_ref.shape[0]
    width = 2 * NSA_GROUPS * HEAD_DIM
    lane = lax.broadcasted_iota(jnp.int32, (nchunk, LANES), 1)
    for kv in range(2):
        for g in range(NSA_GROUPS):
            acc = [jnp.zeros((nchunk, HEAD_DIM), F32) for _ in range(CMP_CHUNKS)]
            for half in range(CMP_CHUNKS):
                for j in range(CMP_STRIDE):
                    c0 = j * width + kv * NSA_GROUPS * HEAD_DIM + g * HEAD_DIM
                    jj = half * CMP_STRIDE + j
                    blk = (x_ref[:, c0:c0 + HEAD_DIM] + pos_ref[kv, jj:jj + 1, :]).astype(BF16)
                    acc[half] = acc[half] + jnp.dot(blk, w1_ref[kv, jj * HEAD_DIM:(jj + 1) * HEAD_DIM, :],
                                                    preferred_element_type=F32)
            pre = acc[0] + pltpu.roll(acc[1], nchunk - 1, 0)
            out = jnp.dot(_silu(pre).astype(BF16), w2_ref[kv], preferred_element_type=F32)
            if kv == 0:
                out = _rope_apply(out, cos_ref[...], sin_ref[...], lane)
            o_ref[kv * NSA_GROUPS + g] = out.astype(o_ref.dtype)


def _compress(kv_cmp, batch, seq, cmp_pos, cmp_w1, cmp_w2, cos_c, sin_c):
    nchunk = seq // CMP_STRIDE
    width = kv_cmp.shape[1]
    x = kv_cmp.reshape(batch * nchunk, CMP_STRIDE * width)
    return pl.pallas_call(
        _compress_kernel,
        grid=(batch,),
        in_specs=[pl.BlockSpec((nchunk, CMP_STRIDE * width), lambda b: (b, 0)),
                  pl.BlockSpec(cmp_pos.shape, lambda b: (0, 0, 0)),
                  pl.BlockSpec(cmp_w1.shape, lambda b: (0, 0, 0)),
                  pl.BlockSpec(cmp_w2.shape, lambda b: (0, 0, 0)),
                  pl.BlockSpec((nchunk, LANES), lambda b: (0, 0)),
                  pl.BlockSpec((nchunk, LANES), lambda b: (0, 0))],
        out_specs=pl.BlockSpec((None, 2 * NSA_GROUPS, nchunk, HEAD_DIM), lambda b: (b, 0, 0, 0)),
        out_shape=jax.ShapeDtypeStruct((batch, 2 * NSA_GROUPS, nchunk, HEAD_DIM), BF16),
        compiler_params=_cparams(("parallel",)),
        name="nsa_compress",
    )(x, cmp_pos, cmp_w1.astype(BF16), cmp_w2.astype(BF16), cos_c, sin_c)


NSA_TQ = 128
NSA_TK = 512


def _nsa_kernel(q_ref, kc_ref, vc_ref, ksel_ref, vsel_ref, kwin_ref, vwin_ref, gate_ref, pool_ref, expand_ref,
                o_ref):
    tq = NSA_TQ
    rep = NSA_REP
    scale = HEAD_DIM ** -0.5
    q0 = pl.program_id(2) * tq
    q = q_ref[...]
    qs = jnp.concatenate([q[:, r * HEAD_DIM:(r + 1) * HEAD_DIM] for r in range(rep)], axis=0)
    tpos = q0 + lax.broadcasted_iota(jnp.int32, (tq, 1), 0)

    def rep_rows(a):
        return jnp.concatenate([a] * rep, axis=0)

    ncmp = kc_ref.shape[0]
    cend = lax.broadcasted_iota(jnp.int32, (1, ncmp), 1) * CMP_STRIDE + (CMP_CHUNKS * CMP_STRIDE - 1)
    cbias = rep_rows(jnp.where(cend <= tpos, 0.0, NEG_INF).astype(F32))
    s = _nt_dot(qs, kc_ref[...]) * scale + cbias
    m = jnp.max(s, axis=-1, keepdims=True)
    p = jnp.where(cbias == 0.0, jnp.exp(s - m), 0.0)
    l = jnp.sum(p, axis=-1, keepdims=True)
    p = p * jnp.where(l > 0.0, 1.0 / l, 0.0)
    o_cmp = jnp.dot(p.astype(BF16), vc_ref[...], preferred_element_type=F32)
    imp = p[0:tq]
    for r in range(1, rep):
        imp = imp + p[r * tq:(r + 1) * tq]
    p_sel = jnp.dot(imp, pool_ref[...], preferred_element_type=F32, precision=HIGHEST)

    nsel = ksel_ref.shape[0] // SEL_BLOCK
    blk = lax.broadcasted_iota(jnp.int32, (tq, LANES), 1)
    cur = tpos // SEL_BLOCK
    forced = (blk == 0) | (blk == cur) | (blk == cur - 1)
    valid = blk * SEL_BLOCK <= tpos
    score = jnp.where(forced, SEL_FORCE, jnp.where(valid, p_sel, NEG_INF))
    score = jnp.where(blk < nsel, score, 3.0 * NEG_INF)
    st = score.T
    sub = 8
    chunks = [st[c * sub:(c + 1) * sub, :] for c in range(nsel // sub)]
    cnt = [jnp.zeros((sub, tq), F32) for _ in chunks]
    jrow = lax.broadcasted_iota(jnp.int32, (sub, tq), 0)
    for i in range(nsel):
        ci, ii = divmod(i, sub)
        si = chunks[ci][ii:ii + 1, :]
        for c in range(len(chunks)):
            if c > ci:
                beats = si >= chunks[c]
            elif c < ci:
                beats = si > chunks[c]
            else:
                beats = jnp.where(jrow > ii, jnp.where(si >= chunks[c], 1.0, 0.0), jnp.where(si > chunks[c], 1.0, 0.0)) > 0.5
            cnt[c] = cnt[c] + jnp.where(beats, 1.0, 0.0)
    sel_t = jnp.concatenate([jnp.where(c < min(SEL_TOPK, nsel), 1.0, 0.0) for c in cnt]
                            + [jnp.zeros((LANES - nsel, tq), F32)], axis=0)
    sel = sel_t.T.astype(BF16)

    tk = NSA_TK
    nkt = q0 // tk + 1

    def sel_body(kt, carry):
        m_i, l_i, acc = carry
        k0 = pl.multiple_of(kt * tk, tk)
        kt_ = ksel_ref[pl.ds(k0, tk), :]
        vt_ = vsel_ref[pl.ds(k0, tk), :]
        chosen = jnp.dot(sel, expand_ref[:, pl.ds(k0, tk)], preferred_element_type=F32)
        kpos = k0 + lax.broadcasted_iota(jnp.int32, (1, tk), 1)
        bias = rep_rows(jnp.where((chosen > 0.5) & (kpos <= tpos), 0.0, NEG_INF).astype(F32))
        sc = _nt_dot(qs, kt_) * scale + bias
        m_new = jnp.maximum(m_i, jnp.max(sc, axis=-1, keepdims=True))
        alpha = jnp.exp(m_i - m_new)
        pe = jnp.exp(sc - m_new)
        l_new = alpha * l_i + jnp.sum(pe, axis=-1, keepdims=True)
        acc_new = alpha * acc + jnp.dot(pe.astype(BF16), vt_, preferred_element_type=F32)
        return m_new, l_new, acc_new

    init = (jnp.full((rep * tq, 1), NEG_INF, F32), jnp.zeros((rep * tq, 1), F32), jnp.zeros((rep * tq, HEAD_DIM), F32))
    _, l_s, acc_s = lax.fori_loop(0, nkt, sel_body, init)
    o_sel = acc_s / l_s

    wk = WINDOW + tq
    w0 = pl.multiple_of(jnp.maximum(q0 - WINDOW, 0), tq)
    kw = kwin_ref[pl.ds(w0, wk), :]
    vw = vwin_ref[pl.ds(w0, wk), :]
    kpos = w0 + lax.broadcasted_iota(jnp.int32, (1, wk), 1)
    wbias = rep_rows(jnp.where((kpos <= tpos) & (kpos > tpos - WINDOW), 0.0, NEG_INF).astype(F32))
    sw = _nt_dot(qs, kw) * scale + wbias
    pw = jnp.exp(sw - jnp.max(sw, axis=-1, keepdims=True))
    o_win = jnp.dot(pw.astype(BF16), vw, preferred_element_type=F32) / jnp.sum(pw, axis=-1, keepdims=True)

    gate = jax.nn.sigmoid(gate_ref[...])
    for r in range(rep):
        rs = slice(r * tq, (r + 1) * tq)
        o = (gate[:, 3 * r:3 * r + 1] * o_cmp[rs] + gate[:, 3 * r + 1:3 * r + 2] * o_sel[rs]
             + gate[:, 3 * r + 2:3 * r + 3] * o_win[rs])
        o_ref[:, r * HEAD_DIM:(r + 1) * HEAD_DIM] = o.astype(o_ref.dtype)


def _nsa(roped, plain, cmp_kv, gates, batch, seq, pool, expand):
    tq = NSA_TQ
    nq = seq // tq
    gw = NSA_REP * HEAD_DIM
    ksel_cb = NSA_HEADS
    kwin_cb = NSA_HEADS + NSA_GROUPS
    ncmp = seq // CMP_STRIDE
    seq_blk = lambda cb: pl.BlockSpec((seq, HEAD_DIM), functools.partial(lambda b, g, t, cb: (b, cb + g), cb=cb))
    return pl.pallas_call(
        _nsa_kernel,
        grid=(batch, NSA_GROUPS, nq),
        in_specs=[pl.BlockSpec((tq, gw), lambda b, g, t: (b * nq + t, g)),
                  pl.BlockSpec((None, None, ncmp, HEAD_DIM), lambda b, g, t: (b, g, 0, 0)),
                  pl.BlockSpec((None, None, ncmp, HEAD_DIM), lambda b, g, t: (b, NSA_GROUPS + g, 0, 0)),
                  seq_blk(ksel_cb), seq_blk(0), seq_blk(kwin_cb), seq_blk(NSA_GROUPS),
                  pl.BlockSpec((tq, LANES), lambda b, g, t: (b * nq + t, g)),
                  pl.BlockSpec(pool.shape, lambda b, g, t: (0, 0)),
                  pl.BlockSpec(expand.shape, lambda b, g, t: (0, 0))],
        out_specs=pl.BlockSpec((tq, gw), lambda b, g, t: (b * nq + t, g)),
        out_shape=jax.ShapeDtypeStruct((batch * seq, NSA_HEADS * HEAD_DIM), BF16),
        compiler_params=_cparams(("parallel", "parallel", "arbitrary")),
        name="nsa_attention",
    )(roped, cmp_kv, cmp_kv, roped, plain, roped, plain, gates, pool, expand)


DIFF_TQ = 256
DIFF_TK = 512


def _diff_kernel(q_ref, k_ref, v_ref, lam_ref, subln_ref, o_ref, *, lambda_init):
    tq, tk = DIFF_TQ, DIFF_TK
    scale = HEAD_DIM ** -0.5
    q0 = pl.program_id(2) * tq
    q = q_ref[...]
    q1 = q[:, :HEAD_DIM]
    q2 = q[:, HEAD_DIM:]
    tpos = q0 + lax.broadcasted_iota(jnp.int32, (tq, 1), 0)
    nkt = (q0 + tq - 1) // tk + 1

    def body(kt, carry):
        m_i, l_i, acc = carry
        k0 = pl.multiple_of(kt * tk, tk)
        k = k_ref[pl.ds(k0, tk), :]
        v = v_ref[pl.ds(k0, tk), :]
        kpos = k0 + lax.broadcasted_iota(jnp.int32, (1, tk), 1)
        bias = jnp.where(kpos <= tpos, 0.0, NEG_INF).astype(F32)
        s1 = _nt_dot(q1, k[:, :HEAD_DIM]) * scale + bias
        s2 = _nt_dot(q2, k[:, HEAD_DIM:]) * scale + bias
        sc = jnp.concatenate([s1, s2], axis=0)
        m_new = jnp.maximum(m_i, jnp.max(sc, axis=-1, keepdims=True))
        alpha = jnp.exp(m_i - m_new)
        pe = jnp.exp(sc - m_new)
        l_new = alpha * l_i + jnp.sum(pe, axis=-1, keepdims=True)
        acc_new = alpha * acc + jnp.dot(pe.astype(BF16), v, preferred_element_type=F32)
        return m_new, l_new, acc_new

    init = (jnp.full((2 * tq, 1), NEG_INF, F32), jnp.zeros((2 * tq, 1), F32), jnp.zeros((2 * tq, DIFF_V), F32))
    _, l_f, acc = lax.fori_loop(0, nkt, body, init)
    o = acc / l_f
    lam = lam_ref[...]
    lam_full = (jnp.exp(jnp.sum(lam[0:1] * lam[1:2], axis=-1, keepdims=True))
                - jnp.exp(jnp.sum(lam[2:3] * lam[3:4], axis=-1, keepdims=True)) + lambda_init)
    od = o[:tq] - lam_full * o[tq:]
    ms = jnp.mean(od * od, axis=-1, keepdims=True)
    o_ref[...] = (od * lax.rsqrt(ms + NORM_EPS) * subln_ref[...] * (1.0 - lambda_init)).astype(o_ref.dtype)


def _diff_attn(roped, plain, lam, subln, batch, seq, lambda_init):
    tq = DIFF_TQ
    nq = seq // tq
    pair = 2 * HEAD_DIM
    dq_cb = (NSA_HEADS + 2 * NSA_GROUPS) * HEAD_DIM // pair
    dk_cb = dq_cb + DIFF_HEADS
    dv_cb = 2 * NSA_GROUPS * HEAD_DIM // DIFF_V
    return pl.pallas_call(
        functools.partial(_diff_kernel, lambda_init=lambda_init),
        grid=(batch, DIFF_HEADS, nq),
        in_specs=[pl.BlockSpec((tq, pair), lambda b, h, t: (b * nq + t, dq_cb + h)),
                  pl.BlockSpec((seq, pair), lambda b, h, t: (b, dk_cb + h)),
                  pl.BlockSpec((seq, DIFF_V), lambda b, h, t: (b, dv_cb + h)),
                  pl.BlockSpec(lam.shape, lambda b, h, t: (0, 0)),
                  pl.BlockSpec((1, DIFF_V), lambda b, h, t: (0, 0))],
        out_specs=pl.BlockSpec((tq, DIFF_V), lambda b, h, t: (b * nq + t, h)),
        out_shape=jax.ShapeDtypeStruct((batch * seq, DIFF_HEADS * DIFF_V), BF16),
        compiler_params=_cparams(("parallel", "parallel", "arbitrary")),
        name="diff_attention",
    )(roped, roped, plain, lam, subln.reshape(1, DIFF_V))


def _rope_tables(pos):
    inv_freq = ROPE_THETA ** (-jnp.arange(ROPE_HALF, dtype=F32) / ROPE_HALF)
    ang = pos.astype(F32)[:, None] * inv_freq[None, :]
    cos, sin = jnp.cos(ang), jnp.sin(ang)
    rest = LANES - 2 * ROPE_HALF
    n = pos.shape[0]
    cosf = jnp.concatenate([cos, cos, jnp.ones((n, rest), F32)], axis=1)
    sinf = jnp.concatenate([-sin, sin, jnp.zeros((n, rest), F32)], axis=1)
    return cosf, sinf


def _router_weights(w_group, b_group, w_expert, b_expert):
    pad = LANES - MOE_GROUPS - MOE_EXPERTS
    wr = jnp.concatenate([w_group, w_expert, jnp.zeros((w_group.shape[0], pad), F32)], axis=1)
    br = jnp.concatenate([b_group, b_expert, jnp.zeros((pad,), F32)]).reshape(1, LANES)
    return wr, br


def _even_layer(h, hn, batch, seq, w_in, conv_w, conv_b, dt_bias, a_log, d_skip, gate_norm, sc_w, w_out,
                g_ffn, wr, br):
    i = SSD_INNER
    conv_ch = i + 2 * SSD_GROUPS * SSD_STATE
    o_dt = i + conv_ch
    o_sc = o_dt + SSD_HEADS
    w_main = jnp.concatenate([w_in[:, :o_dt], w_in[:, o_sc:]], axis=1).astype(BF16)
    w_dt = jnp.pad(w_in[:, o_dt:o_sc], ((0, 0), (0, LANES - SSD_HEADS))).astype(BF16)
    proj = _matmul(hn, w_main, F32, 512, 512)
    dt_raw = _matmul(hn, w_dt, F32, 512, LANES)
    mix = _even_core(proj, dt_raw, batch, seq, conv_w, conv_b, dt_bias, a_log, d_skip, gate_norm, sc_w)
    return _outproj([mix], [w_out.astype(BF16)], h, g_ffn, wr, br)


def _odd_layer(h, hn, batch, seq, w_in, cmp_pos, cmp_w1, cmp_w2, lam, subln, w_out, lambda_init, g_ffn, wr, br):
    hd, kvw = HEAD_DIM, NSA_GROUPS * HEAD_DIM
    o = [0]
    for wdt in (NSA_HEADS * hd,) + (kvw,) * 6 + (3 * NSA_HEADS,) + (DIFF_HEADS * 2 * hd,) * 2 + (DIFF_HEADS * DIFF_V,):
        o.append(o[-1] + wdt)
    col = lambda k: w_in[:, o[k]:o[k + 1]]
    q, k_cmp, v_cmp, k_sel, v_sel, k_win, v_win, gates, dq, dk, dv = [col(k) for k in range(11)]
    gcols = []
    for g in range(NSA_GROUPS):
        per = 3 * NSA_REP
        gcols.append(jnp.pad(gates[:, g * per:(g + 1) * per], ((0, 0), (0, LANES - per))))
    w_rope = jnp.concatenate([q, k_sel, k_win, dq, dk], axis=1).astype(BF16)
    w_plain = jnp.concatenate([v_sel, v_win, dv], axis=1).astype(BF16)
    w_cmp = jnp.concatenate([k_cmp, v_cmp], axis=1).astype(BF16)
    w_gates = jnp.concatenate(gcols, axis=1).astype(BF16)
    pos = jnp.arange(seq)
    roped = _matmul(hn, w_rope, BF16, 512, 512, rope=_rope_tables(pos), seq=seq)
    plain = _matmul(hn, w_plain, BF16, 512, 512)
    kv_cmp = _matmul(hn, w_cmp, F32, 512, 512)
    gate_lg = _matmul(hn, w_gates, F32, 512, 2 * LANES)
    ncmp = seq // CMP_STRIDE
    cmp_end = jnp.arange(ncmp) * CMP_STRIDE + CMP_CHUNKS * CMP_STRIDE - 1
    cos_c, sin_c = _rope_tables(cmp_end)
    cmp_kv = _compress(kv_cmp, batch, seq, cmp_pos, cmp_w1, cmp_w2, cos_c, sin_c)
    n_idx = jnp.arange(ncmp)[:, None]
    j_idx = jnp.arange(LANES)[None, :]
    per_sel = SEL_BLOCK // CMP_STRIDE
    pool = ((n_idx >= per_sel * j_idx - 1) & (n_idx <= per_sel * j_idx + per_sel - 1)
            & (n_idx < ncmp - 1) & (j_idx < seq // SEL_BLOCK)).astype(F32)
    expand = (jnp.arange(LANES)[:, None] == (jnp.arange(seq)[None, :] // SEL_BLOCK)).astype(BF16)
    o_nsa = _nsa(roped, plain, cmp_kv, gate_lg, batch, seq, pool, expand)
    o_diff = _diff_attn(roped, plain, lam, subln, batch, seq, lambda_init)
    n_nsa = NSA_HEADS * hd
    w_o = w_out.astype(BF16)
    return _outproj([o_nsa, o_diff], [w_o[:n_nsa], w_o[n_nsa:]], h, g_ffn, wr, br)


def kernel(x, p, norm_mix, norm_ffn, norm_ple, norm_final, ev_w_in, ev_conv_w, ev_conv_b, ev_dt_bias, ev_a_log, ev_d_skip, ev_gate_norm, ev_sc_w, ev_w_out, od_w_in, od_cmp_pos, od_cmp_w1, od_cmp_w2, od_lambda, od_subln, od_w_out, moe_w_group, moe_b_group, moe_w_expert, moe_b_expert, moe_w_gate, moe_w_up, moe_w_down, ple_gate, ple_proj):
    batch, seq, d = x.shape
    n = batch * seq
    depth = p.shape[0]
    h = x.reshape(n, d)
    hn = _rmsnorm(h, norm_mix[0], BF16)
    for i in range(depth):
        j = i // 2
        wr, br = _router_weights(moe_w_group[i], moe_b_group[i], moe_w_expert[i], moe_b_expert[i])
        if i % 2 == 0:
            h, hn2, logits = _even_layer(h, hn, batch, seq, ev_w_in[j], ev_conv_w[j], ev_conv_b[j], ev_dt_bias[j],
                                         ev_a_log[j], ev_d_skip[j], ev_gate_norm[j], ev_sc_w[j], ev_w_out[j],
                                         norm_ffn[i], wr, br)
        else:
            lambda_init = 0.8 - 0.6 * math.exp(-0.3 * i)
            h, hn2, logits = _odd_layer(h, hn, batch, seq, od_w_in[j], od_cmp_pos[j], od_cmp_w1[j], od_cmp_w2[j],
                                        od_lambda[j], od_subln[j], od_w_out[j], lambda_init, norm_ffn[i], wr, br)
        y = _moe(hn2, logits, moe_w_gate[i].astype(BF16), moe_w_up[i].astype(BF16), moe_w_down[i].astype(BF16))
        last = i == depth - 1
        g_next = norm_final if last else norm_mix[i + 1]
        h, hn = _ple(h, y, p[i].reshape(n, PLE_DIM), norm_ple[i], ple_gate[i].astype(BF16), ple_proj[i].astype(BF16),
                     g_next, F32 if last else BF16)
    return hn.reshape(batch, seq, d)
```

```python
import functools
import math

import jax
import jax.numpy as jnp
from jax import lax
from jax.experimental import pallas as pl
from jax.experimental.pallas import tpu as pltpu

F32 = jnp.float32
BF16 = jnp.bfloat16
HIGHEST = lax.Precision.HIGHEST

LANES = 128
D_MODEL = 2048
NORM_EPS = 1e-6
ROPE_THETA = 500000.0
ROPE_HALF = 16
NEG_INF = -1e30

SSD_HEADS = 32
SSD_HEAD_DIM = 64
SSD_GROUPS = 8
SSD_STATE = 128
SSD_CHUNK = 128
SSD_INNER = 2048
CONV_HALO = 8

HEAD_DIM = 128
NSA_HEADS = 8
NSA_GROUPS = 2
NSA_REP = 4
CMP_STRIDE = 16
CMP_CHUNKS = 2
SEL_BLOCK = 64
SEL_TOPK = 16
SEL_FORCE = 1e4
WINDOW = 512
DIFF_HEADS = 4
DIFF_V = 256

MOE_GROUPS = 4
MOE_PER_GROUP = 8
MOE_EXPERTS = 32
MOE_FF = 1024
MOE_ROWS = 256
ROUTER_TILE = 512
PLE_DIM = 256

VMEM_LIMIT = 56 * 1024 * 1024


def _cparams(sem, vmem=VMEM_LIMIT):
    return pltpu.CompilerParams(dimension_semantics=sem, vmem_limit_bytes=vmem)


def _nt_dot(a, b):
    return lax.dot_general(a, b, (((1,), (1,)), ((), ())), preferred_element_type=F32)


def _silu(x):
    return x * jax.nn.sigmoid(x)


def _rmsnorm_kernel(x_ref, g_ref, o_ref):
    x = x_ref[...]
    ms = jnp.mean(x * x, axis=-1, keepdims=True)
    o_ref[...] = (x * lax.rsqrt(ms + NORM_EPS) * g_ref[...]).astype(o_ref.dtype)


def _rmsnorm(x, g, out_dtype, tm=512):
    n, d = x.shape
    return pl.pallas_call(
        _rmsnorm_kernel,
        grid=(n // tm,),
        in_specs=[pl.BlockSpec((tm, d), lambda i: (i, 0)), pl.BlockSpec((1, d), lambda i: (0, 0))],
        out_specs=pl.BlockSpec((tm, d), lambda i: (i, 0)),
        out_shape=jax.ShapeDtypeStruct((n, d), out_dtype),
        compiler_params=_cparams(("parallel",)),
        name="rmsnorm",
    )(x, g.reshape(1, d))


def _rope_apply(x, cosf, sinf, lane):
    swapped = jnp.where(lane < ROPE_HALF, pltpu.roll(x, LANES - ROPE_HALF, 1), pltpu.roll(x, ROPE_HALF, 1))
    return x * cosf + swapped * sinf


def _mm_kernel(a_ref, w_ref, o_ref):
    o_ref[...] = jnp.dot(a_ref[...], w_ref[...], preferred_element_type=F32).astype(o_ref.dtype)


def _mm_rope_kernel(a_ref, w_ref, cos_ref, sin_ref, o_ref):
    acc = jnp.dot(a_ref[...], w_ref[...], preferred_element_type=F32)
    tm, tn = acc.shape
    cosf = cos_ref[...]
    sinf = sin_ref[...]
    lane = lax.broadcasted_iota(jnp.int32, (tm, LANES), 1)
    for h in range(tn // LANES):
        sl = slice(h * LANES, (h + 1) * LANES)
        o_ref[:, sl] = _rope_apply(acc[:, sl], cosf, sinf, lane).astype(o_ref.dtype)


def _matmul(a, w, out_dtype, tm, tn, rope=None, seq=None):
    m, k = a.shape
    n = w.shape[1]
    in_specs = [pl.BlockSpec((tm, k), lambda i, j: (i, 0)), pl.BlockSpec((k, tn), lambda i, j: (0, j))]
    args = [a, w]
    body = _mm_kernel
    if rope is not None:
        per_seq = seq // tm
        in_specs += [pl.BlockSpec((tm, LANES), lambda i, j: (i % per_seq, 0))] * 2
        args += list(rope)
        body = _mm_rope_kernel
    return pl.pallas_call(
        body,
        grid=(m // tm, n // tn),
        in_specs=in_specs,
        out_specs=pl.BlockSpec((tm, tn), lambda i, j: (i, j)),
        out_shape=jax.ShapeDtypeStruct((m, n), out_dtype),
        compiler_params=_cparams(("parallel", "parallel")),
        name="proj_rope" if rope is not None else "proj",
    )(*args)


def _even_core_kernel(z_ref, xs_ref, bc_ref, scb_ref, scc_ref, sch_ref,
                      xs_h_ref, bc_h_ref, scc_h_ref, sch_h_ref, dt_ref,
                      cwx_ref, cwbc_ref, cbx_ref, cbbc_ref, dtb_ref, alog_ref, dskip_ref, gnorm_ref, scw_ref,
                      o_ref, state_ref, cbuf_ref, xsc_ref, bcc_ref, y_ref):
    q = SSD_CHUNK
    first = pl.program_id(1) == 0
    keep = jnp.where(first, 0.0, 1.0).astype(F32)

    @pl.when(first)
    def _():
        state_ref[...] = jnp.zeros_like(state_ref)

    strip = 512

    def causal_conv(load_main, load_halo, w_ref, width, finish):
        cbuf_ref[0:CONV_HALO, :] = load_halo() * keep
        cbuf_ref[CONV_HALO:CONV_HALO + q, :] = load_main()
        for c0 in range(0, SSD_INNER, strip):
            cs = slice(c0, c0 + strip)
            acc = None
            for k in range(width):
                r0 = CONV_HALO - (width - 1) + k
                term = w_ref[k:k + 1, cs] * cbuf_ref[r0:r0 + q, cs]
                acc = term if acc is None else acc + term
            finish(cs, acc)

    def fin_xs(cs, acc):
        xsc_ref[:, cs] = _silu(acc + cbx_ref[:, cs])

    def fin_bc(cs, acc):
        bcc_ref[:, cs] = _silu(acc + cbbc_ref[:, cs])

    causal_conv(lambda: xs_ref[...], lambda: xs_h_ref[...], cwx_ref, 4, fin_xs)
    causal_conv(lambda: bc_ref[...], lambda: bc_h_ref[...], cwbc_ref, 4, fin_bc)

    def fin_sc(cs, acc):
        o_ref[:, SSD_INNER + cs.start:SSD_INNER + cs.stop] = (scb_ref[:, cs] * acc).astype(o_ref.dtype)

    causal_conv(lambda: scc_ref[...] * sch_ref[...], lambda: scc_h_ref[...] * sch_h_ref[...], scw_ref, 3, fin_sc)

    dt = jax.nn.softplus(dt_ref[...] + dtb_ref[...])
    d_a = dt * (-jnp.exp(alog_ref[...]))
    row = lax.broadcasted_iota(jnp.int32, (q, q), 0)
    col = lax.broadcasted_iota(jnp.int32, (q, q), 1)
    causal = row >= col
    tril = jnp.where(causal, 1.0, 0.0).astype(F32)
    a_cum = jnp.dot(tril, d_a, preferred_element_type=F32, precision=HIGHEST)
    a_last = a_cum[q - 1:q, :]
    decay_end = jnp.exp(a_last - a_cum)
    chunk_decay = jnp.exp(a_last)
    exp_acum = jnp.exp(a_cum)
    a_cum_t = a_cum.T
    lane = lax.broadcasted_iota(jnp.int32, (q, LANES), 1)
    lo = lane < SSD_HEAD_DIM
    lane_row = lax.broadcasted_iota(jnp.int32, (1, LANES), 1)
    lo_row = lane_row < SSD_HEAD_DIM

    def pair_cols(mat, h0):
        return jnp.where(lo, mat[:, h0:h0 + 1], mat[:, h0 + 1:h0 + 2])

    for g in range(SSD_GROUPS):
        gs = slice(g * SSD_STATE, (g + 1) * SSD_STATE)
        b_g = bcc_ref[:, gs]
        c_g = bcc_ref[:, SSD_GROUPS * SSD_STATE + g * SSD_STATE:SSD_GROUPS * SSD_STATE + (g + 1) * SSD_STATE]
        b_gt = b_g.T.astype(BF16)
        c_gb = c_g.astype(BF16)
        cb = jnp.dot(c_gb, b_gt, preferred_element_type=F32)
        for hp in range(2):
            pr = g * 2 + hp
            h0 = 2 * pr
            ps = slice(pr * LANES, (pr + 1) * LANES)
            xp = xsc_ref[:, ps]
            xdt = xp * pair_cols(dt, h0)
            mats = []
            for h in (h0, h0 + 1):
                seg = a_cum[:, h:h + 1] - a_cum_t[h:h + 1, :]
                dec = jnp.where(causal, jnp.exp(jnp.minimum(seg, 0.0)), 0.0)
                mats.append((cb * dec).astype(BF16))
            lhs = jnp.concatenate(mats, axis=1)
            rhs = jnp.concatenate([jnp.where(lo, xdt, 0.0), jnp.where(lo, 0.0, xdt)], axis=0).astype(BF16)
            y = jnp.dot(lhs, rhs, preferred_element_type=F32)
            st = state_ref[pr]
            y = y + jnp.dot(c_gb, st.astype(BF16), preferred_element_type=F32) * pair_cols(exp_acum, h0)
            xw = (xdt * pair_cols(decay_end, h0)).astype(BF16)
            cd = jnp.where(lo_row, chunk_decay[:, h0:h0 + 1], chunk_decay[:, h0 + 1:h0 + 2])
            state_ref[pr] = st * cd + jnp.dot(b_gt, xw, preferred_element_type=F32)
            y_ref[:, ps] = y + dskip_ref[:, ps] * xp

    gw = SSD_INNER // SSD_GROUPS
    for g in range(SSD_GROUPS):
        cs = slice(g * gw, (g + 1) * gw)
        yg = y_ref[:, cs] * _silu(z_ref[:, cs])
        ms = jnp.mean(yg * yg, axis=-1, keepdims=True)
        o_ref[:, cs] = (yg * lax.rsqrt(ms + NORM_EPS) * gnorm_ref[:, cs]).astype(o_ref.dtype)


def _even_core(proj, dt_raw, batch, seq, conv_w, conv_b, dt_bias, a_log, d_skip, gate_norm, sc_w):
    q = SSD_CHUNK
    nc = seq // q
    w = SSD_INNER
    hb = q // CONV_HALO

    def main(cb):
        return pl.BlockSpec((q, w), lambda b, c: (b * nc + c, cb))

    def halo(cb):
        return pl.BlockSpec((CONV_HALO, w), lambda b, c: (jnp.maximum((b * nc + c) * hb - 1, 0), cb))

    def full(shape):
        return pl.BlockSpec(shape, lambda b, c: (0, 0))

    pad = LANES - SSD_HEADS
    dtb = jnp.pad(dt_bias, (0, pad)).reshape(1, LANES)
    alog = jnp.pad(a_log, (0, pad)).reshape(1, LANES)
    dskip = jnp.repeat(d_skip, SSD_HEAD_DIM).reshape(1, w)
    in_specs = [main(0), main(1), main(2), main(3), main(4), main(5),
                halo(1), halo(2), halo(4), halo(5),
                pl.BlockSpec((q, LANES), lambda b, c: (b * nc + c, 0)),
                full((4, w)), full((4, w)), full((1, w)), full((1, w)),
                full((1, LANES)), full((1, LANES)), full((1, w)), full((1, w)), full((3, w))]
    return pl.pallas_call(
        _even_core_kernel,
        grid=(batch, nc),
        in_specs=in_specs,
        out_specs=pl.BlockSpec((q, 2 * w), lambda b, c: (b * nc + c, 0)),
        out_shape=jax.ShapeDtypeStruct((batch * seq, 2 * w), BF16),
        scratch_shapes=[pltpu.VMEM((SSD_HEADS // 2, SSD_STATE, LANES), F32),
                        pltpu.VMEM((CONV_HALO + q, w), F32),
                        pltpu.VMEM((q, w), F32), pltpu.VMEM((q, w), F32), pltpu.VMEM((q, w), F32)],
        compiler_params=_cparams(("parallel", "arbitrary")),
        name="ssd_conv_core",
    )(proj, proj, proj, proj, proj, proj, proj, proj, proj, proj, dt_raw,
      conv_w[:, :w], conv_w[:, w:], conv_b[:w].reshape(1, w), conv_b[w:].reshape(1, w),
      dtb, alog, dskip, gate_norm.reshape(1, w), sc_w)


def _outproj_kernel(*refs, n_in):
    mix = refs[:n_in]
    ws = refs[n_in:2 * n_in]
    h_ref, g_ref, wr_ref, br_ref, hnew_ref, hn_ref, lg_ref = refs[2 * n_in:]
    acc = h_ref[...]
    for m_ref, w_ref in zip(mix, ws):
        acc = acc + jnp.dot(m_ref[...], w_ref[...], preferred_element_type=F32)
    hnew_ref[...] = acc
    ms = jnp.mean(acc * acc, axis=-1, keepdims=True)
    hn = acc * lax.rsqrt(ms + NORM_EPS) * g_ref[...]
    hn_ref[...] = hn.astype(hn_ref.dtype)
    lg_ref[...] = jnp.dot(hn, wr_ref[...], preferred_element_type=F32, precision=HIGHEST) + br_ref[...]


def _outproj(mixes, ws, h, g, wr, br, tm=256):
    n, d = h.shape
    n_in = len(mixes)
    in_specs = ([pl.BlockSpec((tm, m.shape[1]), lambda i: (i, 0)) for m in mixes]
                + [pl.BlockSpec(w.shape, lambda i: (0, 0)) for w in ws]
                + [pl.BlockSpec((tm, d), lambda i: (i, 0)), pl.BlockSpec((1, d), lambda i: (0, 0)),
                   pl.BlockSpec((d, LANES), lambda i: (0, 0)), pl.BlockSpec((1, LANES), lambda i: (0, 0))])
    return pl.pallas_call(
        functools.partial(_outproj_kernel, n_in=n_in),
        grid=(n // tm,),
        in_specs=in_specs,
        out_specs=[pl.BlockSpec((tm, d), lambda i: (i, 0)), pl.BlockSpec((tm, d), lambda i: (i, 0)),
                   pl.BlockSpec((tm, LANES), lambda i: (i, 0))],
        out_shape=[jax.ShapeDtypeStruct((n, d), F32), jax.ShapeDtypeStruct((n, d), BF16),
                   jax.ShapeDtypeStruct((n, LANES), F32)],
        compiler_params=_cparams(("parallel",)),
        name="outproj_norm_router",
    )(*mixes, *ws, h, g.reshape(1, d), wr, br)


ROUTE_E1, ROUTE_E2, ROUTE_W1, ROUTE_W2, ROUTE_R1, ROUTE_R2 = range(6)


def _router_kernel(lg_ref, out_ref, cnt_ref, run_ref):
    @pl.when(pl.program_id(0) == 0)
    def _():
        run_ref[...] = jnp.zeros_like(run_ref)

    x = lg_ref[...]
    t = x.shape[0]
    lane = lax.broadcasted_iota(jnp.int32, (t, LANES), 1).astype(F32)
    far = float(LANES)
    gmask = lane < MOE_GROUPS
    gl = jnp.where(gmask, x, NEG_INF)
    gmax = jnp.max(gl, axis=-1, keepdims=True)
    g_top = jnp.min(jnp.where(gl == gmax, lane, far), axis=-1, keepdims=True)
    gsum = jnp.sum(jnp.where(gmask, jnp.exp(gl - gmax), 0.0), axis=-1, keepdims=True)
    g_prob = 1.0 / gsum
    lo = MOE_GROUPS + MOE_PER_GROUP * g_top
    emask = (lane >= lo) & (lane < lo + MOE_PER_GROUP)
    el = jnp.where(emask, x, NEG_INF)
    emax = jnp.max(el, axis=-1, keepdims=True)
    ee = jnp.where(emask, jnp.exp(el - emax), 0.0)
    ep = jnp.where(emask, ee / jnp.sum(ee, axis=-1, keepdims=True), -1.0)
    p1 = jnp.max(ep, axis=-1, keepdims=True)
    i1 = jnp.min(jnp.where(ep == p1, lane, far), axis=-1, keepdims=True)
    ep2 = jnp.where(lane == i1, -1.0, ep)
    p2 = jnp.max(ep2, axis=-1, keepdims=True)
    i2 = jnp.min(jnp.where(ep2 == p2, lane, far), axis=-1, keepdims=True)
    den = p1 + p2
    w1 = g_prob * p1 / den
    w2 = g_prob * p2 / den
    oh1 = jnp.where(lane == i1, 1.0, 0.0)
    oh2 = jnp.where(lane == i2, 1.0, 0.0)
    oh = oh1 + oh2
    row = lax.broadcasted_iota(jnp.int32, (t, t), 0)
    col = lax.broadcasted_iota(jnp.int32, (t, t), 1)
    strict = jnp.where(row > col, 1.0, 0.0).astype(BF16)
    before = jnp.dot(strict, oh.astype(BF16), preferred_element_type=F32) + run_ref[...]
    r1 = jnp.sum(oh1 * before, axis=-1, keepdims=True)
    r2 = jnp.sum(oh2 * before, axis=-1, keepdims=True)
    run_ref[...] = run_ref[...] + jnp.sum(oh, axis=0, keepdims=True)
    cnt_ref[...] = run_ref[...]
    packed = jnp.zeros((t, LANES), F32)
    for k, v in ((ROUTE_E1, i1 - MOE_GROUPS), (ROUTE_E2, i2 - MOE_GROUPS), (ROUTE_W1, w1), (ROUTE_W2, w2),
                 (ROUTE_R1, r1), (ROUTE_R2, r2)):
        packed = jnp.where(lane == k, v, packed)
    out_ref[...] = packed


def _router(logits, tt=ROUTER_TILE):
    n = logits.shape[0]
    return pl.pallas_call(
        _router_kernel,
        grid=(n // tt,),
        in_specs=[pl.BlockSpec((tt, LANES), lambda i: (i, 0))],
        out_specs=[pl.BlockSpec((tt, LANES), lambda i: (i, 0)), pl.BlockSpec((1, LANES), lambda i: (0, 0))],
        out_shape=[jax.ShapeDtypeStruct((n, LANES), F32), jax.ShapeDtypeStruct((1, LANES), F32)],
        scratch_shapes=[pltpu.VMEM((1, LANES), F32)],
        compiler_params=_cparams(("arbitrary",)),
        name="router_rank",
    )(logits)


def _moe_kernel(be_ref, bv_ref, x_ref, wg_ref, wu_ref, wd_ref, o_ref):
    i = pl.program_id(0)

    @pl.when(bv_ref[i] > 0)
    def _():
        x = x_ref[...]
        step = 256
        for n0 in range(0, MOE_FF, step):
            gate = jnp.dot(x, wg_ref[:, n0:n0 + step].astype(BF16), preferred_element_type=F32)
            up = jnp.dot(x, wu_ref[:, n0:n0 + step].astype(BF16), preferred_element_type=F32)
            hid = (_silu(gate) * up).astype(BF16)
            part = jnp.dot(hid, wd_ref[n0:n0 + step, :].astype(BF16), preferred_element_type=F32)
            if n0 == 0:
                o_ref[...] = part
            else:
                o_ref[...] += part

    @pl.when(bv_ref[i] == 0)
    def _():
        o_ref[...] = jnp.zeros_like(o_ref)


def _moe_ffn(xs, blk_expert, blk_valid, layer, wg, wu, wd):
    cap, d = xs.shape
    nb = cap // MOE_ROWS
    ff = wg.shape[3]
    grid_spec = pltpu.PrefetchScalarGridSpec(
        num_scalar_prefetch=2,
        grid=(nb,),
        in_specs=[pl.BlockSpec((MOE_ROWS, d), lambda i, be, bv: (i, 0)),
                  pl.BlockSpec((None, None, d, ff), lambda i, be, bv: (layer, be[i], 0, 0)),
                  pl.BlockSpec((None, None, d, ff), lambda i, be, bv: (layer, be[i], 0, 0)),
                  pl.BlockSpec((None, None, ff, d), lambda i, be, bv: (layer, be[i], 0, 0),
                               pipeline_mode=pl.Buffered(1))],
        out_specs=pl.BlockSpec((MOE_ROWS, d), lambda i, be, bv: (i, 0)),
    )
    return pl.pallas_call(
        _moe_kernel,
        grid_spec=grid_spec,
        out_shape=jax.ShapeDtypeStruct((cap, d), F32),
        compiler_params=_cparams(("arbitrary",)),
        name="moe_ffn",
    )(blk_expert, blk_valid, xs, wg, wu, wd)


def _moe(hn, logits, layer, wg, wu, wd):
    n, d = hn.shape
    routed, counts = _router(logits)
    e1 = routed[:, ROUTE_E1].astype(jnp.int32)
    e2 = routed[:, ROUTE_E2].astype(jnp.int32)
    counts = counts[0, MOE_GROUPS:MOE_GROUPS + MOE_EXPERTS].astype(jnp.int32)
    padded = (counts + MOE_ROWS - 1) // MOE_ROWS * MOE_ROWS
    pad_end = jnp.cumsum(padded)
    pad_start = pad_end - padded
    slot1 = pad_start[e1] + routed[:, ROUTE_R1].astype(jnp.int32)
    slot2 = pad_start[e2] + routed[:, ROUTE_R2].astype(jnp.int32)
    nb = -(-(2 * n + MOE_EXPERTS * (MOE_ROWS - 1)) // MOE_ROWS)
    cap = nb * MOE_ROWS
    tok = jnp.arange(n, dtype=jnp.int32)
    slot_tok = jnp.zeros((cap,), jnp.int32).at[slot1].set(tok).at[slot2].set(tok)
    blk_start = jnp.arange(nb, dtype=jnp.int32) * MOE_ROWS
    blk_valid = (blk_start < pad_end[-1]).astype(jnp.int32)
    last_valid = jnp.maximum(pad_end[-1] // MOE_ROWS - 1, 0)
    blk_expert = jnp.sum((blk_start[:, None] >= pad_end[None, :]).astype(jnp.int32), axis=1)
    blk_expert = jnp.minimum(blk_expert, MOE_EXPERTS - 1)
    blk_expert = jnp.where(blk_valid > 0, blk_expert, blk_expert[last_valid])
    xs = jnp.take(hn, slot_tok, axis=0)
    yb = _moe_ffn(xs, blk_expert, blk_valid, layer, wg, wu, wd)
    return jnp.take(yb, slot1, axis=0), jnp.take(yb, slot2, axis=0), routed


def _ple_kernel(h_ref, y1_ref, y2_ref, rt_ref, p_ref, gp_ref, wg_ref, wp_ref, gn_ref, hout_ref, nout_ref):
    rt = rt_ref[...]
    moe = y1_ref[...] * rt[:, ROUTE_W1:ROUTE_W1 + 1] + y2_ref[...] * rt[:, ROUTE_W2:ROUTE_W2 + 1]
    h = h_ref[...] + moe
    ms = jnp.mean(h * h, axis=-1, keepdims=True)
    hn = (h * lax.rsqrt(ms + NORM_EPS) * gp_ref[...]).astype(BF16)
    gate = jax.nn.sigmoid(jnp.dot(hn, wg_ref[...], preferred_element_type=F32))
    emb = jnp.dot(p_ref[...].astype(BF16), wp_ref[...], preferred_element_type=F32)
    h = h + gate * emb
    hout_ref[...] = h
    ms = jnp.mean(h * h, axis=-1, keepdims=True)
    nout_ref[...] = (h * lax.rsqrt(ms + NORM_EPS) * gn_ref[...]).astype(nout_ref.dtype)


def _ple(h, y1, y2, routed, p, g_ple, w_gate, w_proj, g_next, next_dtype, tm=256):
    n, d = h.shape
    row = lambda i: (i, 0)
    fixed = lambda i: (0, 0)
    return pl.pallas_call(
        _ple_kernel,
        grid=(n // tm,),
        in_specs=[pl.BlockSpec((tm, d), row), pl.BlockSpec((tm, d), row), pl.BlockSpec((tm, d), row),
                  pl.BlockSpec((tm, LANES), row), pl.BlockSpec((tm, PLE_DIM), row),
                  pl.BlockSpec((1, d), fixed), pl.BlockSpec((d, d), fixed), pl.BlockSpec((PLE_DIM, d), fixed),
                  pl.BlockSpec((1, d), fixed)],
        out_specs=[pl.BlockSpec((tm, d), row), pl.BlockSpec((tm, d), row)],
        out_shape=[jax.ShapeDtypeStruct((n, d), F32), jax.ShapeDtypeStruct((n, d), next_dtype)],
        compiler_params=_cparams(("parallel",)),
        name="ple_norm",
    )(h, y1, y2, routed, p, g_ple.reshape(1, d), w_gate, w_proj, g_next.reshape(1, d))


def _compress_kernel(x_ref, pos_ref, w1_ref, w2_ref, cos_ref, sin_ref, o_ref):
    nchunk = x_ref.shape[0]
    width = 2 * NSA_GROUPS * HEAD_DIM
    lane = lax.broadcasted_iota(jnp.int32, (nchunk, LANES), 1)
    for kv in range(2):
        for g in range(NSA_GROUPS):
            acc = [jnp.zeros((nchunk, HEAD_DIM), F32) for _ in range(CMP_CHUNKS)]
            for half in range(CMP_CHUNKS):
                for j in range(CMP_STRIDE):
                    c0 = j * width + kv * NSA_GROUPS * HEAD_DIM + g * HEAD_DIM
                    jj = half * CMP_STRIDE + j
                    blk = (x_ref[:, c0:c0 + HEAD_DIM] + pos_ref[kv, jj:jj + 1, :]).astype(BF16)
                    acc[half] = acc[half] + jnp.dot(blk, w1_ref[kv, jj * HEAD_DIM:(jj + 1) * HEAD_DIM, :],
                                                    preferred_element_type=F32)
            pre = acc[0] + pltpu.roll(acc[1], nchunk - 1, 0)
            out = jnp.dot(_silu(pre).astype(BF16), w2_ref[kv], preferred_element_type=F32)
            if kv == 0:
                out = _rope_apply(out, cos_ref[...], sin_ref[...], lane)
            o_ref[kv * NSA_GROUPS + g] = out.astype(o_ref.dtype)


def _compress(kv_cmp, batch, seq, cmp_pos, cmp_w1, cmp_w2, cos_c, sin_c):
    nchunk = seq // CMP_STRIDE
    width = kv_cmp.shape[1]
    x = kv_cmp.reshape(batch * nchunk, CMP_STRIDE * width)
    return pl.pallas_call(
        _compress_kernel,
        grid=(batch,),
        in_specs=[pl.BlockSpec((nchunk, CMP_STRIDE * width), lambda b: (b, 0)),
                  pl.BlockSpec(cmp_pos.shape, lambda b: (0, 0, 0)),
                  pl.BlockSpec(cmp_w1.shape, lambda b: (0, 0, 0)),
                  pl.BlockSpec(cmp_w2.shape, lambda b: (0, 0, 0)),
                  pl.BlockSpec((nchunk, LANES), lambda b: (0, 0)),
                  pl.BlockSpec((nchunk, LANES), lambda b: (0, 0))],
        out_specs=pl.BlockSpec((None, 2 * NSA_GROUPS, nchunk, HEAD_DIM), lambda b: (b, 0, 0, 0)),
        out_shape=jax.ShapeDtypeStruct((batch, 2 * NSA_GROUPS, nchunk, HEAD_DIM), BF16),
        compiler_params=_cparams(("parallel",)),
        name="nsa_compress",
    )(x, cmp_pos, cmp_w1.astype(BF16), cmp_w2.astype(BF16), cos_c, sin_c)


NSA_TQ = 128
NSA_TK = 512


def _nsa_kernel(q_ref, kc_ref, vc_ref, ksel_ref, vsel_ref, kwin_ref, vwin_ref, gate_ref, pool_ref, expand_ref,
                o_ref):
    tq = NSA_TQ
    rep = NSA_REP
    scale = HEAD_DIM ** -0.5
    q0 = pl.program_id(2) * tq
    q = q_ref[...]
    qs = jnp.concatenate([q[:, r * HEAD_DIM:(r + 1) * HEAD_DIM] for r in range(rep)], axis=0)
    tpos = q0 + lax.broadcasted_iota(jnp.int32, (tq, 1), 0)

    def rep_rows(a):
        return jnp.concatenate([a] * rep, axis=0)

    ncmp = kc_ref.shape[0]
    cend = lax.broadcasted_iota(jnp.int32, (1, ncmp), 1) * CMP_STRIDE + (CMP_CHUNKS * CMP_STRIDE - 1)
    cbias = rep_rows(jnp.where(cend <= tpos, 0.0, NEG_INF).astype(F32))
    s = _nt_dot(qs, kc_ref[...]) * scale + cbias
    m = jnp.max(s, axis=-1, keepdims=True)
    p = jnp.where(cbias == 0.0, jnp.exp(s - m), 0.0)
    l = jnp.sum(p, axis=-1, keepdims=True)
    p = p * jnp.where(l > 0.0, 1.0 / l, 0.0)
    o_cmp = jnp.dot(p.astype(BF16), vc_ref[...], preferred_element_type=F32)
    imp = p[0:tq]
    for r in range(1, rep):
        imp = imp + p[r * tq:(r + 1) * tq]
    p_sel = jnp.dot(imp, pool_ref[...], preferred_element_type=F32, precision=HIGHEST)

    nsel = ksel_ref.shape[0] // SEL_BLOCK
    blk = lax.broadcasted_iota(jnp.int32, (tq, LANES), 1)
    cur = tpos // SEL_BLOCK
    forced = (blk == 0) | (blk == cur) | (blk == cur - 1)
    valid = blk * SEL_BLOCK <= tpos
    score = jnp.where(forced, SEL_FORCE, jnp.where(valid, p_sel, NEG_INF))
    score = jnp.where(blk < nsel, score, 3.0 * NEG_INF)
    st = score.T
    sub = 8
    chunks = [st[c * sub:(c + 1) * sub, :] for c in range(nsel // sub)]
    cnt = [jnp.zeros((sub, tq), F32) for _ in chunks]
    jrow = lax.broadcasted_iota(jnp.int32, (sub, tq), 0)
    for i in range(nsel):
        ci, ii = divmod(i, sub)
        si = chunks[ci][ii:ii + 1, :]
        for c in range(len(chunks)):
            if c > ci:
                beats = si >= chunks[c]
            elif c < ci:
                beats = si > chunks[c]
            else:
                beats = jnp.where(jrow > ii, jnp.where(si >= chunks[c], 1.0, 0.0), jnp.where(si > chunks[c], 1.0, 0.0)) > 0.5
            cnt[c] = cnt[c] + jnp.where(beats, 1.0, 0.0)
    sel_t = jnp.concatenate([jnp.where(c < min(SEL_TOPK, nsel), 1.0, 0.0) for c in cnt]
                            + [jnp.zeros((LANES - nsel, tq), F32)], axis=0)
    sel = sel_t.T.astype(BF16)

    tk = NSA_TK
    nkt = q0 // tk + 1

    def sel_body(kt, carry):
        m_i, l_i, acc = carry
        k0 = pl.multiple_of(kt * tk, tk)
        kt_ = ksel_ref[pl.ds(k0, tk), :]
        vt_ = vsel_ref[pl.ds(k0, tk), :]
        chosen = jnp.dot(sel, expand_ref[:, pl.ds(k0, tk)], preferred_element_type=F32)
        kpos = k0 + lax.broadcasted_iota(jnp.int32, (1, tk), 1)
        bias = rep_rows(jnp.where((chosen > 0.5) & (kpos <= tpos), 0.0, NEG_INF).astype(F32))
        sc = _nt_dot(qs, kt_) * scale + bias
        m_new = jnp.maximum(m_i, jnp.max(sc, axis=-1, keepdims=True))
        alpha = jnp.exp(m_i - m_new)
        pe = jnp.exp(sc - m_new)
        l_new = alpha * l_i + jnp.sum(pe, axis=-1, keepdims=True)
        acc_new = alpha * acc + jnp.dot(pe.astype(BF16), vt_, preferred_element_type=F32)
        return m_new, l_new, acc_new

    init = (jnp.full((rep * tq, 1), NEG_INF, F32), jnp.zeros((rep * tq, 1), F32), jnp.zeros((rep * tq, HEAD_DIM), F32))
    _, l_s, acc_s = lax.fori_loop(0, nkt, sel_body, init)
    o_sel = acc_s / l_s

    wk = WINDOW + tq
    w0 = pl.multiple_of(jnp.maximum(q0 - WINDOW, 0), tq)
    kw = kwin_ref[pl.ds(w0, wk), :]
    vw = vwin_ref[pl.ds(w0, wk), :]
    kpos = w0 + lax.broadcasted_iota(jnp.int32, (1, wk), 1)
    wbias = rep_rows(jnp.where((kpos <= tpos) & (kpos > tpos - WINDOW), 0.0, NEG_INF).astype(F32))
    sw = _nt_dot(qs, kw) * scale + wbias
    pw = jnp.exp(sw - jnp.max(sw, axis=-1, keepdims=True))
    o_win = jnp.dot(pw.astype(BF16), vw, preferred_element_type=F32) / jnp.sum(pw, axis=-1, keepdims=True)

    gate = jax.nn.sigmoid(gate_ref[...])
    for r in range(rep):
        rs = slice(r * tq, (r + 1) * tq)
        o = (gate[:, 3 * r:3 * r + 1] * o_cmp[rs] + gate[:, 3 * r + 1:3 * r + 2] * o_sel[rs]
             + gate[:, 3 * r + 2:3 * r + 3] * o_win[rs])
        o_ref[:, r * HEAD_DIM:(r + 1) * HEAD_DIM] = o.astype(o_ref.dtype)


def _nsa(roped, plain, cmp_kv, gates, batch, seq, pool, expand):
    tq = NSA_TQ
    nq = seq // tq
    gw = NSA_REP * HEAD_DIM
    ksel_cb = NSA_HEADS
    kwin_cb = NSA_HEADS + NSA_GROUPS
    ncmp = seq // CMP_STRIDE
    seq_blk = lambda cb: pl.BlockSpec((seq, HEAD_DIM), functools.partial(lambda b, g, t, cb: (b, cb + g), cb=cb))
    return pl.pallas_call(
        _nsa_kernel,
        grid=(batch, NSA_GROUPS, nq),
        in_specs=[pl.BlockSpec((tq, gw), lambda b, g, t: (b * nq + t, g)),
                  pl.BlockSpec((None, None, ncmp, HEAD_DIM), lambda b, g, t: (b, g, 0, 0)),
                  pl.BlockSpec((None, None, ncmp, HEAD_DIM), lambda b, g, t: (b, NSA_GROUPS + g, 0, 0)),
                  seq_blk(ksel_cb), seq_blk(0), seq_blk(kwin_cb), seq_blk(NSA_GROUPS),
                  pl.BlockSpec((tq, LANES), lambda b, g, t: (b * nq + t, g)),
                  pl.BlockSpec(pool.shape, lambda b, g, t: (0, 0)),
                  pl.BlockSpec(expand.shape, lambda b, g, t: (0, 0))],
        out_specs=pl.BlockSpec((tq, gw), lambda b, g, t: (b * nq + t, g)),
        out_shape=jax.ShapeDtypeStruct((batch * seq, NSA_HEADS * HEAD_DIM), BF16),
        compiler_params=_cparams(("parallel", "parallel", "arbitrary")),
        name="nsa_attention",
    )(roped, cmp_kv, cmp_kv, roped, plain, roped, plain, gates, pool, expand)


DIFF_TQ = 256
DIFF_TK = 512


def _diff_kernel(q_ref, k_ref, v_ref, lam_ref, subln_ref, o_ref, *, lambda_init):
    tq, tk = DIFF_TQ, DIFF_TK
    scale = HEAD_DIM ** -0.5
    q0 = pl.program_id(2) * tq
    q = q_ref[...]
    q1 = q[:, :HEAD_DIM]
    q2 = q[:, HEAD_DIM:]
    tpos = q0 + lax.broadcasted_iota(jnp.int32, (tq, 1), 0)
    nkt = (q0 + tq - 1) // tk + 1

    def body(kt, carry):
        m_i, l_i, acc = carry
        k0 = pl.multiple_of(kt * tk, tk)
        k = k_ref[pl.ds(k0, tk), :]
        v = v_ref[pl.ds(k0, tk), :]
        kpos = k0 + lax.broadcasted_iota(jnp.int32, (1, tk), 1)
        bias = jnp.where(kpos <= tpos, 0.0, NEG_INF).astype(F32)
        s1 = _nt_dot(q1, k[:, :HEAD_DIM]) * scale + bias
        s2 = _nt_dot(q2, k[:, HEAD_DIM:]) * scale + bias
        sc = jnp.concatenate([s1, s2], axis=0)
        m_new = jnp.maximum(m_i, jnp.max(sc, axis=-1, keepdims=True))
        alpha = jnp.exp(m_i - m_new)
        pe = jnp.exp(sc - m_new)
        l_new = alpha * l_i + jnp.sum(pe, axis=-1, keepdims=True)
        acc_new = alpha * acc + jnp.dot(pe.astype(BF16), v, preferred_element_type=F32)
        return m_new, l_new, acc_new

    init = (jnp.full((2 * tq, 1), NEG_INF, F32), jnp.zeros((2 * tq, 1), F32), jnp.zeros((2 * tq, DIFF_V), F32))
    _, l_f, acc = lax.fori_loop(0, nkt, body, init)
    o = acc / l_f
    lam = lam_ref[...]
    lam_full = (jnp.exp(jnp.sum(lam[0:1] * lam[1:2], axis=-1, keepdims=True))
                - jnp.exp(jnp.sum(lam[2:3] * lam[3:4], axis=-1, keepdims=True)) + lambda_init)
    od = o[:tq] - lam_full * o[tq:]
    ms = jnp.mean(od * od, axis=-1, keepdims=True)
    o_ref[...] = (od * lax.rsqrt(ms + NORM_EPS) * subln_ref[...] * (1.0 - lambda_init)).astype(o_ref.dtype)


def _diff_attn(roped, plain, lam, subln, batch, seq, lambda_init):
    tq = DIFF_TQ
    nq = seq // tq
    pair = 2 * HEAD_DIM
    dq_cb = (NSA_HEADS + 2 * NSA_GROUPS) * HEAD_DIM // pair
    dk_cb = dq_cb + DIFF_HEADS
    dv_cb = 2 * NSA_GROUPS * HEAD_DIM // DIFF_V
    return pl.pallas_call(
        functools.partial(_diff_kernel, lambda_init=lambda_init),
        grid=(batch, DIFF_HEADS, nq),
        in_specs=[pl.BlockSpec((tq, pair), lambda b, h, t: (b * nq + t, dq_cb + h)),
                  pl.BlockSpec((seq, pair), lambda b, h, t: (b, dk_cb + h)),
                  pl.BlockSpec((seq, DIFF_V), lambda b, h, t: (b, dv_cb + h)),
                  pl.BlockSpec(lam.shape, lambda b, h, t: (0, 0)),
                  pl.BlockSpec((1, DIFF_V), lambda b, h, t: (0, 0))],
        out_specs=pl.BlockSpec((tq, DIFF_V), lambda b, h, t: (b * nq + t, h)),
        out_shape=jax.ShapeDtypeStruct((batch * seq, DIFF_HEADS * DIFF_V), BF16),
        compiler_params=_cparams(("parallel", "parallel", "arbitrary")),
        name="diff_attention",
    )(roped, roped, plain, lam, subln.reshape(1, DIFF_V))


def _rope_tables(pos):
    inv_freq = ROPE_THETA ** (-jnp.arange(ROPE_HALF, dtype=F32) / ROPE_HALF)
    ang = pos.astype(F32)[:, None] * inv_freq[None, :]
    cos, sin = jnp.cos(ang), jnp.sin(ang)
    rest = LANES - 2 * ROPE_HALF
    n = pos.shape[0]
    cosf = jnp.concatenate([cos, cos, jnp.ones((n, rest), F32)], axis=1)
    sinf = jnp.concatenate([-sin, sin, jnp.zeros((n, rest), F32)], axis=1)
    return cosf, sinf


def _router_weights(w_group, b_group, w_expert, b_expert):
    pad = LANES - MOE_GROUPS - MOE_EXPERTS
    wr = jnp.concatenate([w_group, w_expert, jnp.zeros((w_group.shape[0], pad), F32)], axis=1)
    br = jnp.concatenate([b_group, b_expert, jnp.zeros((pad,), F32)]).reshape(1, LANES)
    return wr, br


def _even_layer(h, hn, batch, seq, w_in, conv_w, conv_b, dt_bias, a_log, d_skip, gate_norm, sc_w, w_out,
                g_ffn, wr, br):
    i = SSD_INNER
    conv_ch = i + 2 * SSD_GROUPS * SSD_STATE
    o_dt = i + conv_ch
    o_sc = o_dt + SSD_HEADS
    w_main = jnp.concatenate([w_in[:, :o_dt], w_in[:, o_sc:]], axis=1).astype(BF16)
    w_dt = jnp.pad(w_in[:, o_dt:o_sc], ((0, 0), (0, LANES - SSD_HEADS))).astype(BF16)
    proj = _matmul(hn, w_main, F32, 1024, 1024)
    dt_raw = _matmul(hn, w_dt, F32, 512, LANES)
    mix = _even_core(proj, dt_raw, batch, seq, conv_w, conv_b, dt_bias, a_log, d_skip, gate_norm, sc_w)
    return _outproj([mix], [w_out.astype(BF16)], h, g_ffn, wr, br)


def _odd_layer(h, hn, batch, seq, w_in, cmp_pos, cmp_w1, cmp_w2, lam, subln, w_out, lambda_init, g_ffn, wr, br):
    hd, kvw = HEAD_DIM, NSA_GROUPS * HEAD_DIM
    o = [0]
    for wdt in (NSA_HEADS * hd,) + (kvw,) * 6 + (3 * NSA_HEADS,) + (DIFF_HEADS * 2 * hd,) * 2 + (DIFF_HEADS * DIFF_V,):
        o.append(o[-1] + wdt)
    col = lambda k: w_in[:, o[k]:o[k + 1]]
    q, k_cmp, v_cmp, k_sel, v_sel, k_win, v_win, gates, dq, dk, dv = [col(k) for k in range(11)]
    gcols = []
    for g in range(NSA_GROUPS):
        per = 3 * NSA_REP
        gcols.append(jnp.pad(gates[:, g * per:(g + 1) * per], ((0, 0), (0, LANES - per))))
    w_rope = jnp.concatenate([q, k_sel, k_win, dq, dk], axis=1).astype(BF16)
    w_plain = jnp.concatenate([v_sel, v_win, dv], axis=1).astype(BF16)
    w_cmp = jnp.concatenate([k_cmp, v_cmp], axis=1).astype(BF16)
    w_gates = jnp.concatenate(gcols, axis=1).astype(BF16)
    pos = jnp.arange(seq)
    roped = _matmul(hn, w_rope, BF16, 512, 512, rope=_rope_tables(pos), seq=seq)
    plain = _matmul(hn, w_plain, BF16, 512, 512)
    kv_cmp = _matmul(hn, w_cmp, F32, 512, 512)
    gate_lg = _matmul(hn, w_gates, F32, 512, 2 * LANES)
    ncmp = seq // CMP_STRIDE
    cmp_end = jnp.arange(ncmp) * CMP_STRIDE + CMP_CHUNKS * CMP_STRIDE - 1
    cos_c, sin_c = _rope_tables(cmp_end)
    cmp_kv = _compress(kv_cmp, batch, seq, cmp_pos, cmp_w1, cmp_w2, cos_c, sin_c)
    n_idx = jnp.arange(ncmp)[:, None]
    j_idx = jnp.arange(LANES)[None, :]
    per_sel = SEL_BLOCK // CMP_STRIDE
    pool = ((n_idx >= per_sel * j_idx - 1) & (n_idx <= per_sel * j_idx + per_sel - 1)
            & (n_idx < ncmp - 1) & (j_idx < seq // SEL_BLOCK)).astype(F32)
    expand = (jnp.arange(LANES)[:, None] == (jnp.arange(seq)[None, :] // SEL_BLOCK)).astype(BF16)
    o_nsa = _nsa(roped, plain, cmp_kv, gate_lg, batch, seq, pool, expand)
    o_diff = _diff_attn(roped, plain, lam, subln, batch, seq, lambda_init)
    n_nsa = NSA_HEADS * hd
    w_o = w_out.astype(BF16)
    return _outproj([o_nsa, o_diff], [w_o[:n_nsa], w_o[n_nsa:]], h, g_ffn, wr, br)


def kernel(x, p, norm_mix, norm_ffn, norm_ple, norm_final, ev_w_in, ev_conv_w, ev_conv_b, ev_dt_bias, ev_a_log, ev_d_skip, ev_gate_norm, ev_sc_w, ev_w_out, od_w_in, od_cmp_pos, od_cmp_w1, od_cmp_w2, od_lambda, od_subln, od_w_out, moe_w_group, moe_b_group, moe_w_expert, moe_b_expert, moe_w_gate, moe_w_up, moe_w_down, ple_gate, ple_proj):
    batch, seq, d = x.shape
    n = batch * seq
    depth = p.shape[0]
    h = x.reshape(n, d)
    hn = _rmsnorm(h, norm_mix[0], BF16)
    for i in range(depth):
        j = i // 2
        wr, br = _router_weights(moe_w_group[i], moe_b_group[i], moe_w_expert[i], moe_b_expert[i])
        if i % 2 == 0:
            h, hn2, logits = _even_layer(h, hn, batch, seq, ev_w_in[j], ev_conv_w[j], ev_conv_b[j], ev_dt_bias[j],
                                         ev_a_log[j], ev_d_skip[j], ev_gate_norm[j], ev_sc_w[j], ev_w_out[j],
                                         norm_ffn[i], wr, br)
        else:
            lambda_init = 0.8 - 0.6 * math.exp(-0.3 * i)
            h, hn2, logits = _odd_layer(h, hn, batch, seq, od_w_in[j], od_cmp_pos[j], od_cmp_w1[j], od_cmp_w2[j],
                                        od_lambda[j], od_subln[j], od_w_out[j], lambda_init, norm_ffn[i], wr, br)
        y1, y2, routed = _moe(hn2, logits, i, moe_w_gate, moe_w_up, moe_w_down)
        last = i == depth - 1
        g_next = norm_final if last else norm_mix[i + 1]
        h, hn = _ple(h, y1, y2, routed, p[i].reshape(n, PLE_DIM), norm_ple[i], ple_gate[i].astype(BF16),
                     ple_proj[i].astype(BF16), g_next, F32 if last else BF16)
    return hn.reshape(batch, seq, d)
```

```python
import functools
import math

import jax
import jax.numpy as jnp
from jax import lax
from jax.experimental import pallas as pl
from jax.experimental.pallas import tpu as pltpu

F32 = jnp.float32
BF16 = jnp.bfloat16
HIGHEST = lax.Precision.HIGHEST

LANES = 128
D_MODEL = 2048
NORM_EPS = 1e-6
ROPE_THETA = 500000.0
ROPE_HALF = 16
NEG_INF = -1e30

SSD_HEADS = 32
SSD_HEAD_DIM = 64
SSD_GROUPS = 8
SSD_STATE = 128
SSD_CHUNK = 128
SSD_INNER = 2048
CONV_HALO = 8

HEAD_DIM = 128
NSA_HEADS = 8
NSA_GROUPS = 2
NSA_REP = 4
CMP_STRIDE = 16
CMP_CHUNKS = 2
SEL_BLOCK = 64
SEL_TOPK = 16
SEL_FORCE = 1e4
WINDOW = 512
DIFF_HEADS = 4
DIFF_V = 256

MOE_GROUPS = 4
MOE_PER_GROUP = 8
MOE_EXPERTS = 32
MOE_FF = 1024
MOE_ROWS = 256
ROUTER_TILE = 512
PLE_DIM = 256

VMEM_LIMIT = 56 * 1024 * 1024


def _cparams(sem, vmem=VMEM_LIMIT):
    return pltpu.CompilerParams(dimension_semantics=sem, vmem_limit_bytes=vmem)


def _nt_dot(a, b):
    return lax.dot_general(a, b, (((1,), (1,)), ((), ())), preferred_element_type=F32)


def _silu(x):
    return x * jax.nn.sigmoid(x)


def _rmsnorm_kernel(x_ref, g_ref, o_ref):
    x = x_ref[...]
    ms = jnp.mean(x * x, axis=-1, keepdims=True)
    o_ref[...] = (x * lax.rsqrt(ms + NORM_EPS) * g_ref[...]).astype(o_ref.dtype)


def _rmsnorm(x, g, out_dtype, tm=512):
    n, d = x.shape
    return pl.pallas_call(
        _rmsnorm_kernel,
        grid=(n // tm,),
        in_specs=[pl.BlockSpec((tm, d), lambda i: (i, 0)), pl.BlockSpec((1, d), lambda i: (0, 0))],
        out_specs=pl.BlockSpec((tm, d), lambda i: (i, 0)),
        out_shape=jax.ShapeDtypeStruct((n, d), out_dtype),
        compiler_params=_cparams(("parallel",)),
        name="rmsnorm",
    )(x, g.reshape(1, d))


def _rope_apply(x, cosf, sinf, lane):
    swapped = jnp.where(lane < ROPE_HALF, pltpu.roll(x, LANES - ROPE_HALF, 1), pltpu.roll(x, ROPE_HALF, 1))
    return x * cosf + swapped * sinf


def _mm_kernel(a_ref, w_ref, o_ref):
    o_ref[...] = jnp.dot(a_ref[...], w_ref[...], preferred_element_type=F32).astype(o_ref.dtype)


def _mm_rope_kernel(a_ref, w_ref, cos_ref, sin_ref, o_ref):
    acc = jnp.dot(a_ref[...], w_ref[...], preferred_element_type=F32)
    tm, tn = acc.shape
    cosf = cos_ref[...]
    sinf = sin_ref[...]
    lane = lax.broadcasted_iota(jnp.int32, (tm, LANES), 1)
    for h in range(tn // LANES):
        sl = slice(h * LANES, (h + 1) * LANES)
        o_ref[:, sl] = _rope_apply(acc[:, sl], cosf, sinf, lane).astype(o_ref.dtype)


def _matmul(a, w, out_dtype, tm, tn, rope=None, seq=None):
    m, k = a.shape
    n = w.shape[1]
    in_specs = [pl.BlockSpec((tm, k), lambda i, j: (i, 0)), pl.BlockSpec((k, tn), lambda i, j: (0, j))]
    args = [a, w]
    body = _mm_kernel
    if rope is not None:
        per_seq = seq // tm
        in_specs += [pl.BlockSpec((tm, LANES), lambda i, j: (i % per_seq, 0))] * 2
        args += list(rope)
        body = _mm_rope_kernel
    return pl.pallas_call(
        body,
        grid=(m // tm, n // tn),
        in_specs=in_specs,
        out_specs=pl.BlockSpec((tm, tn), lambda i, j: (i, j)),
        out_shape=jax.ShapeDtypeStruct((m, n), out_dtype),
        compiler_params=_cparams(("parallel", "parallel")),
        name="proj_rope" if rope is not None else "proj",
    )(*args)


def _even_core_kernel(z_ref, xs_ref, bc_ref, scb_ref, scc_ref, sch_ref,
                      xs_h_ref, bc_h_ref, scc_h_ref, sch_h_ref, dt_ref,
                      cwx_ref, cwbc_ref, cbx_ref, cbbc_ref, dtb_ref, alog_ref, dskip_ref, gnorm_ref, scw_ref,
                      o_ref, state_ref, cbuf_ref, xsc_ref, bcc_ref, y_ref):
    q = SSD_CHUNK
    first = pl.program_id(1) == 0
    keep = jnp.where(first, 0.0, 1.0).astype(F32)

    @pl.when(first)
    def _():
        state_ref[...] = jnp.zeros_like(state_ref)

    strip = 512

    def causal_conv(load_main, load_halo, w_ref, width, finish):
        cbuf_ref[0:CONV_HALO, :] = load_halo() * keep
        cbuf_ref[CONV_HALO:CONV_HALO + q, :] = load_main()
        for c0 in range(0, SSD_INNER, strip):
            cs = slice(c0, c0 + strip)
            acc = None
            for k in range(width):
                r0 = CONV_HALO - (width - 1) + k
                term = w_ref[k:k + 1, cs] * cbuf_ref[r0:r0 + q, cs]
                acc = term if acc is None else acc + term
            finish(cs, acc)

    def fin_xs(cs, acc):
        xsc_ref[:, cs] = _silu(acc + cbx_ref[:, cs])

    def fin_bc(cs, acc):
        bcc_ref[:, cs] = _silu(acc + cbbc_ref[:, cs])

    causal_conv(lambda: xs_ref[...], lambda: xs_h_ref[...], cwx_ref, 4, fin_xs)
    causal_conv(lambda: bc_ref[...], lambda: bc_h_ref[...], cwbc_ref, 4, fin_bc)

    def fin_sc(cs, acc):
        o_ref[:, SSD_INNER + cs.start:SSD_INNER + cs.stop] = (scb_ref[:, cs] * acc).astype(o_ref.dtype)

    causal_conv(lambda: scc_ref[...] * sch_ref[...], lambda: scc_h_ref[...] * sch_h_ref[...], scw_ref, 3, fin_sc)

    dt = jax.nn.softplus(dt_ref[...] + dtb_ref[...])
    d_a = dt * (-jnp.exp(alog_ref[...]))
    row = lax.broadcasted_iota(jnp.int32, (q, q), 0)
    col = lax.broadcasted_iota(jnp.int32, (q, q), 1)
    causal = row >= col
    tril = jnp.where(causal, 1.0, 0.0).astype(F32)
    a_cum = jnp.dot(tril, d_a, preferred_element_type=F32, precision=HIGHEST)
    a_last = a_cum[q - 1:q, :]
    decay_end = jnp.exp(a_last - a_cum)
    chunk_decay = jnp.exp(a_last)
    exp_acum = jnp.exp(a_cum)
    a_cum_t = a_cum.T
    lane = lax.broadcasted_iota(jnp.int32, (q, LANES), 1)
    lo = lane < SSD_HEAD_DIM
    lane_row = lax.broadcasted_iota(jnp.int32, (1, LANES), 1)
    lo_row = lane_row < SSD_HEAD_DIM

    def pair_cols(mat, h0):
        return jnp.where(lo, mat[:, h0:h0 + 1], mat[:, h0 + 1:h0 + 2])

    for g in range(SSD_GROUPS):
        gs = slice(g * SSD_STATE, (g + 1) * SSD_STATE)
        b_g = bcc_ref[:, gs]
        c_g = bcc_ref[:, SSD_GROUPS * SSD_STATE + g * SSD_STATE:SSD_GROUPS * SSD_STATE + (g + 1) * SSD_STATE]
        b_gt = b_g.T.astype(BF16)
        c_gb = c_g.astype(BF16)
        cb = jnp.dot(c_gb, b_gt, preferred_element_type=F32)
        for hp in range(2):
            pr = g * 2 + hp
            h0 = 2 * pr
            ps = slice(pr * LANES, (pr + 1) * LANES)
            xp = xsc_ref[:, ps]
            xdt = xp * pair_cols(dt, h0)
            mats = []
            for h in (h0, h0 + 1):
                seg = a_cum[:, h:h + 1] - a_cum_t[h:h + 1, :]
                dec = jnp.where(causal, jnp.exp(jnp.minimum(seg, 0.0)), 0.0)
                mats.append((cb * dec).astype(BF16))
            lhs = jnp.concatenate(mats, axis=1)
            rhs = jnp.concatenate([jnp.where(lo, xdt, 0.0), jnp.where(lo, 0.0, xdt)], axis=0).astype(BF16)
            y = jnp.dot(lhs, rhs, preferred_element_type=F32)
            st = state_ref[pr]
            y = y + jnp.dot(c_gb, st.astype(BF16), preferred_element_type=F32) * pair_cols(exp_acum, h0)
            xw = (xdt * pair_cols(decay_end, h0)).astype(BF16)
            cd = jnp.where(lo_row, chunk_decay[:, h0:h0 + 1], chunk_decay[:, h0 + 1:h0 + 2])
            state_ref[pr] = st * cd + jnp.dot(b_gt, xw, preferred_element_type=F32)
            y_ref[:, ps] = y + dskip_ref[:, ps] * xp

    gw = SSD_INNER // SSD_GROUPS
    for g in range(SSD_GROUPS):
        cs = slice(g * gw, (g + 1) * gw)
        yg = y_ref[:, cs] * _silu(z_ref[:, cs])
        ms = jnp.mean(yg * yg, axis=-1, keepdims=True)
        o_ref[:, cs] = (yg * lax.rsqrt(ms + NORM_EPS) * gnorm_ref[:, cs]).astype(o_ref.dtype)


def _even_core(proj, dt_raw, batch, seq, conv_w, conv_b, dt_bias, a_log, d_skip, gate_norm, sc_w):
    q = SSD_CHUNK
    nc = seq // q
    w = SSD_INNER
    hb = q // CONV_HALO

    def main(cb):
        return pl.BlockSpec((q, w), lambda b, c: (b * nc + c, cb))

    def halo(cb):
        return pl.BlockSpec((CONV_HALO, w), lambda b, c: (jnp.maximum((b * nc + c) * hb - 1, 0), cb))

    def full(shape):
        return pl.BlockSpec(shape, lambda b, c: (0, 0))

    pad = LANES - SSD_HEADS
    dtb = jnp.pad(dt_bias, (0, pad)).reshape(1, LANES)
    alog = jnp.pad(a_log, (0, pad)).reshape(1, LANES)
    dskip = jnp.repeat(d_skip, SSD_HEAD_DIM).reshape(1, w)
    in_specs = [main(0), main(1), main(2), main(3), main(4), main(5),
                halo(1), halo(2), halo(4), halo(5),
                pl.BlockSpec((q, LANES), lambda b, c: (b * nc + c, 0)),
                full((4, w)), full((4, w)), full((1, w)), full((1, w)),
                full((1, LANES)), full((1, LANES)), full((1, w)), full((1, w)), full((3, w))]
    return pl.pallas_call(
        _even_core_kernel,
        grid=(batch, nc),
        in_specs=in_specs,
        out_specs=pl.BlockSpec((q, 2 * w), lambda b, c: (b * nc + c, 0)),
        out_shape=jax.ShapeDtypeStruct((batch * seq, 2 * w), BF16),
        scratch_shapes=[pltpu.VMEM((SSD_HEADS // 2, SSD_STATE, LANES), F32),
                        pltpu.VMEM((CONV_HALO + q, w), F32),
                        pltpu.VMEM((q, w), F32), pltpu.VMEM((q, w), F32), pltpu.VMEM((q, w), F32)],
        compiler_params=_cparams(("parallel", "arbitrary")),
        name="ssd_conv_core",
    )(proj, proj, proj, proj, proj, proj, proj, proj, proj, proj, dt_raw,
      conv_w[:, :w], conv_w[:, w:], conv_b[:w].reshape(1, w), conv_b[w:].reshape(1, w),
      dtb, alog, dskip, gate_norm.reshape(1, w), sc_w)


def _outproj_kernel(*refs, n_in):
    mix = refs[:n_in]
    ws = refs[n_in:2 * n_in]
    h_ref, g_ref, wr_ref, br_ref, hnew_ref, hn_ref, lg_ref = refs[2 * n_in:]
    acc = h_ref[...]
    for m_ref, w_ref in zip(mix, ws):
        acc = acc + jnp.dot(m_ref[...], w_ref[...], preferred_element_type=F32)
    hnew_ref[...] = acc
    ms = jnp.mean(acc * acc, axis=-1, keepdims=True)
    hn = acc * lax.rsqrt(ms + NORM_EPS) * g_ref[...]
    hi = hn.astype(BF16)
    hn_ref[...] = hi
    lo = (hn - hi.astype(F32)).astype(BF16)
    wr = wr_ref[...]
    t = jnp.dot(hi, wr, preferred_element_type=F32)
    u = jnp.dot(lo, wr[:, :LANES], preferred_element_type=F32)
    lg_ref[...] = t[:, :LANES] + t[:, LANES:] + u + br_ref[...]


def _outproj(mixes, ws, h, g, wr, br, tm=256):
    n, d = h.shape
    n_in = len(mixes)
    in_specs = ([pl.BlockSpec((tm, m.shape[1]), lambda i: (i, 0)) for m in mixes]
                + [pl.BlockSpec(w.shape, lambda i: (0, 0)) for w in ws]
                + [pl.BlockSpec((tm, d), lambda i: (i, 0)), pl.BlockSpec((1, d), lambda i: (0, 0)),
                   pl.BlockSpec((d, 2 * LANES), lambda i: (0, 0)), pl.BlockSpec((1, LANES), lambda i: (0, 0))])
    return pl.pallas_call(
        functools.partial(_outproj_kernel, n_in=n_in),
        grid=(n // tm,),
        in_specs=in_specs,
        out_specs=[pl.BlockSpec((tm, d), lambda i: (i, 0)), pl.BlockSpec((tm, d), lambda i: (i, 0)),
                   pl.BlockSpec((tm, LANES), lambda i: (i, 0))],
        out_shape=[jax.ShapeDtypeStruct((n, d), F32), jax.ShapeDtypeStruct((n, d), BF16),
                   jax.ShapeDtypeStruct((n, LANES), F32)],
        compiler_params=_cparams(("parallel",)),
        name="outproj_norm_router",
    )(*mixes, *ws, h, g.reshape(1, d), wr, br)


ROUTE_E1, ROUTE_E2, ROUTE_W1, ROUTE_W2, ROUTE_R1, ROUTE_R2 = range(6)


def _router_kernel(lg_ref, out_ref, cnt_ref, run_ref):
    @pl.when(pl.program_id(0) == 0)
    def _():
        run_ref[...] = jnp.zeros_like(run_ref)

    x = lg_ref[...]
    t = x.shape[0]
    lane = lax.broadcasted_iota(jnp.int32, (t, LANES), 1).astype(F32)
    far = float(LANES)
    gmask = lane < MOE_GROUPS
    gl = jnp.where(gmask, x, NEG_INF)
    gmax = jnp.max(gl, axis=-1, keepdims=True)
    g_top = jnp.min(jnp.where(gl == gmax, lane, far), axis=-1, keepdims=True)
    gsum = jnp.sum(jnp.where(gmask, jnp.exp(gl - gmax), 0.0), axis=-1, keepdims=True)
    g_prob = 1.0 / gsum
    lo = MOE_GROUPS + MOE_PER_GROUP * g_top
    emask = (lane >= lo) & (lane < lo + MOE_PER_GROUP)
    el = jnp.where(emask, x, NEG_INF)
    emax = jnp.max(el, axis=-1, keepdims=True)
    ee = jnp.where(emask, jnp.exp(el - emax), 0.0)
    ep = jnp.where(emask, ee / jnp.sum(ee, axis=-1, keepdims=True), -1.0)
    p1 = jnp.max(ep, axis=-1, keepdims=True)
    i1 = jnp.min(jnp.where(ep == p1, lane, far), axis=-1, keepdims=True)
    ep2 = jnp.where(lane == i1, -1.0, ep)
    p2 = jnp.max(ep2, axis=-1, keepdims=True)
    i2 = jnp.min(jnp.where(ep2 == p2, lane, far), axis=-1, keepdims=True)
    den = p1 + p2
    w1 = g_prob * p1 / den
    w2 = g_prob * p2 / den
    oh1 = jnp.where(lane == i1, 1.0, 0.0)
    oh2 = jnp.where(lane == i2, 1.0, 0.0)
    oh = oh1 + oh2
    row = lax.broadcasted_iota(jnp.int32, (t, t), 0)
    col = lax.broadcasted_iota(jnp.int32, (t, t), 1)
    strict = jnp.where(row > col, 1.0, 0.0).astype(BF16)
    before = jnp.dot(strict, oh.astype(BF16), preferred_element_type=F32) + run_ref[...]
    r1 = jnp.sum(oh1 * before, axis=-1, keepdims=True)
    r2 = jnp.sum(oh2 * before, axis=-1, keepdims=True)
    run_ref[...] = run_ref[...] + jnp.sum(oh, axis=0, keepdims=True)
    cnt_ref[...] = run_ref[...]
    packed = jnp.zeros((t, LANES), F32)
    for k, v in ((ROUTE_E1, i1 - MOE_GROUPS), (ROUTE_E2, i2 - MOE_GROUPS), (ROUTE_W1, w1), (ROUTE_W2, w2),
                 (ROUTE_R1, r1), (ROUTE_R2, r2)):
        packed = jnp.where(lane == k, v, packed)
    out_ref[...] = packed


def _router(logits, tt=ROUTER_TILE):
    n = logits.shape[0]
    return pl.pallas_call(
        _router_kernel,
        grid=(n // tt,),
        in_specs=[pl.BlockSpec((tt, LANES), lambda i: (i, 0))],
        out_specs=[pl.BlockSpec((tt, LANES), lambda i: (i, 0)), pl.BlockSpec((1, LANES), lambda i: (0, 0))],
        out_shape=[jax.ShapeDtypeStruct((n, LANES), F32), jax.ShapeDtypeStruct((1, LANES), F32)],
        scratch_shapes=[pltpu.VMEM((1, LANES), F32)],
        compiler_params=_cparams(("arbitrary",)),
        name="router_rank",
    )(logits)


SLOT_1, SLOT_2 = 0, 1
BLK_HALVES = 2


def _slot_kernel(rt_ref, cnt_ref, slot_ref, blk_ref):
    lane_row = lax.broadcasted_iota(jnp.int32, (1, LANES), 1)
    is_exp = (lane_row >= MOE_GROUPS) & (lane_row < MOE_GROUPS + MOE_EXPERTS)
    padded = jnp.where(is_exp, jnp.floor((cnt_ref[...] + (MOE_ROWS - 1)) * (1.0 / MOE_ROWS)) * MOE_ROWS, 0.0)
    r = lax.broadcasted_iota(jnp.int32, (LANES, LANES), 0)
    c = lax.broadcasted_iota(jnp.int32, (LANES, LANES), 1)
    incl = jnp.where(r <= c, 1.0, 0.0).astype(F32)
    pad_end = jnp.dot(jnp.broadcast_to(padded, (8, LANES)), incl, preferred_element_type=F32,
                      precision=HIGHEST)[0:1]
    pad_start = pad_end - padded
    rt = rt_ref[...]
    lane = lax.broadcasted_iota(jnp.int32, rt.shape, 1).astype(F32)

    def slot(e, rank):
        return jnp.sum(jnp.where(lane == e + MOE_GROUPS, pad_start, 0.0), axis=-1, keepdims=True) + rank

    s1 = slot(rt[:, ROUTE_E1:ROUTE_E1 + 1], rt[:, ROUTE_R1:ROUTE_R1 + 1])
    s2 = slot(rt[:, ROUTE_E2:ROUTE_E2 + 1], rt[:, ROUTE_R2:ROUTE_R2 + 1])
    slot_ref[...] = jnp.where(lane == SLOT_1, s1, jnp.where(lane == SLOT_2, s2, 0.0))

    end_col = jnp.broadcast_to(pad_end, (LANES, LANES)).T
    exp_col = (r >= MOE_GROUPS) & (r < MOE_GROUPS + MOE_EXPERTS)
    last = MOE_GROUPS + MOE_EXPERTS - 1
    total = pad_end[:, last:last + 1]

    def experts_done(start):
        return jnp.sum(jnp.where(exp_col & (end_col <= start), 1.0, 0.0), axis=0, keepdims=True)

    idle = experts_done(total - MOE_ROWS)
    rows = []
    for h in range(BLK_HALVES):
        start = (lane_row + h * LANES).astype(F32) * MOE_ROWS
        valid = start < total
        rows += [jnp.where(valid, experts_done(start), idle), jnp.where(valid, 1.0, 0.0)]
    blk_ref[...] = jnp.concatenate(rows + [jnp.zeros((8 - 2 * BLK_HALVES, LANES), F32)], axis=0)


def _slots(routed, counts, tt=ROUTER_TILE):
    n = routed.shape[0]
    return pl.pallas_call(
        _slot_kernel,
        grid=(n // tt,),
        in_specs=[pl.BlockSpec((tt, LANES), lambda i: (i, 0)), pl.BlockSpec((1, LANES), lambda i: (0, 0))],
        out_specs=[pl.BlockSpec((tt, LANES), lambda i: (i, 0)), pl.BlockSpec((8, LANES), lambda i: (0, 0))],
        out_shape=[jax.ShapeDtypeStruct((n, LANES), F32), jax.ShapeDtypeStruct((8, LANES), F32)],
        compiler_params=_cparams(("arbitrary",)),
        name="moe_slots",
    )(routed, counts)


def _moe_kernel(be_ref, bv_ref, x_ref, wg_ref, wu_ref, wd_ref, o_ref):
    i = pl.program_id(0)

    @pl.when(bv_ref[i] > 0)
    def _():
        x = x_ref[...]
        step = 256
        for n0 in range(0, MOE_FF, step):
            gate = jnp.dot(x, wg_ref[:, n0:n0 + step].astype(BF16), preferred_element_type=F32)
            up = jnp.dot(x, wu_ref[:, n0:n0 + step].astype(BF16), preferred_element_type=F32)
            hid = (_silu(gate) * up).astype(BF16)
            part = jnp.dot(hid, wd_ref[n0:n0 + step, :].astype(BF16), preferred_element_type=F32)
            if n0 == 0:
                o_ref[...] = part
            else:
                o_ref[...] += part

    @pl.when(bv_ref[i] == 0)
    def _():
        o_ref[...] = jnp.zeros_like(o_ref)


def _moe_ffn(xs, blk_expert, blk_valid, layer, wg, wu, wd):
    cap, d = xs.shape
    nb = cap // MOE_ROWS
    ff = wg.shape[3]
    grid_spec = pltpu.PrefetchScalarGridSpec(
        num_scalar_prefetch=2,
        grid=(nb,),
        in_specs=[pl.BlockSpec((MOE_ROWS, d), lambda i, be, bv: (i, 0)),
                  pl.BlockSpec((None, None, d, ff), lambda i, be, bv: (layer, be[i], 0, 0)),
                  pl.BlockSpec((None, None, d, ff), lambda i, be, bv: (layer, be[i], 0, 0)),
                  pl.BlockSpec((None, None, ff, d), lambda i, be, bv: (layer, be[i], 0, 0),
                               pipeline_mode=pl.Buffered(1))],
        out_specs=pl.BlockSpec((MOE_ROWS, d), lambda i, be, bv: (i, 0)),
    )
    return pl.pallas_call(
        _moe_kernel,
        grid_spec=grid_spec,
        out_shape=jax.ShapeDtypeStruct((cap, d), F32),
        compiler_params=_cparams(("arbitrary",)),
        name="moe_ffn",
    )(blk_expert, blk_valid, xs, wg, wu, wd)


def _moe(hn, logits, layer, wg, wu, wd):
    n, d = hn.shape
    routed, counts = _router(logits)
    slots, blk = _slots(routed, counts)
    slot1 = slots[:, SLOT_1].astype(jnp.int32)
    slot2 = slots[:, SLOT_2].astype(jnp.int32)
    nb = -(-(2 * n + MOE_EXPERTS * (MOE_ROWS - 1)) // MOE_ROWS)
    assert nb <= BLK_HALVES * LANES
    cap = nb * MOE_ROWS
    blk_expert = blk[0:2 * BLK_HALVES:2].reshape(-1)[:nb].astype(jnp.int32)
    blk_valid = blk[1:2 * BLK_HALVES:2].reshape(-1)[:nb].astype(jnp.int32)
    tok = jnp.arange(n, dtype=jnp.int32)
    slot_tok = jnp.zeros((cap,), jnp.int32).at[slot1].set(tok).at[slot2].set(tok)
    rows = lambda a, idx: a.at[idx].get(mode="promise_in_bounds")
    yb = _moe_ffn(rows(hn, slot_tok), blk_expert, blk_valid, layer, wg, wu, wd)
    return rows(yb, slot1), rows(yb, slot2), routed


def _ple_kernel(h_ref, y1_ref, y2_ref, rt_ref, p_ref, gp_ref, wg_ref, wp_ref, gn_ref, hout_ref, nout_ref):
    rt = rt_ref[...]
    moe = y1_ref[...] * rt[:, ROUTE_W1:ROUTE_W1 + 1] + y2_ref[...] * rt[:, ROUTE_W2:ROUTE_W2 + 1]
    h = h_ref[...] + moe
    ms = jnp.mean(h * h, axis=-1, keepdims=True)
    hn = (h * lax.rsqrt(ms + NORM_EPS) * gp_ref[...]).astype(BF16)
    gate = jax.nn.sigmoid(jnp.dot(hn, wg_ref[...], preferred_element_type=F32))
    emb = jnp.dot(p_ref[...].astype(BF16), wp_ref[...], preferred_element_type=F32)
    h = h + gate * emb
    hout_ref[...] = h
    ms = jnp.mean(h * h, axis=-1, keepdims=True)
    nout_ref[...] = (h * lax.rsqrt(ms + NORM_EPS) * gn_ref[...]).astype(nout_ref.dtype)


def _ple(h, y1, y2, routed, p, layer, g_ple, w_gate, w_proj, g_next, next_dtype, tm=256):
    n, d = h.shape
    row = lambda i: (i, 0)
    fixed = lambda i: (0, 0)
    return pl.pallas_call(
        _ple_kernel,
        grid=(n // tm,),
        in_specs=[pl.BlockSpec((tm, d), row), pl.BlockSpec((tm, d), row), pl.BlockSpec((tm, d), row),
                  pl.BlockSpec((tm, LANES), row), pl.BlockSpec((tm, PLE_DIM), lambda i: (layer * (n // tm) + i, 0)),
                  pl.BlockSpec((1, d), fixed), pl.BlockSpec((d, d), fixed), pl.BlockSpec((PLE_DIM, d), fixed),
                  pl.BlockSpec((1, d), fixed)],
        out_specs=[pl.BlockSpec((tm, d), row), pl.BlockSpec((tm, d), row)],
        out_shape=[jax.ShapeDtypeStruct((n, d), F32), jax.ShapeDtypeStruct((n, d), next_dtype)],
        compiler_params=_cparams(("parallel",)),
        name="ple_norm",
    )(h, y1, y2, routed, p, g_ple.reshape(1, d), w_gate, w_proj, g_next.reshape(1, d))


def _compress_kernel(x_ref, pos_ref, w1_ref, w2_ref, cos_ref, sin_ref, o_ref):
    nchunk = x_ref.shape[0]
    width = 2 * NSA_GROUPS * HEAD_DIM
    lane = lax.broadcasted_iota(jnp.int32, (nchunk, LANES), 1)
    for kv in range(2):
        for g in range(NSA_GROUPS):
            acc = [jnp.zeros((nchunk, HEAD_DIM), F32) for _ in range(CMP_CHUNKS)]
            for half in range(CMP_CHUNKS):
                for j in range(CMP_STRIDE):
                    c0 = j * width + kv * NSA_GROUPS * HEAD_DIM + g * HEAD_DIM
                    jj = half * CMP_STRIDE + j
                    blk = (x_ref[:, c0:c0 + HEAD_DIM] + pos_ref[kv, jj:jj + 1, :]).astype(BF16)
                    acc[half] = acc[half] + jnp.dot(blk, w1_ref[kv, jj * HEAD_DIM:(jj + 1) * HEAD_DIM, :],
                                                    preferred_element_type=F32)
            pre = acc[0] + pltpu.roll(acc[1], nchunk - 1, 0)
            out = jnp.dot(_silu(pre).astype(BF16), w2_ref[kv], preferred_element_type=F32)
            if kv == 0:
                out = _rope_apply(out, cos_ref[...], sin_ref[...], lane)
            o_ref[kv * NSA_GROUPS + g] = out.astype(o_ref.dtype)


def _compress(kv_cmp, batch, seq, cmp_pos, cmp_w1, cmp_w2, cos_c, sin_c):
    nchunk = seq // CMP_STRIDE
    width = kv_cmp.shape[1]
    x = kv_cmp.reshape(batch * nchunk, CMP_STRIDE * width)
    return pl.pallas_call(
        _compress_kernel,
        grid=(batch,),
        in_specs=[pl.BlockSpec((nchunk, CMP_STRIDE * width), lambda b: (b, 0)),
                  pl.BlockSpec(cmp_pos.shape, lambda b: (0, 0, 0)),
                  pl.BlockSpec(cmp_w1.shape, lambda b: (0, 0, 0)),
                  pl.BlockSpec(cmp_w2.shape, lambda b: (0, 0, 0)),
                  pl.BlockSpec((nchunk, LANES), lambda b: (0, 0)),
                  pl.BlockSpec((nchunk, LANES), lambda b: (0, 0))],
        out_specs=pl.BlockSpec((None, 2 * NSA_GROUPS, nchunk, HEAD_DIM), lambda b: (b, 0, 0, 0)),
        out_shape=jax.ShapeDtypeStruct((batch, 2 * NSA_GROUPS, nchunk, HEAD_DIM), BF16),
        compiler_params=_cparams(("parallel",)),
        name="nsa_compress",
    )(x, cmp_pos, cmp_w1.astype(BF16), cmp_w2.astype(BF16), cos_c, sin_c)


NSA_TQ = 128
NSA_TK = 512
EXP2_SCALE = HEAD_DIM ** -0.5 * math.log2(math.e)


def _softmax_tile(s, m_old, l_old):
    m_new = jnp.maximum(m_old, jnp.max(s, axis=-1, keepdims=True))
    alpha = jnp.exp2(EXP2_SCALE * (m_old - m_new))
    p = jnp.exp2(EXP2_SCALE * (s - jnp.concatenate([m_new] * (s.shape[1] // LANES), axis=1)))
    l_new = alpha * l_old + jnp.sum(p, axis=-1, keepdims=True)
    return m_new, l_new, alpha, p


def _nsa_kernel(q_ref, kc_ref, vc_ref, ksel_ref, vsel_ref, kwin_ref, vwin_ref, gate_ref, pool_ref, negexp_ref,
                o_ref, acc_ref):
    tq = NSA_TQ
    rep = NSA_REP
    q0 = pl.program_id(2) * tq
    q = q_ref[...]
    qs = jnp.concatenate([q[:, r * HEAD_DIM:(r + 1) * HEAD_DIM] for r in range(rep)], axis=0)
    tpos = q0 + lax.broadcasted_iota(jnp.int32, (tq, 1), 0)
    heads = [slice(r * tq, (r + 1) * tq) for r in range(rep)]

    ncmp = kc_ref.shape[0]
    cend = lax.broadcasted_iota(jnp.int32, (1, ncmp), 1) * CMP_STRIDE + (CMP_CHUNKS * CMP_STRIDE - 1)
    cvalid = cend <= tpos
    cbias = jnp.where(cvalid, 0.0, NEG_INF).astype(F32)
    s_all = _nt_dot(qs, kc_ref[...])
    probs = []
    for hs in heads:
        s = s_all[hs] + cbias
        m = jnp.max(s, axis=-1, keepdims=True)
        p = jnp.where(cvalid, jnp.exp2(EXP2_SCALE * (s - m)), 0.0)
        l = jnp.sum(p, axis=-1, keepdims=True)
        probs.append(p * jnp.where(l > 0.0, 1.0 / l, 0.0))
    o_cmp = jnp.dot(jnp.concatenate(probs, axis=0).astype(BF16), vc_ref[...], preferred_element_type=F32)
    imp = probs[0]
    for p in probs[1:]:
        imp = imp + p
    p_sel = jnp.dot(imp, pool_ref[...], preferred_element_type=F32, precision=HIGHEST)

    nsel = ksel_ref.shape[0] // SEL_BLOCK
    blk = lax.broadcasted_iota(jnp.int32, (tq, LANES), 1)
    cur = tpos // SEL_BLOCK
    forced = (blk == 0) | (blk == cur) | (blk == cur - 1)
    valid = blk * SEL_BLOCK <= tpos
    score = jnp.where(forced, SEL_FORCE, jnp.where(valid, p_sel, NEG_INF))
    score = jnp.where(blk < nsel, score, 3.0 * NEG_INF)
    st = score.T
    sub = 8
    chunks = [st[c * sub:(c + 1) * sub, :] for c in range(nsel // sub)]
    cnt = [jnp.zeros((sub, tq), F32) for _ in chunks]
    jrow = lax.broadcasted_iota(jnp.int32, (sub, tq), 0)
    for i in range(nsel):
        ci, ii = divmod(i, sub)
        si = chunks[ci][ii:ii + 1, :]
        for c in range(len(chunks)):
            if c > ci:
                beats = si >= chunks[c]
            elif c < ci:
                beats = si > chunks[c]
            else:
                beats = jnp.where(jrow > ii, jnp.where(si >= chunks[c], 1.0, 0.0), jnp.where(si > chunks[c], 1.0, 0.0)) > 0.5
            cnt[c] = cnt[c] + jnp.where(beats, 1.0, 0.0)
    tpos_row = q0 + lax.broadcasted_iota(jnp.int32, (sub, tq), 1)
    dropped = []
    for c, cn in enumerate(cnt):
        keep_blk = (cn < min(SEL_TOPK, nsel)) & ((jrow + c * sub) * SEL_BLOCK <= tpos_row)
        dropped.append(jnp.where(keep_blk, 0.0, 1.0))
    dropped_t = jnp.concatenate(dropped + [jnp.zeros((LANES - nsel, tq), F32)], axis=0)
    dropped_q = dropped_t.T.astype(BF16)
    q_aug = jnp.concatenate([qs, jnp.concatenate([dropped_q] * rep, axis=0)], axis=1)

    tk = NSA_TK
    last = q0 // tk
    acc_ref[...] = jnp.zeros_like(acc_ref)

    def sel_tile(kt, carry, diagonal):
        m_i, l_i = carry
        k0 = pl.multiple_of(kt * tk, tk)
        k_aug = jnp.concatenate([ksel_ref[pl.ds(k0, tk), :], negexp_ref[pl.ds(k0, tk), :]], axis=1)
        v_t = vsel_ref[pl.ds(k0, tk), :]
        s_t = _nt_dot(q_aug, k_aug)
        if diagonal:
            kpos = k0 + lax.broadcasted_iota(jnp.int32, (1, tk), 1)
            causal = jnp.where(kpos <= tpos, 0.0, NEG_INF).astype(F32)
        m_out, l_out, alphas, ps = [], [], [], []
        for hs in heads:
            s = s_t[hs] + causal if diagonal else s_t[hs]
            m_new, l_new, alpha, p = _softmax_tile(s, m_i[hs], l_i[hs])
            m_out.append(m_new)
            l_out.append(l_new)
            alphas.append(alpha)
            ps.append(p.astype(BF16))
        pv = jnp.dot(jnp.concatenate(ps, axis=0), v_t, preferred_element_type=F32)
        acc_ref[...] = jnp.concatenate(alphas, axis=0) * acc_ref[...] + pv
        return jnp.concatenate(m_out, axis=0), jnp.concatenate(l_out, axis=0)

    init = (jnp.full((rep * tq, LANES), NEG_INF, F32), jnp.zeros((rep * tq, LANES), F32))
    carry = lax.fori_loop(0, last, functools.partial(sel_tile, diagonal=False), init)
    _, l_s = sel_tile(last, carry, True)
    o_sel = acc_ref[...] / l_s

    wk = WINDOW + tq
    w0 = pl.multiple_of(jnp.maximum(q0 - WINDOW, 0), tq)
    kw = kwin_ref[pl.ds(w0, wk), :]
    vw = vwin_ref[pl.ds(w0, wk), :]
    kpos = w0 + lax.broadcasted_iota(jnp.int32, (1, wk), 1)
    wbias = jnp.where((kpos <= tpos) & (kpos > tpos - WINDOW), 0.0, NEG_INF).astype(F32)
    sw_all = _nt_dot(qs, kw)
    pws, lws = [], []
    for hs in heads:
        sw = sw_all[hs] + wbias
        pw = jnp.exp2(EXP2_SCALE * (sw - jnp.max(sw, axis=-1, keepdims=True)))
        lws.append(jnp.sum(pw, axis=-1, keepdims=True))
        pws.append(pw.astype(BF16))
    o_win = jnp.dot(jnp.concatenate(pws, axis=0), vw, preferred_element_type=F32) / jnp.concatenate(lws, axis=0)

    gate = jax.nn.sigmoid(gate_ref[...])
    for r, rs in enumerate(heads):
        o = (gate[:, 3 * r:3 * r + 1] * o_cmp[rs] + gate[:, 3 * r + 1:3 * r + 2] * o_sel[rs]
             + gate[:, 3 * r + 2:3 * r + 3] * o_win[rs])
        o_ref[:, r * HEAD_DIM:(r + 1) * HEAD_DIM] = o.astype(o_ref.dtype)


def _nsa(roped, plain, cmp_kv, gates, batch, seq, pool, negexp):
    tq = NSA_TQ
    nq = seq // tq
    gw = NSA_REP * HEAD_DIM
    ksel_cb = NSA_HEADS
    kwin_cb = NSA_HEADS + NSA_GROUPS
    ncmp = seq // CMP_STRIDE
    seq_blk = lambda cb: pl.BlockSpec((seq, HEAD_DIM), functools.partial(lambda b, g, t, cb: (b, cb + g), cb=cb))
    return pl.pallas_call(
        _nsa_kernel,
        grid=(batch, NSA_GROUPS, nq),
        in_specs=[pl.BlockSpec((tq, gw), lambda b, g, t: (b * nq + t, g)),
                  pl.BlockSpec((None, None, ncmp, HEAD_DIM), lambda b, g, t: (b, g, 0, 0)),
                  pl.BlockSpec((None, None, ncmp, HEAD_DIM), lambda b, g, t: (b, NSA_GROUPS + g, 0, 0)),
                  seq_blk(ksel_cb), seq_blk(0), seq_blk(kwin_cb), seq_blk(NSA_GROUPS),
                  pl.BlockSpec((tq, LANES), lambda b, g, t: (b * nq + t, g)),
                  pl.BlockSpec(pool.shape, lambda b, g, t: (0, 0)),
                  pl.BlockSpec(negexp.shape, lambda b, g, t: (0, 0))],
        out_specs=pl.BlockSpec((tq, gw), lambda b, g, t: (b * nq + t, g)),
        out_shape=jax.ShapeDtypeStruct((batch * seq, NSA_HEADS * HEAD_DIM), BF16),
        scratch_shapes=[pltpu.VMEM((NSA_REP * tq, HEAD_DIM), F32)],
        compiler_params=_cparams(("parallel", "parallel", "arbitrary")),
        name="nsa_attention",
    )(roped, cmp_kv, cmp_kv, roped, plain, roped, plain, gates, pool, negexp)


DIFF_TQ = 256
DIFF_TK = 512


def _diff_kernel(q_ref, k_ref, v_ref, lam_ref, subln_ref, o_ref, acc_ref, *, lambda_init):
    tq, tk = DIFF_TQ, DIFF_TK
    q0 = pl.program_id(2) * tq
    last = q0 // tk
    q = q_ref[...]
    qm = (q[:, :HEAD_DIM], q[:, HEAD_DIM:])
    tpos = q0 + lax.broadcasted_iota(jnp.int32, (tq, 1), 0)
    maps = (slice(0, tq), slice(tq, 2 * tq))
    acc_ref[...] = jnp.zeros_like(acc_ref)

    def tile(kt, carry, diagonal):
        m_i, l_i = carry
        k0 = pl.multiple_of(kt * tk, tk)
        k = k_ref[pl.ds(k0, tk), :]
        v = v_ref[pl.ds(k0, tk), :]
        if diagonal:
            kpos = k0 + lax.broadcasted_iota(jnp.int32, (1, tk), 1)
            causal = jnp.where(kpos <= tpos, 0.0, NEG_INF).astype(F32)
        m_out, l_out, alphas, ps = [], [], [], []
        for i, ms in enumerate(maps):
            s = _nt_dot(qm[i], k[:, i * HEAD_DIM:(i + 1) * HEAD_DIM])
            if diagonal:
                s = s + causal
            m_new, l_new, alpha, p = _softmax_tile(s, m_i[ms], l_i[ms])
            m_out.append(m_new)
            l_out.append(l_new)
            alphas.append(jnp.concatenate([alpha] * (DIFF_V // LANES), axis=1))
            ps.append(p.astype(BF16))
        pv = jnp.dot(jnp.concatenate(ps, axis=0), v, preferred_element_type=F32)
        acc_ref[...] = jnp.concatenate(alphas, axis=0) * acc_ref[...] + pv
        return jnp.concatenate(m_out, axis=0), jnp.concatenate(l_out, axis=0)

    init = (jnp.full((2 * tq, LANES), NEG_INF, F32), jnp.zeros((2 * tq, LANES), F32))
    carry = lax.fori_loop(0, last, functools.partial(tile, diagonal=False), init)
    _, l_f = tile(last, carry, True)
    o = acc_ref[...] / jnp.concatenate([l_f] * (DIFF_V // LANES), axis=1)
    lam = lam_ref[...]
    lam_full = (jnp.exp(jnp.sum(lam[0:1] * lam[1:2], axis=-1, keepdims=True))
                - jnp.exp(jnp.sum(lam[2:3] * lam[3:4], axis=-1, keepdims=True)) + lambda_init)
    od = o[:tq] - lam_full * o[tq:]
    ms = jnp.mean(od * od, axis=-1, keepdims=True)
    o_ref[...] = (od * lax.rsqrt(ms + NORM_EPS) * subln_ref[...] * (1.0 - lambda_init)).astype(o_ref.dtype)


def _diff_attn(roped, plain, lam, subln, batch, seq, lambda_init):
    tq = DIFF_TQ
    nq = seq // tq
    pair = 2 * HEAD_DIM
    dq_cb = (NSA_HEADS + 2 * NSA_GROUPS) * HEAD_DIM // pair
    dk_cb = dq_cb + DIFF_HEADS
    dv_cb = 2 * NSA_GROUPS * HEAD_DIM // DIFF_V
    return pl.pallas_call(
        functools.partial(_diff_kernel, lambda_init=lambda_init),
        grid=(batch, DIFF_HEADS, nq),
        in_specs=[pl.BlockSpec((tq, pair), lambda b, h, t: (b * nq + t, dq_cb + h)),
                  pl.BlockSpec((seq, pair), lambda b, h, t: (b, dk_cb + h)),
                  pl.BlockSpec((seq, DIFF_V), lambda b, h, t: (b, dv_cb + h)),
                  pl.BlockSpec(lam.shape, lambda b, h, t: (0, 0)),
                  pl.BlockSpec((1, DIFF_V), lambda b, h, t: (0, 0))],
        out_specs=pl.BlockSpec((tq, DIFF_V), lambda b, h, t: (b * nq + t, h)),
        out_shape=jax.ShapeDtypeStruct((batch * seq, DIFF_HEADS * DIFF_V), BF16),
        scratch_shapes=[pltpu.VMEM((2 * tq, DIFF_V), F32)],
        compiler_params=_cparams(("parallel", "parallel", "arbitrary")),
        name="diff_attention",
    )(roped, roped, plain, lam, subln.reshape(1, DIFF_V))


def _rope_tables(pos):
    inv_freq = ROPE_THETA ** (-jnp.arange(ROPE_HALF, dtype=F32) / ROPE_HALF)
    ang = pos.astype(F32)[:, None] * inv_freq[None, :]
    cos, sin = jnp.cos(ang), jnp.sin(ang)
    rest = LANES - 2 * ROPE_HALF
    n = pos.shape[0]
    cosf = jnp.concatenate([cos, cos, jnp.ones((n, rest), F32)], axis=1)
    sinf = jnp.concatenate([-sin, sin, jnp.zeros((n, rest), F32)], axis=1)
    return cosf, sinf


def _router_weights(w_group, b_group, w_expert, b_expert):
    pad = LANES - MOE_GROUPS - MOE_EXPERTS
    wr = jnp.concatenate([w_group, w_expert, jnp.zeros((w_group.shape[0], pad), F32)], axis=1)
    br = jnp.concatenate([b_group, b_expert, jnp.zeros((pad,), F32)]).reshape(1, LANES)
    w_hi = wr.astype(BF16)
    w_lo = (wr - w_hi.astype(F32)).astype(BF16)
    return jnp.concatenate([w_hi, w_lo], axis=1), br


def _even_layer(h, hn, batch, seq, w_in, conv_w, conv_b, dt_bias, a_log, d_skip, gate_norm, sc_w, w_out,
                g_ffn, wr, br):
    i = SSD_INNER
    conv_ch = i + 2 * SSD_GROUPS * SSD_STATE
    o_dt = i + conv_ch
    o_sc = o_dt + SSD_HEADS
    w_main = jnp.concatenate([w_in[:, :o_dt], w_in[:, o_sc:]], axis=1).astype(BF16)
    w_dt = jnp.pad(w_in[:, o_dt:o_sc], ((0, 0), (0, LANES - SSD_HEADS))).astype(BF16)
    proj = _matmul(hn, w_main, F32, 1024, 1024)
    dt_raw = _matmul(hn, w_dt, F32, 512, LANES)
    mix = _even_core(proj, dt_raw, batch, seq, conv_w, conv_b, dt_bias, a_log, d_skip, gate_norm, sc_w)
    return _outproj([mix], [w_out.astype(BF16)], h, g_ffn, wr, br)


def _odd_layer(h, hn, batch, seq, w_in, cmp_pos, cmp_w1, cmp_w2, lam, subln, w_out, lambda_init, g_ffn, wr, br):
    hd, kvw = HEAD_DIM, NSA_GROUPS * HEAD_DIM
    o = [0]
    for wdt in (NSA_HEADS * hd,) + (kvw,) * 6 + (3 * NSA_HEADS,) + (DIFF_HEADS * 2 * hd,) * 2 + (DIFF_HEADS * DIFF_V,):
        o.append(o[-1] + wdt)
    col = lambda k: w_in[:, o[k]:o[k + 1]]
    q, k_cmp, v_cmp, k_sel, v_sel, k_win, v_win, gates, dq, dk, dv = [col(k) for k in range(11)]
    gcols = []
    for g in range(NSA_GROUPS):
        per = 3 * NSA_REP
        gcols.append(jnp.pad(gates[:, g * per:(g + 1) * per], ((0, 0), (0, LANES - per))))
    w_rope = jnp.concatenate([q, k_sel, k_win, dq, dk], axis=1).astype(BF16)
    w_plain = jnp.concatenate([v_sel, v_win, dv], axis=1).astype(BF16)
    w_cmp = jnp.concatenate([k_cmp, v_cmp], axis=1).astype(BF16)
    w_gates = jnp.concatenate(gcols, axis=1).astype(BF16)
    pos = jnp.arange(seq)
    roped = _matmul(hn, w_rope, BF16, 512, 512, rope=_rope_tables(pos), seq=seq)
    plain = _matmul(hn, w_plain, BF16, 512, 512)
    kv_cmp = _matmul(hn, w_cmp, F32, 512, 512)
    gate_lg = _matmul(hn, w_gates, F32, 512, 2 * LANES)
    ncmp = seq // CMP_STRIDE
    cmp_end = jnp.arange(ncmp) * CMP_STRIDE + CMP_CHUNKS * CMP_STRIDE - 1
    cos_c, sin_c = _rope_tables(cmp_end)
    cmp_kv = _compress(kv_cmp, batch, seq, cmp_pos, cmp_w1, cmp_w2, cos_c, sin_c)
    n_idx = jnp.arange(ncmp)[:, None]
    j_idx = jnp.arange(LANES)[None, :]
    per_sel = SEL_BLOCK // CMP_STRIDE
    pool = ((n_idx >= per_sel * j_idx - 1) & (n_idx <= per_sel * j_idx + per_sel - 1)
            & (n_idx < ncmp - 1) & (j_idx < seq // SEL_BLOCK)).astype(F32)
    negexp = jnp.where((jnp.arange(seq)[:, None] // SEL_BLOCK) == jnp.arange(LANES)[None, :], NEG_INF, 0.0).astype(BF16)
    o_nsa = _nsa(roped, plain, cmp_kv, gate_lg, batch, seq, pool, negexp)
    o_diff = _diff_attn(roped, plain, lam, subln, batch, seq, lambda_init)
    n_nsa = NSA_HEADS * hd
    w_o = w_out.astype(BF16)
    return _outproj([o_nsa, o_diff], [w_o[:n_nsa], w_o[n_nsa:]], h, g_ffn, wr, br)


def kernel(x, p, norm_mix, norm_ffn, norm_ple, norm_final, ev_w_in, ev_conv_w, ev_conv_b, ev_dt_bias, ev_a_log, ev_d_skip, ev_gate_norm, ev_sc_w, ev_w_out, od_w_in, od_cmp_pos, od_cmp_w1, od_cmp_w2, od_lambda, od_subln, od_w_out, moe_w_group, moe_b_group, moe_w_expert, moe_b_expert, moe_w_gate, moe_w_up, moe_w_down, ple_gate, ple_proj):
    batch, seq, d = x.shape
    n = batch * seq
    depth = p.shape[0]
    h = x.reshape(n, d)
    hn = _rmsnorm(h, norm_mix[0], BF16)
    for i in range(depth):
        j = i // 2
        wr, br = _router_weights(moe_w_group[i], moe_b_group[i], moe_w_expert[i], moe_b_expert[i])
        if i % 2 == 0:
            h, hn2, logits = _even_layer(h, hn, batch, seq, ev_w_in[j], ev_conv_w[j], ev_conv_b[j], ev_dt_bias[j],
                                         ev_a_log[j], ev_d_skip[j], ev_gate_norm[j], ev_sc_w[j], ev_w_out[j],
                                         norm_ffn[i], wr, br)
        else:
            lambda_init = 0.8 - 0.6 * math.exp(-0.3 * i)
            h, hn2, logits = _odd_layer(h, hn, batch, seq, od_w_in[j], od_cmp_pos[j], od_cmp_w1[j], od_cmp_w2[j],
                                        od_lambda[j], od_subln[j], od_w_out[j], lambda_init, norm_ffn[i], wr, br)
        y1, y2, routed = _moe(hn2, logits, i, moe_w_gate, moe_w_up, moe_w_down)
        last = i == depth - 1
        g_next = norm_final if last else norm_mix[i + 1]
        h, hn = _ple(h, y1, y2, routed, p.reshape(depth * n, PLE_DIM), i, norm_ple[i], ple_gate[i].astype(BF16),
                     ple_proj[i].astype(BF16), g_next, F32 if last else BF16)
    return hn.reshape(batch, seq, d)
```

```python
import functools
import math

import jax
import jax.numpy as jnp
from jax import lax
from jax.experimental import pallas as pl
from jax.experimental.pallas import tpu as pltpu

F32 = jnp.float32
BF16 = jnp.bfloat16
HIGHEST = lax.Precision.HIGHEST

LANES = 128
D_MODEL = 2048
NORM_EPS = 1e-6
ROPE_THETA = 500000.0
ROPE_HALF = 16
NEG_INF = -1e30

SSD_HEADS = 32
SSD_HEAD_DIM = 64
SSD_GROUPS = 8
SSD_STATE = 128
SSD_CHUNK = 128
SSD_INNER = 2048
CONV_HALO = 8

HEAD_DIM = 128
NSA_HEADS = 8
NSA_GROUPS = 2
NSA_REP = 4
CMP_STRIDE = 16
CMP_CHUNKS = 2
SEL_BLOCK = 64
SEL_TOPK = 16
SEL_FORCE = 1e4
WINDOW = 512
DIFF_HEADS = 4
DIFF_V = 256

MOE_GROUPS = 4
MOE_PER_GROUP = 8
MOE_EXPERTS = 32
MOE_FF = 1024
MOE_ROWS = 256
ROUTER_TILE = 512
PLE_DIM = 256

VMEM_LIMIT = 56 * 1024 * 1024


def _cparams(sem, vmem=VMEM_LIMIT):
    return pltpu.CompilerParams(dimension_semantics=sem, vmem_limit_bytes=vmem)


def _nt_dot(a, b):
    return lax.dot_general(a, b, (((1,), (1,)), ((), ())), preferred_element_type=F32)


def _silu(x):
    return x * jax.nn.sigmoid(x)


def _rmsnorm_kernel(x_ref, g_ref, o_ref):
    x = x_ref[...]
    ms = jnp.mean(x * x, axis=-1, keepdims=True)
    o_ref[...] = (x * lax.rsqrt(ms + NORM_EPS) * g_ref[...]).astype(o_ref.dtype)


def _rmsnorm(x, g, out_dtype, tm=512):
    n, d = x.shape
    return pl.pallas_call(
        _rmsnorm_kernel,
        grid=(n // tm,),
        in_specs=[pl.BlockSpec((tm, d), lambda i: (i, 0)), pl.BlockSpec((1, d), lambda i: (0, 0))],
        out_specs=pl.BlockSpec((tm, d), lambda i: (i, 0)),
        out_shape=jax.ShapeDtypeStruct((n, d), out_dtype),
        compiler_params=_cparams(("parallel",)),
        name="rmsnorm",
    )(x, g.reshape(1, d))


def _rope_apply(x, cosf, sinf, lane):
    swapped = jnp.where(lane < ROPE_HALF, pltpu.roll(x, LANES - ROPE_HALF, 1), pltpu.roll(x, ROPE_HALF, 1))
    return x * cosf + swapped * sinf


def _mm_kernel(a_ref, w_ref, o_ref):
    o_ref[...] = jnp.dot(a_ref[...], w_ref[...], preferred_element_type=F32).astype(o_ref.dtype)


def _mm_rope_kernel(a_ref, w_ref, cos_ref, sin_ref, o_ref):
    acc = jnp.dot(a_ref[...], w_ref[...], preferred_element_type=F32)
    tm, tn = acc.shape
    cosf = cos_ref[...]
    sinf = sin_ref[...]
    lane = lax.broadcasted_iota(jnp.int32, (tm, LANES), 1)
    for h in range(tn // LANES):
        sl = slice(h * LANES, (h + 1) * LANES)
        o_ref[:, sl] = _rope_apply(acc[:, sl], cosf, sinf, lane).astype(o_ref.dtype)


def _matmul(a, w, out_dtype, tm, tn, rope=None, seq=None):
    m, k = a.shape
    n = w.shape[1]
    in_specs = [pl.BlockSpec((tm, k), lambda i, j: (i, 0)), pl.BlockSpec((k, tn), lambda i, j: (0, j))]
    args = [a, w]
    body = _mm_kernel
    if rope is not None:
        per_seq = seq // tm
        in_specs += [pl.BlockSpec((tm, LANES), lambda i, j: (i % per_seq, 0))] * 2
        args += list(rope)
        body = _mm_rope_kernel
    return pl.pallas_call(
        body,
        grid=(m // tm, n // tn),
        in_specs=in_specs,
        out_specs=pl.BlockSpec((tm, tn), lambda i, j: (i, j)),
        out_shape=jax.ShapeDtypeStruct((m, n), out_dtype),
        compiler_params=_cparams(("parallel", "parallel")),
        name="proj_rope" if rope is not None else "proj",
    )(*args)


def _even_core_kernel(z_ref, xs_ref, bc_ref, scb_ref, scc_ref, sch_ref,
                      xs_h_ref, bc_h_ref, scc_h_ref, sch_h_ref, dt_ref,
                      cwx_ref, cwbc_ref, cbx_ref, cbbc_ref, dtb_ref, alog_ref, dskip_ref, gnorm_ref, scw_ref,
                      o_ref, state_ref, cbuf_ref, xsc_ref, bcc_ref, y_ref):
    q = SSD_CHUNK
    first = pl.program_id(1) == 0
    keep = jnp.where(first, 0.0, 1.0).astype(F32)

    @pl.when(first)
    def _():
        state_ref[...] = jnp.zeros_like(state_ref)

    strip = 512

    def causal_conv(load_main, load_halo, w_ref, width, finish):
        cbuf_ref[0:CONV_HALO, :] = load_halo() * keep
        cbuf_ref[CONV_HALO:CONV_HALO + q, :] = load_main()
        for c0 in range(0, SSD_INNER, strip):
            cs = slice(c0, c0 + strip)
            acc = None
            for k in range(width):
                r0 = CONV_HALO - (width - 1) + k
                term = w_ref[k:k + 1, cs] * cbuf_ref[r0:r0 + q, cs]
                acc = term if acc is None else acc + term
            finish(cs, acc)

    def fin_xs(cs, acc):
        xsc_ref[:, cs] = _silu(acc + cbx_ref[:, cs])

    def fin_bc(cs, acc):
        bcc_ref[:, cs] = _silu(acc + cbbc_ref[:, cs])

    causal_conv(lambda: xs_ref[...], lambda: xs_h_ref[...], cwx_ref, 4, fin_xs)
    causal_conv(lambda: bc_ref[...], lambda: bc_h_ref[...], cwbc_ref, 4, fin_bc)

    def fin_sc(cs, acc):
        o_ref[:, SSD_INNER + cs.start:SSD_INNER + cs.stop] = (scb_ref[:, cs] * acc).astype(o_ref.dtype)

    causal_conv(lambda: scc_ref[...] * sch_ref[...], lambda: scc_h_ref[...] * sch_h_ref[...], scw_ref, 3, fin_sc)

    dt = jax.nn.softplus(dt_ref[...] + dtb_ref[...])
    d_a = dt * (-jnp.exp(alog_ref[...]))
    row = lax.broadcasted_iota(jnp.int32, (q, q), 0)
    col = lax.broadcasted_iota(jnp.int32, (q, q), 1)
    causal = row >= col
    tril = jnp.where(causal, 1.0, 0.0).astype(F32)
    a_cum = jnp.dot(tril, d_a, preferred_element_type=F32, precision=HIGHEST)
    a_last = a_cum[q - 1:q, :]
    decay_end = jnp.exp(a_last - a_cum)
    chunk_decay = jnp.exp(a_last)
    exp_acum = jnp.exp(a_cum)
    a_cum_t = a_cum.T
    lane = lax.broadcasted_iota(jnp.int32, (q, LANES), 1)
    lo = lane < SSD_HEAD_DIM
    lane_row = lax.broadcasted_iota(jnp.int32, (1, LANES), 1)
    lo_row = lane_row < SSD_HEAD_DIM

    def pair_cols(mat, h0):
        return jnp.where(lo, mat[:, h0:h0 + 1], mat[:, h0 + 1:h0 + 2])

    for g in range(SSD_GROUPS):
        gs = slice(g * SSD_STATE, (g + 1) * SSD_STATE)
        b_g = bcc_ref[:, gs]
        c_g = bcc_ref[:, SSD_GROUPS * SSD_STATE + g * SSD_STATE:SSD_GROUPS * SSD_STATE + (g + 1) * SSD_STATE]
        b_gt = b_g.T.astype(BF16)
        c_gb = c_g.astype(BF16)
        cb = jnp.dot(c_gb, b_gt, preferred_element_type=F32)
        for hp in range(2):
            pr = g * 2 + hp
            h0 = 2 * pr
            ps = slice(pr * LANES, (pr + 1) * LANES)
            xp = xsc_ref[:, ps]
            xdt = xp * pair_cols(dt, h0)
            mats = []
            for h in (h0, h0 + 1):
                seg = a_cum[:, h:h + 1] - a_cum_t[h:h + 1, :]
                dec = jnp.where(causal, jnp.exp(jnp.minimum(seg, 0.0)), 0.0)
                mats.append((cb * dec).astype(BF16))
            lhs = jnp.concatenate(mats, axis=1)
            rhs = jnp.concatenate([jnp.where(lo, xdt, 0.0), jnp.where(lo, 0.0, xdt)], axis=0).astype(BF16)
            y = jnp.dot(lhs, rhs, preferred_element_type=F32)
            st = state_ref[pr]
            y = y + jnp.dot(c_gb, st.astype(BF16), preferred_element_type=F32) * pair_cols(exp_acum, h0)
            xw = (xdt * pair_cols(decay_end, h0)).astype(BF16)
            cd = jnp.where(lo_row, chunk_decay[:, h0:h0 + 1], chunk_decay[:, h0 + 1:h0 + 2])
            state_ref[pr] = st * cd + jnp.dot(b_gt, xw, preferred_element_type=F32)
            y_ref[:, ps] = y + dskip_ref[:, ps] * xp

    gw = SSD_INNER // SSD_GROUPS
    for g in range(SSD_GROUPS):
        cs = slice(g * gw, (g + 1) * gw)
        yg = y_ref[:, cs] * _silu(z_ref[:, cs])
        ms = jnp.mean(yg * yg, axis=-1, keepdims=True)
        o_ref[:, cs] = (yg * lax.rsqrt(ms + NORM_EPS) * gnorm_ref[:, cs]).astype(o_ref.dtype)


def _even_core(proj, dt_raw, batch, seq, conv_w, conv_b, dt_bias, a_log, d_skip, gate_norm, sc_w):
    q = SSD_CHUNK
    nc = seq // q
    w = SSD_INNER
    hb = q // CONV_HALO

    def main(cb):
        return pl.BlockSpec((q, w), lambda b, c: (b * nc + c, cb))

    def halo(cb):
        return pl.BlockSpec((CONV_HALO, w), lambda b, c: (jnp.maximum((b * nc + c) * hb - 1, 0), cb))

    def full(shape):
        return pl.BlockSpec(shape, lambda b, c: (0, 0))

    pad = LANES - SSD_HEADS
    dtb = jnp.pad(dt_bias, (0, pad)).reshape(1, LANES)
    alog = jnp.pad(a_log, (0, pad)).reshape(1, LANES)
    dskip = jnp.repeat(d_skip, SSD_HEAD_DIM).reshape(1, w)
    in_specs = [main(0), main(1), main(2), main(3), main(4), main(5),
                halo(1), halo(2), halo(4), halo(5),
                pl.BlockSpec((q, LANES), lambda b, c: (b * nc + c, 0)),
                full((4, w)), full((4, w)), full((1, w)), full((1, w)),
                full((1, LANES)), full((1, LANES)), full((1, w)), full((1, w)), full((3, w))]
    return pl.pallas_call(
        _even_core_kernel,
        grid=(batch, nc),
        in_specs=in_specs,
        out_specs=pl.BlockSpec((q, 2 * w), lambda b, c: (b * nc + c, 0)),
        out_shape=jax.ShapeDtypeStruct((batch * seq, 2 * w), BF16),
        scratch_shapes=[pltpu.VMEM((SSD_HEADS // 2, SSD_STATE, LANES), F32),
                        pltpu.VMEM((CONV_HALO + q, w), F32),
                        pltpu.VMEM((q, w), F32), pltpu.VMEM((q, w), F32), pltpu.VMEM((q, w), F32)],
        compiler_params=_cparams(("parallel", "arbitrary")),
        name="ssd_conv_core",
    )(proj, proj, proj, proj, proj, proj, proj, proj, proj, proj, dt_raw,
      conv_w[:, :w], conv_w[:, w:], conv_b[:w].reshape(1, w), conv_b[w:].reshape(1, w),
      dtb, alog, dskip, gate_norm.reshape(1, w), sc_w)


def _outproj_kernel(*refs, n_in):
    mix = refs[:n_in]
    ws = refs[n_in:2 * n_in]
    h_ref, g_ref, wr_ref, br_ref, hnew_ref, hn_ref, lg_ref = refs[2 * n_in:]
    acc = h_ref[...]
    for m_ref, w_ref in zip(mix, ws):
        acc = acc + jnp.dot(m_ref[...], w_ref[...], preferred_element_type=F32)
    hnew_ref[...] = acc
    ms = jnp.mean(acc * acc, axis=-1, keepdims=True)
    hn = acc * lax.rsqrt(ms + NORM_EPS) * g_ref[...]
    hn_ref[...] = hn
    hi = hn.astype(BF16)
    lo = (hn - hi.astype(F32)).astype(BF16)
    wr = wr_ref[...]
    t = jnp.dot(hi, wr, preferred_element_type=F32)
    u = jnp.dot(lo, wr[:, :LANES], preferred_element_type=F32)
    lg_ref[...] = t[:, :LANES] + t[:, LANES:] + u + br_ref[...]


def _outproj(mixes, ws, h, g, wr, br, tm=256):
    n, d = h.shape
    n_in = len(mixes)
    in_specs = ([pl.BlockSpec((tm, m.shape[1]), lambda i: (i, 0)) for m in mixes]
                + [pl.BlockSpec(w.shape, lambda i: (0, 0)) for w in ws]
                + [pl.BlockSpec((tm, d), lambda i: (i, 0)), pl.BlockSpec((1, d), lambda i: (0, 0)),
                   pl.BlockSpec((d, 2 * LANES), lambda i: (0, 0)), pl.BlockSpec((1, LANES), lambda i: (0, 0))])
    return pl.pallas_call(
        functools.partial(_outproj_kernel, n_in=n_in),
        grid=(n // tm,),
        in_specs=in_specs,
        out_specs=[pl.BlockSpec((tm, d), lambda i: (i, 0)), pl.BlockSpec((tm, d), lambda i: (i, 0)),
                   pl.BlockSpec((tm, LANES), lambda i: (i, 0))],
        out_shape=[jax.ShapeDtypeStruct((n, d), F32), jax.ShapeDtypeStruct((n, d), F32),
                   jax.ShapeDtypeStruct((n, LANES), F32)],
        compiler_params=_cparams(("parallel",)),
        name="outproj_norm_router",
    )(*mixes, *ws, h, g.reshape(1, d), wr, br)


ROUTE_E1, ROUTE_E2, ROUTE_W1, ROUTE_W2, ROUTE_R1, ROUTE_R2 = range(6)


def _router_kernel(lg_ref, out_ref, cnt_ref, run_ref):
    @pl.when(pl.program_id(0) == 0)
    def _():
        run_ref[...] = jnp.zeros_like(run_ref)

    x = lg_ref[...]
    t = x.shape[0]
    lane = lax.broadcasted_iota(jnp.int32, (t, LANES), 1).astype(F32)
    far = float(LANES)
    gmask = lane < MOE_GROUPS
    gl = jnp.where(gmask, x, NEG_INF)
    gmax = jnp.max(gl, axis=-1, keepdims=True)
    g_top = jnp.min(jnp.where(gl == gmax, lane, far), axis=-1, keepdims=True)
    gsum = jnp.sum(jnp.where(gmask, jnp.exp(gl - gmax), 0.0), axis=-1, keepdims=True)
    g_prob = 1.0 / gsum
    lo = MOE_GROUPS + MOE_PER_GROUP * g_top
    emask = (lane >= lo) & (lane < lo + MOE_PER_GROUP)
    el = jnp.where(emask, x, NEG_INF)
    emax = jnp.max(el, axis=-1, keepdims=True)
    ee = jnp.where(emask, jnp.exp(el - emax), 0.0)
    ep = jnp.where(emask, ee / jnp.sum(ee, axis=-1, keepdims=True), -1.0)
    p1 = jnp.max(ep, axis=-1, keepdims=True)
    i1 = jnp.min(jnp.where(ep == p1, lane, far), axis=-1, keepdims=True)
    ep2 = jnp.where(lane == i1, -1.0, ep)
    p2 = jnp.max(ep2, axis=-1, keepdims=True)
    i2 = jnp.min(jnp.where(ep2 == p2, lane, far), axis=-1, keepdims=True)
    den = p1 + p2
    w1 = g_prob * p1 / den
    w2 = g_prob * p2 / den
    oh1 = jnp.where(lane == i1, 1.0, 0.0)
    oh2 = jnp.where(lane == i2, 1.0, 0.0)
    oh = oh1 + oh2
    row = lax.broadcasted_iota(jnp.int32, (t, t), 0)
    col = lax.broadcasted_iota(jnp.int32, (t, t), 1)
    strict = jnp.where(row > col, 1.0, 0.0).astype(BF16)
    before = jnp.dot(strict, oh.astype(BF16), preferred_element_type=F32) + run_ref[...]
    r1 = jnp.sum(oh1 * before, axis=-1, keepdims=True)
    r2 = jnp.sum(oh2 * before, axis=-1, keepdims=True)
    run_ref[...] = run_ref[...] + jnp.sum(oh, axis=0, keepdims=True)
    cnt_ref[...] = run_ref[...]
    packed = jnp.zeros((t, LANES), F32)
    for k, v in ((ROUTE_E1, i1 - MOE_GROUPS), (ROUTE_E2, i2 - MOE_GROUPS), (ROUTE_W1, w1), (ROUTE_W2, w2),
                 (ROUTE_R1, r1), (ROUTE_R2, r2)):
        packed = jnp.where(lane == k, v, packed)
    out_ref[...] = packed


def _router(logits, tt=ROUTER_TILE):
    n = logits.shape[0]
    return pl.pallas_call(
        _router_kernel,
        grid=(n // tt,),
        in_specs=[pl.BlockSpec((tt, LANES), lambda i: (i, 0))],
        out_specs=[pl.BlockSpec((tt, LANES), lambda i: (i, 0)), pl.BlockSpec((1, LANES), lambda i: (0, 0))],
        out_shape=[jax.ShapeDtypeStruct((n, LANES), F32), jax.ShapeDtypeStruct((1, LANES), F32)],
        scratch_shapes=[pltpu.VMEM((1, LANES), F32)],
        compiler_params=_cparams(("arbitrary",)),
        name="router_rank",
    )(logits)


SLOT_1, SLOT_2 = 0, 1
BLK_HALVES = 2


def _slot_kernel(rt_ref, cnt_ref, slot_ref, blk_ref):
    lane_row = lax.broadcasted_iota(jnp.int32, (1, LANES), 1)
    is_exp = (lane_row >= MOE_GROUPS) & (lane_row < MOE_GROUPS + MOE_EXPERTS)
    padded = jnp.where(is_exp, jnp.floor((cnt_ref[...] + (MOE_ROWS - 1)) * (1.0 / MOE_ROWS)) * MOE_ROWS, 0.0)
    r = lax.broadcasted_iota(jnp.int32, (LANES, LANES), 0)
    c = lax.broadcasted_iota(jnp.int32, (LANES, LANES), 1)
    incl = jnp.where(r <= c, 1.0, 0.0).astype(F32)
    pad_end = jnp.dot(jnp.broadcast_to(padded, (8, LANES)), incl, preferred_element_type=F32,
                      precision=HIGHEST)[0:1]
    pad_start = pad_end - padded
    rt = rt_ref[...]
    lane = lax.broadcasted_iota(jnp.int32, rt.shape, 1).astype(F32)

    def slot(e, rank):
        return jnp.sum(jnp.where(lane == e + MOE_GROUPS, pad_start, 0.0), axis=-1, keepdims=True) + rank

    s1 = slot(rt[:, ROUTE_E1:ROUTE_E1 + 1], rt[:, ROUTE_R1:ROUTE_R1 + 1])
    s2 = slot(rt[:, ROUTE_E2:ROUTE_E2 + 1], rt[:, ROUTE_R2:ROUTE_R2 + 1])
    slot_ref[...] = jnp.where(lane == SLOT_1, s1, jnp.where(lane == SLOT_2, s2, 0.0))

    end_col = jnp.broadcast_to(pad_end, (LANES, LANES)).T
    exp_col = (r >= MOE_GROUPS) & (r < MOE_GROUPS + MOE_EXPERTS)
    last = MOE_GROUPS + MOE_EXPERTS - 1
    total = pad_end[:, last:last + 1]

    def experts_done(start):
        return jnp.sum(jnp.where(exp_col & (end_col <= start), 1.0, 0.0), axis=0, keepdims=True)

    idle = experts_done(total - MOE_ROWS)
    rows = []
    for h in range(BLK_HALVES):
        start = (lane_row + h * LANES).astype(F32) * MOE_ROWS
        valid = start < total
        rows += [jnp.where(valid, experts_done(start), idle), jnp.where(valid, 1.0, 0.0)]
    blk_ref[...] = jnp.concatenate(rows + [jnp.zeros((8 - 2 * BLK_HALVES, LANES), F32)], axis=0)


def _slots(routed, counts, tt=ROUTER_TILE):
    n = routed.shape[0]
    return pl.pallas_call(
        _slot_kernel,
        grid=(n // tt,),
        in_specs=[pl.BlockSpec((tt, LANES), lambda i: (i, 0)), pl.BlockSpec((1, LANES), lambda i: (0, 0))],
        out_specs=[pl.BlockSpec((tt, LANES), lambda i: (i, 0)), pl.BlockSpec((8, LANES), lambda i: (0, 0))],
        out_shape=[jax.ShapeDtypeStruct((n, LANES), F32), jax.ShapeDtypeStruct((8, LANES), F32)],
        compiler_params=_cparams(("arbitrary",)),
        name="moe_slots",
    )(routed, counts)


def _moe_kernel(be_ref, bv_ref, x_ref, wg_ref, wu_ref, wd_ref, o_ref):
    i = pl.program_id(0)

    @pl.when(bv_ref[i] > 0)
    def _():
        x = x_ref[...].astype(BF16)
        step = 256
        for n0 in range(0, MOE_FF, step):
            gate = jnp.dot(x, wg_ref[:, n0:n0 + step].astype(BF16), preferred_element_type=F32)
            up = jnp.dot(x, wu_ref[:, n0:n0 + step].astype(BF16), preferred_element_type=F32)
            hid = (_silu(gate) * up).astype(BF16)
            part = jnp.dot(hid, wd_ref[n0:n0 + step, :].astype(BF16), preferred_element_type=F32)
            if n0 == 0:
                o_ref[...] = part
            else:
                o_ref[...] += part

    @pl.when(bv_ref[i] == 0)
    def _():
        o_ref[...] = jnp.zeros_like(o_ref)


def _moe_ffn(xs, blk_expert, blk_valid, layer, wg, wu, wd):
    cap, d = xs.shape
    nb = cap // MOE_ROWS
    ff = wg.shape[3]
    grid_spec = pltpu.PrefetchScalarGridSpec(
        num_scalar_prefetch=2,
        grid=(nb,),
        in_specs=[pl.BlockSpec((MOE_ROWS, d), lambda i, be, bv: (i, 0)),
                  pl.BlockSpec((None, None, d, ff), lambda i, be, bv: (layer, be[i], 0, 0)),
                  pl.BlockSpec((None, None, d, ff), lambda i, be, bv: (layer, be[i], 0, 0)),
                  pl.BlockSpec((None, None, ff, d), lambda i, be, bv: (layer, be[i], 0, 0),
                               pipeline_mode=pl.Buffered(1))],
        out_specs=pl.BlockSpec((MOE_ROWS, d), lambda i, be, bv: (i, 0)),
    )
    return pl.pallas_call(
        _moe_kernel,
        grid_spec=grid_spec,
        out_shape=jax.ShapeDtypeStruct((cap, d), F32),
        compiler_params=_cparams(("arbitrary",)),
        name="moe_ffn",
    )(blk_expert, blk_valid, xs, wg, wu, wd)


DISPATCH_TILE = 256
ROW_DMA_UNROLL = 8


def _row_copy(src_ref, src_row, dst_ref, dst_row, sem):
    return pltpu.make_async_copy(src_ref.at[pl.ds(src_row, 1), :], dst_ref.at[pl.ds(dst_row, 1), :], sem)


def _dispatch_kernel(s1_ref, s2_ref, hn_ref, xs_in_ref, xs_ref, sem):
    del xs_in_ref
    t = hn_ref.shape[0]
    base = pl.program_id(0) * t

    def issue(r, carry):
        _row_copy(hn_ref, r, xs_ref, s1_ref[base + r], sem.at[0]).start()
        _row_copy(hn_ref, r, xs_ref, s2_ref[base + r], sem.at[1]).start()
        return carry

    lax.fori_loop(0, t, issue, 0, unroll=ROW_DMA_UNROLL)

    def drain(r, carry):
        _row_copy(hn_ref, 0, xs_ref, 0, sem.at[0]).wait()
        _row_copy(hn_ref, 0, xs_ref, 0, sem.at[1]).wait()
        return carry

    lax.fori_loop(0, t, drain, 0, unroll=ROW_DMA_UNROLL)


def _dispatch(hn, slot1, slot2, cap, tt=DISPATCH_TILE):
    n, d = hn.shape
    grid_spec = pltpu.PrefetchScalarGridSpec(
        num_scalar_prefetch=2,
        grid=(n // tt,),
        in_specs=[pl.BlockSpec((tt, d), lambda i, s1, s2: (i, 0)), pl.BlockSpec(memory_space=pl.ANY)],
        out_specs=pl.BlockSpec(memory_space=pl.ANY),
        scratch_shapes=[pltpu.SemaphoreType.DMA((2,))],
    )
    return pl.pallas_call(
        _dispatch_kernel,
        grid_spec=grid_spec,
        out_shape=jax.ShapeDtypeStruct((cap, d), F32),
        input_output_aliases={3: 0},
        compiler_params=_cparams(("arbitrary",)),
        name="moe_dispatch",
    )(slot1, slot2, hn, jnp.zeros((cap, d), F32))


def _moe(hn, logits, layer, wg, wu, wd):
    n, d = hn.shape
    routed, counts = _router(logits)
    slots, blk = _slots(routed, counts)
    slot1 = slots[:, SLOT_1].astype(jnp.int32)
    slot2 = slots[:, SLOT_2].astype(jnp.int32)
    nb = -(-(2 * n + MOE_EXPERTS * (MOE_ROWS - 1)) // MOE_ROWS)
    assert nb <= BLK_HALVES * LANES
    cap = nb * MOE_ROWS
    blk_expert = blk[0:2 * BLK_HALVES:2].reshape(-1)[:nb].astype(jnp.int32)
    blk_valid = blk[1:2 * BLK_HALVES:2].reshape(-1)[:nb].astype(jnp.int32)
    xs = _dispatch(hn, slot1, slot2, cap)
    yb = _moe_ffn(xs, blk_expert, blk_valid, layer, wg, wu, wd)
    return yb, slot1, slot2, routed


def _ple_kernel(s1_ref, s2_ref, h_ref, yb_ref, rt_ref, p_ref, gp_ref, wg_ref, wp_ref, gn_ref, hout_ref, nout_ref,
                ybuf_ref, sem):
    tm = h_ref.shape[0]
    i = pl.program_id(0)
    cur = i % 2

    def gather(step, buf, start):
        base = step * tm

        def one(r, carry):
            for k, s_ref in enumerate((s1_ref, s2_ref)):
                row = s_ref[base + r] if start else 0
                cp = _row_copy(yb_ref, row, ybuf_ref.at[buf, k], r if start else 0, sem.at[buf, k])
                if start:
                    cp.start()
                else:
                    cp.wait()
            return carry

        lax.fori_loop(0, tm, one, 0, unroll=ROW_DMA_UNROLL)

    @pl.when(i == 0)
    def _():
        gather(0, 0, True)

    @pl.when(i + 1 < pl.num_programs(0))
    def _():
        gather(i + 1, 1 - cur, True)

    gather(i, cur, False)
    rt = rt_ref[...]
    moe = ybuf_ref[cur, 0] * rt[:, ROUTE_W1:ROUTE_W1 + 1] + ybuf_ref[cur, 1] * rt[:, ROUTE_W2:ROUTE_W2 + 1]
    h = h_ref[...] + moe
    ms = jnp.mean(h * h, axis=-1, keepdims=True)
    hn = (h * lax.rsqrt(ms + NORM_EPS) * gp_ref[...]).astype(BF16)
    gate = jax.nn.sigmoid(jnp.dot(hn, wg_ref[...], preferred_element_type=F32))
    emb = jnp.dot(p_ref[...].astype(BF16), wp_ref[...], preferred_element_type=F32)
    h = h + gate * emb
    hout_ref[...] = h
    ms = jnp.mean(h * h, axis=-1, keepdims=True)
    nout_ref[...] = (h * lax.rsqrt(ms + NORM_EPS) * gn_ref[...]).astype(nout_ref.dtype)


def _ple(h, yb, slot1, slot2, routed, p, layer, g_ple, w_gate, w_proj, g_next, next_dtype, tm=256):
    n, d = h.shape
    row = lambda i, s1, s2: (i, 0)
    fixed = lambda i, s1, s2: (0, 0)
    grid_spec = pltpu.PrefetchScalarGridSpec(
        num_scalar_prefetch=2,
        grid=(n // tm,),
        in_specs=[pl.BlockSpec((tm, d), row), pl.BlockSpec(memory_space=pl.ANY),
                  pl.BlockSpec((tm, LANES), row),
                  pl.BlockSpec((tm, PLE_DIM), lambda i, s1, s2: (layer * (n // tm) + i, 0)),
                  pl.BlockSpec((1, d), fixed), pl.BlockSpec((d, d), fixed), pl.BlockSpec((PLE_DIM, d), fixed),
                  pl.BlockSpec((1, d), fixed)],
        out_specs=[pl.BlockSpec((tm, d), row), pl.BlockSpec((tm, d), row)],
        scratch_shapes=[pltpu.VMEM((2, 2, tm, d), F32), pltpu.SemaphoreType.DMA((2, 2))],
    )
    return pl.pallas_call(
        _ple_kernel,
        grid_spec=grid_spec,
        out_shape=[jax.ShapeDtypeStruct((n, d), F32), jax.ShapeDtypeStruct((n, d), next_dtype)],
        compiler_params=_cparams(("arbitrary",)),
        name="ple_norm",
    )(slot1, slot2, h, yb, routed, p, g_ple.reshape(1, d), w_gate, w_proj, g_next.reshape(1, d))


def _compress_kernel(x_ref, pos_ref, w1_ref, w2_ref, cos_ref, sin_ref, o_ref):
    nchunk = x_ref.shape[0]
    width = 2 * NSA_GROUPS * HEAD_DIM
    lane = lax.broadcasted_iota(jnp.int32, (nchunk, LANES), 1)
    for kv in range(2):
        for g in range(NSA_GROUPS):
            acc = [jnp.zeros((nchunk, HEAD_DIM), F32) for _ in range(CMP_CHUNKS)]
            for half in range(CMP_CHUNKS):
                for j in range(CMP_STRIDE):
                    c0 = j * width + kv * NSA_GROUPS * HEAD_DIM + g * HEAD_DIM
                    jj = half * CMP_STRIDE + j
                    blk = (x_ref[:, c0:c0 + HEAD_DIM] + pos_ref[kv, jj:jj + 1, :]).astype(BF16)
                    acc[half] = acc[half] + jnp.dot(blk, w1_ref[kv, jj * HEAD_DIM:(jj + 1) * HEAD_DIM, :],
                                                    preferred_element_type=F32)
            pre = acc[0] + pltpu.roll(acc[1], nchunk - 1, 0)
            out = jnp.dot(_silu(pre).astype(BF16), w2_ref[kv], preferred_element_type=F32)
            if kv == 0:
                out = _rope_apply(out, cos_ref[...], sin_ref[...], lane)
            o_ref[kv * NSA_GROUPS + g] = out.astype(o_ref.dtype)


def _compress(kv_cmp, batch, seq, cmp_pos, cmp_w1, cmp_w2, cos_c, sin_c):
    nchunk = seq // CMP_STRIDE
    width = kv_cmp.shape[1]
    x = kv_cmp.reshape(batch * nchunk, CMP_STRIDE * width)
    return pl.pallas_call(
        _compress_kernel,
        grid=(batch,),
        in_specs=[pl.BlockSpec((nchunk, CMP_STRIDE * width), lambda b: (b, 0)),
                  pl.BlockSpec(cmp_pos.shape, lambda b: (0, 0, 0)),
                  pl.BlockSpec(cmp_w1.shape, lambda b: (0, 0, 0)),
                  pl.BlockSpec(cmp_w2.shape, lambda b: (0, 0, 0)),
                  pl.BlockSpec((nchunk, LANES), lambda b: (0, 0)),
                  pl.BlockSpec((nchunk, LANES), lambda b: (0, 0))],
        out_specs=pl.BlockSpec((None, 2 * NSA_GROUPS, nchunk, HEAD_DIM), lambda b: (b, 0, 0, 0)),
        out_shape=jax.ShapeDtypeStruct((batch, 2 * NSA_GROUPS, nchunk, HEAD_DIM), BF16),
        compiler_params=_cparams(("parallel",)),
        name="nsa_compress",
    )(x, cmp_pos, cmp_w1.astype(BF16), cmp_w2.astype(BF16), cos_c, sin_c)


NSA_TQ = 128
NSA_TK = 512
EXP2_SCALE = HEAD_DIM ** -0.5 * math.log2(math.e)


def _softmax_tile(s, m_old, l_old):
    m_new = jnp.maximum(m_old, jnp.max(s, axis=-1, keepdims=True))
    alpha = jnp.exp2(EXP2_SCALE * (m_old - m_new))
    p = jnp.exp2(EXP2_SCALE * (s - jnp.concatenate([m_new] * (s.shape[1] // LANES), axis=1)))
    l_new = alpha * l_old + jnp.sum(p, axis=-1, keepdims=True)
    return m_new, l_new, alpha, p


def _nsa_kernel(q_ref, kc_ref, vc_ref, ksel_ref, vsel_ref, kwin_ref, vwin_ref, gate_ref, pool_ref, negexp_ref,
                o_ref, acc_ref):
    tq = NSA_TQ
    rep = NSA_REP
    q0 = pl.program_id(2) * tq
    q = q_ref[...]
    qs = jnp.concatenate([q[:, r * HEAD_DIM:(r + 1) * HEAD_DIM] for r in range(rep)], axis=0)
    tpos = q0 + lax.broadcasted_iota(jnp.int32, (tq, 1), 0)
    heads = [slice(r * tq, (r + 1) * tq) for r in range(rep)]

    ncmp = kc_ref.shape[0]
    cend = lax.broadcasted_iota(jnp.int32, (1, ncmp), 1) * CMP_STRIDE + (CMP_CHUNKS * CMP_STRIDE - 1)
    cvalid = cend <= tpos
    cbias = jnp.where(cvalid, 0.0, NEG_INF).astype(F32)
    s_all = _nt_dot(qs, kc_ref[...])
    probs = []
    for hs in heads:
        s = s_all[hs] + cbias
        m = jnp.max(s, axis=-1, keepdims=True)
        p = jnp.where(cvalid, jnp.exp2(EXP2_SCALE * (s - m)), 0.0)
        l = jnp.sum(p, axis=-1, keepdims=True)
        probs.append(p * jnp.where(l > 0.0, 1.0 / l, 0.0))
    o_cmp = jnp.dot(jnp.concatenate(probs, axis=0).astype(BF16), vc_ref[...], preferred_element_type=F32)
    imp = probs[0]
    for p in probs[1:]:
        imp = imp + p
    p_sel = jnp.dot(imp, pool_ref[...], preferred_element_type=F32, precision=HIGHEST)

    nsel = ksel_ref.shape[0] // SEL_BLOCK
    blk = lax.broadcasted_iota(jnp.int32, (tq, LANES), 1)
    cur = tpos // SEL_BLOCK
    forced = (blk == 0) | (blk == cur) | (blk == cur - 1)
    valid = blk * SEL_BLOCK <= tpos
    score = jnp.where(forced, SEL_FORCE, jnp.where(valid, p_sel, NEG_INF))
    score = jnp.where(blk < nsel, score, 3.0 * NEG_INF)
    st = score.T
    sub = 8
    chunks = [st[c * sub:(c + 1) * sub, :] for c in range(nsel // sub)]
    cnt = [jnp.zeros((sub, tq), F32) for _ in chunks]
    jrow = lax.broadcasted_iota(jnp.int32, (sub, tq), 0)
    for i in range(nsel):
        ci, ii = divmod(i, sub)
        si = chunks[ci][ii:ii + 1, :]
        for c in range(len(chunks)):
            if c > ci:
                beats = si >= chunks[c]
            elif c < ci:
                beats = si > chunks[c]
            else:
                beats = jnp.where(jrow > ii, jnp.where(si >= chunks[c], 1.0, 0.0), jnp.where(si > chunks[c], 1.0, 0.0)) > 0.5
            cnt[c] = cnt[c] + jnp.where(beats, 1.0, 0.0)
    tpos_row = q0 + lax.broadcasted_iota(jnp.int32, (sub, tq), 1)
    dropped = []
    for c, cn in enumerate(cnt):
        keep_blk = (cn < min(SEL_TOPK, nsel)) & ((jrow + c * sub) * SEL_BLOCK <= tpos_row)
        dropped.append(jnp.where(keep_blk, 0.0, 1.0))
    dropped_t = jnp.concatenate(dropped + [jnp.zeros((LANES - nsel, tq), F32)], axis=0)
    dropped_q = dropped_t.T.astype(BF16)
    q_aug = jnp.concatenate([qs, jnp.concatenate([dropped_q] * rep, axis=0)], axis=1)

    tk = NSA_TK
    last = q0 // tk
    acc_ref[...] = jnp.zeros_like(acc_ref)

    def sel_tile(kt, carry, diagonal):
        m_i, l_i = carry
        k0 = pl.multiple_of(kt * tk, tk)
        k_aug = jnp.concatenate([ksel_ref[pl.ds(k0, tk), :], negexp_ref[pl.ds(k0, tk), :]], axis=1)
        v_t = vsel_ref[pl.ds(k0, tk), :]
        s_t = _nt_dot(q_aug, k_aug)
        if diagonal:
            kpos = k0 + lax.broadcasted_iota(jnp.int32, (1, tk), 1)
            causal = jnp.where(kpos <= tpos, 0.0, NEG_INF).astype(F32)
        m_out, l_out, alphas, ps = [], [], [], []
        for hs in heads:
            s = s_t[hs] + causal if diagonal else s_t[hs]
            m_new, l_new, alpha, p = _softmax_tile(s, m_i[hs], l_i[hs])
            m_out.append(m_new)
            l_out.append(l_new)
            alphas.append(alpha)
            ps.append(p.astype(BF16))
        pv = jnp.dot(jnp.concatenate(ps, axis=0), v_t, preferred_element_type=F32)
        acc_ref[...] = jnp.concatenate(alphas, axis=0) * acc_ref[...] + pv
        return jnp.concatenate(m_out, axis=0), jnp.concatenate(l_out, axis=0)

    init = (jnp.full((rep * tq, LANES), NEG_INF, F32), jnp.zeros((rep * tq, LANES), F32))
    carry = lax.fori_loop(0, last, functools.partial(sel_tile, diagonal=False), init)
    _, l_s = sel_tile(last, carry, True)
    o_sel = acc_ref[...] / l_s

    wk = WINDOW + tq
    w0 = pl.multiple_of(jnp.maximum(q0 - WINDOW, 0), tq)
    kw = kwin_ref[pl.ds(w0, wk), :]
    vw = vwin_ref[pl.ds(w0, wk), :]
    kpos = w0 + lax.broadcasted_iota(jnp.int32, (1, wk), 1)
    wbias = jnp.where((kpos <= tpos) & (kpos > tpos - WINDOW), 0.0, NEG_INF).astype(F32)
    sw_all = _nt_dot(qs, kw)
    pws, lws = [], []
    for hs in heads:
        sw = sw_all[hs] + wbias
        pw = jnp.exp2(EXP2_SCALE * (sw - jnp.max(sw, axis=-1, keepdims=True)))
        lws.append(jnp.sum(pw, axis=-1, keepdims=True))
        pws.append(pw.astype(BF16))
    o_win = jnp.dot(jnp.concatenate(pws, axis=0), vw, preferred_element_type=F32) / jnp.concatenate(lws, axis=0)

    gate = jax.nn.sigmoid(gate_ref[...])
    for r, rs in enumerate(heads):
        o = (gate[:, 3 * r:3 * r + 1] * o_cmp[rs] + gate[:, 3 * r + 1:3 * r + 2] * o_sel[rs]
             + gate[:, 3 * r + 2:3 * r + 3] * o_win[rs])
        o_ref[:, r * HEAD_DIM:(r + 1) * HEAD_DIM] = o.astype(o_ref.dtype)


def _nsa(roped, plain, cmp_kv, gates, batch, seq, pool, negexp):
    tq = NSA_TQ
    nq = seq // tq
    gw = NSA_REP * HEAD_DIM
    ksel_cb = NSA_HEADS
    kwin_cb = NSA_HEADS + NSA_GROUPS
    ncmp = seq // CMP_STRIDE
    seq_blk = lambda cb: pl.BlockSpec((seq, HEAD_DIM), functools.partial(lambda b, g, t, cb: (b, cb + g), cb=cb))
    return pl.pallas_call(
        _nsa_kernel,
        grid=(batch, NSA_GROUPS, nq),
        in_specs=[pl.BlockSpec((tq, gw), lambda b, g, t: (b * nq + t, g)),
                  pl.BlockSpec((None, None, ncmp, HEAD_DIM), lambda b, g, t: (b, g, 0, 0)),
                  pl.BlockSpec((None, None, ncmp, HEAD_DIM), lambda b, g, t: (b, NSA_GROUPS + g, 0, 0)),
                  seq_blk(ksel_cb), seq_blk(0), seq_blk(kwin_cb), seq_blk(NSA_GROUPS),
                  pl.BlockSpec((tq, LANES), lambda b, g, t: (b * nq + t, g)),
                  pl.BlockSpec(pool.shape, lambda b, g, t: (0, 0)),
                  pl.BlockSpec(negexp.shape, lambda b, g, t: (0, 0))],
        out_specs=pl.BlockSpec((tq, gw), lambda b, g, t: (b * nq + t, g)),
        out_shape=jax.ShapeDtypeStruct((batch * seq, NSA_HEADS * HEAD_DIM), BF16),
        scratch_shapes=[pltpu.VMEM((NSA_REP * tq, HEAD_DIM), F32)],
        compiler_params=_cparams(("parallel", "parallel", "arbitrary")),
        name="nsa_attention",
    )(roped, cmp_kv, cmp_kv, roped, plain, roped, plain, gates, pool, negexp)


DIFF_TQ = 256
DIFF_TK = 512


def _diff_kernel(q_ref, k_ref, v_ref, lam_ref, subln_ref, o_ref, acc_ref, *, lambda_init):
    tq, tk = DIFF_TQ, DIFF_TK
    q0 = pl.program_id(2) * tq
    last = q0 // tk
    q = q_ref[...]
    qm = (q[:, :HEAD_DIM], q[:, HEAD_DIM:])
    tpos = q0 + lax.broadcasted_iota(jnp.int32, (tq, 1), 0)
    maps = (slice(0, tq), slice(tq, 2 * tq))
    acc_ref[...] = jnp.zeros_like(acc_ref)

    def tile(kt, carry, diagonal):
        m_i, l_i = carry
        k0 = pl.multiple_of(kt * tk, tk)
        k = k_ref[pl.ds(k0, tk), :]
        v = v_ref[pl.ds(k0, tk), :]
        if diagonal:
            kpos = k0 + lax.broadcasted_iota(jnp.int32, (1, tk), 1)
            causal = jnp.where(kpos <= tpos, 0.0, NEG_INF).astype(F32)
        m_out, l_out, alphas, ps = [], [], [], []
        for i, ms in enumerate(maps):
            s = _nt_dot(qm[i], k[:, i * HEAD_DIM:(i + 1) * HEAD_DIM])
            if diagonal:
                s = s + causal
            m_new, l_new, alpha, p = _softmax_tile(s, m_i[ms], l_i[ms])
            m_out.append(m_new)
            l_out.append(l_new)
            alphas.append(jnp.concatenate([alpha] * (DIFF_V // LANES), axis=1))
            ps.append(p.astype(BF16))
        pv = jnp.dot(jnp.concatenate(ps, axis=0), v, preferred_element_type=F32)
        acc_ref[...] = jnp.concatenate(alphas, axis=0) * acc_ref[...] + pv
        return jnp.concatenate(m_out, axis=0), jnp.concatenate(l_out, axis=0)

    init = (jnp.full((2 * tq, LANES), NEG_INF, F32), jnp.zeros((2 * tq, LANES), F32))
    carry = lax.fori_loop(0, last, functools.partial(tile, diagonal=False), init)
    _, l_f = tile(last, carry, True)
    o = acc_ref[...] / jnp.concatenate([l_f] * (DIFF_V // LANES), axis=1)
    lam = lam_ref[...]
    lam_full = (jnp.exp(jnp.sum(lam[0:1] * lam[1:2], axis=-1, keepdims=True))
                - jnp.exp(jnp.sum(lam[2:3] * lam[3:4], axis=-1, keepdims=True)) + lambda_init)
    od = o[:tq] - lam_full * o[tq:]
    ms = jnp.mean(od * od, axis=-1, keepdims=True)
    o_ref[...] = (od * lax.rsqrt(ms + NORM_EPS) * subln_ref[...] * (1.0 - lambda_init)).astype(o_ref.dtype)


def _diff_attn(roped, plain, lam, subln, batch, seq, lambda_init):
    tq = DIFF_TQ
    nq = seq // tq
    pair = 2 * HEAD_DIM
    dq_cb = (NSA_HEADS + 2 * NSA_GROUPS) * HEAD_DIM // pair
    dk_cb = dq_cb + DIFF_HEADS
    dv_cb = 2 * NSA_GROUPS * HEAD_DIM // DIFF_V
    return pl.pallas_call(
        functools.partial(_diff_kernel, lambda_init=lambda_init),
        grid=(batch, DIFF_HEADS, nq),
        in_specs=[pl.BlockSpec((tq, pair), lambda b, h, t: (b * nq + t, dq_cb + h)),
                  pl.BlockSpec((seq, pair), lambda b, h, t: (b, dk_cb + h)),
                  pl.BlockSpec((seq, DIFF_V), lambda b, h, t: (b, dv_cb + h)),
                  pl.BlockSpec(lam.shape, lambda b, h, t: (0, 0)),
                  pl.BlockSpec((1, DIFF_V), lambda b, h, t: (0, 0))],
        out_specs=pl.BlockSpec((tq, DIFF_V), lambda b, h, t: (b * nq + t, h)),
        out_shape=jax.ShapeDtypeStruct((batch * seq, DIFF_HEADS * DIFF_V), BF16),
        scratch_shapes=[pltpu.VMEM((2 * tq, DIFF_V), F32)],
        compiler_params=_cparams(("parallel", "parallel", "arbitrary")),
        name="diff_attention",
    )(roped, roped, plain, lam, subln.reshape(1, DIFF_V))


def _rope_tables(pos):
    inv_freq = ROPE_THETA ** (-jnp.arange(ROPE_HALF, dtype=F32) / ROPE_HALF)
    ang = pos.astype(F32)[:, None] * inv_freq[None, :]
    cos, sin = jnp.cos(ang), jnp.sin(ang)
    rest = LANES - 2 * ROPE_HALF
    n = pos.shape[0]
    cosf = jnp.concatenate([cos, cos, jnp.ones((n, rest), F32)], axis=1)
    sinf = jnp.concatenate([-sin, sin, jnp.zeros((n, rest), F32)], axis=1)
    return cosf, sinf


def _router_weights(w_group, b_group, w_expert, b_expert):
    pad = LANES - MOE_GROUPS - MOE_EXPERTS
    wr = jnp.concatenate([w_group, w_expert, jnp.zeros((w_group.shape[0], pad), F32)], axis=1)
    br = jnp.concatenate([b_group, b_expert, jnp.zeros((pad,), F32)]).reshape(1, LANES)
    w_hi = wr.astype(BF16)
    w_lo = (wr - w_hi.astype(F32)).astype(BF16)
    return jnp.concatenate([w_hi, w_lo], axis=1), br


def _even_layer(h, hn, batch, seq, w_in, conv_w, conv_b, dt_bias, a_log, d_skip, gate_norm, sc_w, w_out,
                g_ffn, wr, br):
    i = SSD_INNER
    conv_ch = i + 2 * SSD_GROUPS * SSD_STATE
    o_dt = i + conv_ch
    o_sc = o_dt + SSD_HEADS
    w_main = jnp.concatenate([w_in[:, :o_dt], w_in[:, o_sc:]], axis=1).astype(BF16)
    w_dt = jnp.pad(w_in[:, o_dt:o_sc], ((0, 0), (0, LANES - SSD_HEADS))).astype(BF16)
    proj = _matmul(hn, w_main, F32, 1024, 1024)
    dt_raw = _matmul(hn, w_dt, F32, 512, LANES)
    mix = _even_core(proj, dt_raw, batch, seq, conv_w, conv_b, dt_bias, a_log, d_skip, gate_norm, sc_w)
    return _outproj([mix], [w_out.astype(BF16)], h, g_ffn, wr, br)


def _odd_layer(h, hn, batch, seq, w_in, cmp_pos, cmp_w1, cmp_w2, lam, subln, w_out, lambda_init, g_ffn, wr, br):
    hd, kvw = HEAD_DIM, NSA_GROUPS * HEAD_DIM
    o = [0]
    for wdt in (NSA_HEADS * hd,) + (kvw,) * 6 + (3 * NSA_HEADS,) + (DIFF_HEADS * 2 * hd,) * 2 + (DIFF_HEADS * DIFF_V,):
        o.append(o[-1] + wdt)
    col = lambda k: w_in[:, o[k]:o[k + 1]]
    q, k_cmp, v_cmp, k_sel, v_sel, k_win, v_win, gates, dq, dk, dv = [col(k) for k in range(11)]
    gcols = []
    for g in range(NSA_GROUPS):
        per = 3 * NSA_REP
        gcols.append(jnp.pad(gates[:, g * per:(g + 1) * per], ((0, 0), (0, LANES - per))))
    w_rope = jnp.concatenate([q, k_sel, k_win, dq, dk], axis=1).astype(BF16)
    w_plain = jnp.concatenate([v_sel, v_win, dv], axis=1).astype(BF16)
    w_cmp = jnp.concatenate([k_cmp, v_cmp], axis=1).astype(BF16)
    w_gates = jnp.concatenate(gcols, axis=1).astype(BF16)
    pos = jnp.arange(seq)
    roped = _matmul(hn, w_rope, BF16, 512, 512, rope=_rope_tables(pos), seq=seq)
    plain = _matmul(hn, w_plain, BF16, 512, 512)
    kv_cmp = _matmul(hn, w_cmp, F32, 512, 512)
    gate_lg = _matmul(hn, w_gates, F32, 512, 2 * LANES)
    ncmp = seq // CMP_STRIDE
    cmp_end = jnp.arange(ncmp) * CMP_STRIDE + CMP_CHUNKS * CMP_STRIDE - 1
    cos_c, sin_c = _rope_tables(cmp_end)
    cmp_kv = _compress(kv_cmp, batch, seq, cmp_pos, cmp_w1, cmp_w2, cos_c, sin_c)
    n_idx = jnp.arange(ncmp)[:, None]
    j_idx = jnp.arange(LANES)[None, :]
    per_sel = SEL_BLOCK // CMP_STRIDE
    pool = ((n_idx >= per_sel * j_idx - 1) & (n_idx <= per_sel * j_idx + per_sel - 1)
            & (n_idx < ncmp - 1) & (j_idx < seq // SEL_BLOCK)).astype(F32)
    negexp = jnp.where((jnp.arange(seq)[:, None] // SEL_BLOCK) == jnp.arange(LANES)[None, :], NEG_INF, 0.0).astype(BF16)
    o_nsa = _nsa(roped, plain, cmp_kv, gate_lg, batch, seq, pool, negexp)
    o_diff = _diff_attn(roped, plain, lam, subln, batch, seq, lambda_init)
    n_nsa = NSA_HEADS * hd
    w_o = w_out.astype(BF16)
    return _outproj([o_nsa, o_diff], [w_o[:n_nsa], w_o[n_nsa:]], h, g_ffn, wr, br)


def kernel(x, p, norm_mix, norm_ffn, norm_ple, norm_final, ev_w_in, ev_conv_w, ev_conv_b, ev_dt_bias, ev_a_log, ev_d_skip, ev_gate_norm, ev_sc_w, ev_w_out, od_w_in, od_cmp_pos, od_cmp_w1, od_cmp_w2, od_lambda, od_subln, od_w_out, moe_w_group, moe_b_group, moe_w_expert, moe_b_expert, moe_w_gate, moe_w_up, moe_w_down, ple_gate, ple_proj):
    batch, seq, d = x.shape
    n = batch * seq
    depth = p.shape[0]
    h = x.reshape(n, d)
    hn = _rmsnorm(h, norm_mix[0], BF16)
    for i in range(depth):
        j = i // 2
        wr, br = _router_weights(moe_w_group[i], moe_b_group[i], moe_w_expert[i], moe_b_expert[i])
        if i % 2 == 0:
            h, hn2, logits = _even_layer(h, hn, batch, seq, ev_w_in[j], ev_conv_w[j], ev_conv_b[j], ev_dt_bias[j],
                                         ev_a_log[j], ev_d_skip[j], ev_gate_norm[j], ev_sc_w[j], ev_w_out[j],
                                         norm_ffn[i], wr, br)
        else:
            lambda_init = 0.8 - 0.6 * math.exp(-0.3 * i)
            h, hn2, logits = _odd_layer(h, hn, batch, seq, od_w_in[j], od_cmp_pos[j], od_cmp_w1[j], od_cmp_w2[j],
                                        od_lambda[j], od_subln[j], od_w_out[j], lambda_init, norm_ffn[i], wr, br)
        yb, slot1, slot2, routed = _moe(hn2, logits, i, moe_w_gate, moe_w_up, moe_w_down)
        last = i == depth - 1
        g_next = norm_final if last else norm_mix[i + 1]
        h, hn = _ple(h, yb, slot1, slot2, routed, p.reshape(depth * n, PLE_DIM), i, norm_ple[i],
                     ple_gate[i].astype(BF16), ple_proj[i].astype(BF16), g_next, F32 if last else BF16)
    return hn.reshape(batch, seq, d)
```

```python
import functools
import math

import jax
import jax.numpy as jnp
from jax import lax
from jax.experimental import pallas as pl
from jax.experimental.pallas import tpu as pltpu

F32 = jnp.float32
BF16 = jnp.bfloat16
HIGHEST = lax.Precision.HIGHEST

LANES = 128
D_MODEL = 2048
NORM_EPS = 1e-6
ROPE_THETA = 500000.0
ROPE_HALF = 16
NEG_INF = -1e30

SSD_HEADS = 32
SSD_HEAD_DIM = 64
SSD_GROUPS = 8
SSD_STATE = 128
SSD_CHUNK = 128
SSD_INNER = 2048
CONV_HALO = 8

HEAD_DIM = 128
NSA_HEADS = 8
NSA_GROUPS = 2
NSA_REP = 4
CMP_STRIDE = 16
CMP_CHUNKS = 2
SEL_BLOCK = 64
SEL_TOPK = 16
SEL_FORCE = 1e4
WINDOW = 512
DIFF_HEADS = 4
DIFF_V = 256

MOE_GROUPS = 4
MOE_PER_GROUP = 8
MOE_EXPERTS = 32
MOE_FF = 1024
MOE_ROWS = 256
ROUTER_TILE = 512
PLE_DIM = 256

VMEM_LIMIT = 56 * 1024 * 1024


def _cparams(sem, vmem=VMEM_LIMIT):
    return pltpu.CompilerParams(dimension_semantics=sem, vmem_limit_bytes=vmem)


def _nt_dot(a, b):
    return lax.dot_general(a, b, (((1,), (1,)), ((), ())), preferred_element_type=F32)


def _silu(x):
    return x * jax.nn.sigmoid(x)


def _rmsnorm_kernel(x_ref, g_ref, o_ref):
    x = x_ref[...]
    ms = jnp.mean(x * x, axis=-1, keepdims=True)
    o_ref[...] = (x * lax.rsqrt(ms + NORM_EPS) * g_ref[...]).astype(o_ref.dtype)


def _rmsnorm(x, g, out_dtype, tm=512):
    n, d = x.shape
    return pl.pallas_call(
        _rmsnorm_kernel,
        grid=(n // tm,),
        in_specs=[pl.BlockSpec((tm, d), lambda i: (i, 0)), pl.BlockSpec((1, d), lambda i: (0, 0))],
        out_specs=pl.BlockSpec((tm, d), lambda i: (i, 0)),
        out_shape=jax.ShapeDtypeStruct((n, d), out_dtype),
        compiler_params=_cparams(("parallel",)),
        name="rmsnorm",
    )(x, g.reshape(1, d))


def _rope_apply(x, cosf, sinf, lane):
    swapped = jnp.where(lane < ROPE_HALF, pltpu.roll(x, LANES - ROPE_HALF, 1), pltpu.roll(x, ROPE_HALF, 1))
    return x * cosf + swapped * sinf


def _mm_kernel(a_ref, w_ref, o_ref):
    o_ref[...] = jnp.dot(a_ref[...], w_ref[...], preferred_element_type=F32).astype(o_ref.dtype)


def _mm_rope_kernel(a_ref, w_ref, cos_ref, sin_ref, o_ref):
    acc = jnp.dot(a_ref[...], w_ref[...], preferred_element_type=F32)
    tm, tn = acc.shape
    cosf = cos_ref[...]
    sinf = sin_ref[...]
    lane = lax.broadcasted_iota(jnp.int32, (tm, LANES), 1)
    for h in range(tn // LANES):
        sl = slice(h * LANES, (h + 1) * LANES)
        o_ref[:, sl] = _rope_apply(acc[:, sl], cosf, sinf, lane).astype(o_ref.dtype)


def _matmul(a, w, out_dtype, tm, tn, rope=None, seq=None):
    m, k = a.shape
    n = w.shape[1]
    in_specs = [pl.BlockSpec((tm, k), lambda i, j: (i, 0)), pl.BlockSpec((k, tn), lambda i, j: (0, j))]
    args = [a, w]
    body = _mm_kernel
    if rope is not None:
        per_seq = seq // tm
        in_specs += [pl.BlockSpec((tm, LANES), lambda i, j: (i % per_seq, 0))] * 2
        args += list(rope)
        body = _mm_rope_kernel
    return pl.pallas_call(
        body,
        grid=(m // tm, n // tn),
        in_specs=in_specs,
        out_specs=pl.BlockSpec((tm, tn), lambda i, j: (i, j)),
        out_shape=jax.ShapeDtypeStruct((m, n), out_dtype),
        compiler_params=_cparams(("parallel", "parallel")),
        name="proj_rope" if rope is not None else "proj",
    )(*args)


def _even_core_kernel(z_ref, xs_ref, bc_ref, scb_ref, scc_ref, sch_ref,
                      xs_h_ref, bc_h_ref, scc_h_ref, sch_h_ref, dt_ref,
                      cwx_ref, cwbc_ref, cbx_ref, cbbc_ref, dtb_ref, alog_ref, dskip_ref, gnorm_ref, scw_ref,
                      o_ref, state_ref, xsc_ref, bcc_ref, y_ref):
    q = SSD_CHUNK
    first = pl.program_id(1) == 0
    keep = jnp.where(first, 0.0, 1.0).astype(F32)

    @pl.when(first)
    def _():
        state_ref[...] = jnp.zeros_like(state_ref)

    strip = 512

    def causal_conv(load_main, load_halo, w_ref, width, finish):
        for c0 in range(0, SSD_INNER, strip):
            cs = slice(c0, c0 + strip)
            ext = jnp.concatenate([load_halo(cs) * keep, load_main(cs)], axis=0)
            acc = w_ref[width - 1:width, cs] * ext[CONV_HALO:]
            for s in range(1, width):
                acc = acc + w_ref[width - 1 - s:width - s, cs] * pltpu.roll(ext, s, 0)[CONV_HALO:]
            finish(cs, acc)

    def fin_xs(cs, acc):
        xsc_ref[:, cs] = _silu(acc + cbx_ref[:, cs])

    def fin_bc(cs, acc):
        bcc_ref[:, cs] = _silu(acc + cbbc_ref[:, cs])

    causal_conv(lambda cs: xs_ref[:, cs], lambda cs: xs_h_ref[:, cs], cwx_ref, 4, fin_xs)
    causal_conv(lambda cs: bc_ref[:, cs], lambda cs: bc_h_ref[:, cs], cwbc_ref, 4, fin_bc)

    def fin_sc(cs, acc):
        o_ref[:, SSD_INNER + cs.start:SSD_INNER + cs.stop] = (scb_ref[:, cs] * acc).astype(o_ref.dtype)

    causal_conv(lambda cs: scc_ref[:, cs] * sch_ref[:, cs], lambda cs: scc_h_ref[:, cs] * sch_h_ref[:, cs],
                scw_ref, 3, fin_sc)

    dt = jax.nn.softplus(dt_ref[...] + dtb_ref[...])
    d_a = dt * (-jnp.exp(alog_ref[...]))
    row = lax.broadcasted_iota(jnp.int32, (q, q), 0)
    col = lax.broadcasted_iota(jnp.int32, (q, q), 1)
    causal = row >= col
    tril = jnp.where(causal, 1.0, 0.0).astype(F32)
    a_cum = jnp.dot(tril, d_a, preferred_element_type=F32, precision=HIGHEST)
    a_last = a_cum[q - 1:q, :]
    decay_end = jnp.exp(a_last - a_cum)
    chunk_decay = jnp.exp(a_last)
    exp_acum = jnp.exp(a_cum)
    a_cum_t = a_cum.T
    lane = lax.broadcasted_iota(jnp.int32, (q, LANES), 1)
    lo = lane < SSD_HEAD_DIM
    lane_row = lax.broadcasted_iota(jnp.int32, (1, LANES), 1)
    lo_row = lane_row < SSD_HEAD_DIM

    def pair_cols(mat, h0):
        return jnp.where(lo, mat[:, h0:h0 + 1], mat[:, h0 + 1:h0 + 2])

    for g in range(SSD_GROUPS):
        gs = slice(g * SSD_STATE, (g + 1) * SSD_STATE)
        b_g = bcc_ref[:, gs]
        c_g = bcc_ref[:, SSD_GROUPS * SSD_STATE + g * SSD_STATE:SSD_GROUPS * SSD_STATE + (g + 1) * SSD_STATE]
        b_gt = b_g.T.astype(BF16)
        c_gb = c_g.astype(BF16)
        cb = jnp.dot(c_gb, b_gt, preferred_element_type=F32)
        for hp in range(2):
            pr = g * 2 + hp
            h0 = 2 * pr
            ps = slice(pr * LANES, (pr + 1) * LANES)
            xp = xsc_ref[:, ps]
            xdt = xp * pair_cols(dt, h0)
            mats = []
            for h in (h0, h0 + 1):
                seg = a_cum[:, h:h + 1] - a_cum_t[h:h + 1, :]
                dec = jnp.where(causal, jnp.exp(jnp.minimum(seg, 0.0)), 0.0)
                mats.append((cb * dec).astype(BF16))
            lhs = jnp.concatenate(mats, axis=1)
            rhs = jnp.concatenate([jnp.where(lo, xdt, 0.0), jnp.where(lo, 0.0, xdt)], axis=0).astype(BF16)
            y = jnp.dot(lhs, rhs, preferred_element_type=F32)
            st = state_ref[pr]
            y = y + jnp.dot(c_gb, st.astype(BF16), preferred_element_type=F32) * pair_cols(exp_acum, h0)
            xw = (xdt * pair_cols(decay_end, h0)).astype(BF16)
            cd = jnp.where(lo_row, chunk_decay[:, h0:h0 + 1], chunk_decay[:, h0 + 1:h0 + 2])
            state_ref[pr] = st * cd + jnp.dot(b_gt, xw, preferred_element_type=F32)
            y_ref[:, ps] = y + dskip_ref[:, ps] * xp

    gw = SSD_INNER // SSD_GROUPS
    for g in range(SSD_GROUPS):
        cs = slice(g * gw, (g + 1) * gw)
        yg = y_ref[:, cs] * _silu(z_ref[:, cs])
        ms = jnp.mean(yg * yg, axis=-1, keepdims=True)
        o_ref[:, cs] = (yg * lax.rsqrt(ms + NORM_EPS) * gnorm_ref[:, cs]).astype(o_ref.dtype)


def _even_core(proj, dt_raw, batch, seq, conv_w, conv_b, dt_bias, a_log, d_skip, gate_norm, sc_w):
    q = SSD_CHUNK
    nc = seq // q
    w = SSD_INNER
    hb = q // CONV_HALO

    def main(cb):
        return pl.BlockSpec((q, w), lambda b, c: (b * nc + c, cb))

    def halo(cb):
        return pl.BlockSpec((CONV_HALO, w), lambda b, c: (jnp.maximum((b * nc + c) * hb - 1, 0), cb))

    def full(shape):
        return pl.BlockSpec(shape, lambda b, c: (0, 0))

    pad = LANES - SSD_HEADS
    dtb = jnp.pad(dt_bias, (0, pad)).reshape(1, LANES)
    alog = jnp.pad(a_log, (0, pad)).reshape(1, LANES)
    dskip = jnp.repeat(d_skip, SSD_HEAD_DIM).reshape(1, w)
    in_specs = [main(0), main(1), main(2), main(3), main(4), main(5),
                halo(1), halo(2), halo(4), halo(5),
                pl.BlockSpec((q, LANES), lambda b, c: (b * nc + c, 0)),
                full((4, w)), full((4, w)), full((1, w)), full((1, w)),
                full((1, LANES)), full((1, LANES)), full((1, w)), full((1, w)), full((3, w))]
    return pl.pallas_call(
        _even_core_kernel,
        grid=(batch, nc),
        in_specs=in_specs,
        out_specs=pl.BlockSpec((q, 2 * w), lambda b, c: (b * nc + c, 0)),
        out_shape=jax.ShapeDtypeStruct((batch * seq, 2 * w), BF16),
        scratch_shapes=[pltpu.VMEM((SSD_HEADS // 2, SSD_STATE, LANES), F32),
                        pltpu.VMEM((q, w), F32), pltpu.VMEM((q, w), F32), pltpu.VMEM((q, w), F32)],
        compiler_params=_cparams(("parallel", "arbitrary")),
        name="ssd_conv_core",
    )(proj, proj, proj, proj, proj, proj, proj, proj, proj, proj, dt_raw,
      conv_w[:, :w], conv_w[:, w:], conv_b[:w].reshape(1, w), conv_b[w:].reshape(1, w),
      dtb, alog, dskip, gate_norm.reshape(1, w), sc_w)


def _outproj_kernel(*refs, n_in):
    mix = refs[:n_in]
    ws = refs[n_in:2 * n_in]
    h_ref, g_ref, wr_ref, br_ref, hnew_ref, hn_ref, lg_ref = refs[2 * n_in:]
    acc = h_ref[...]
    for m_ref, w_ref in zip(mix, ws):
        acc = acc + jnp.dot(m_ref[...], w_ref[...], preferred_element_type=F32)
    hnew_ref[...] = acc
    ms = jnp.mean(acc * acc, axis=-1, keepdims=True)
    hn = acc * lax.rsqrt(ms + NORM_EPS) * g_ref[...]
    hn_ref[...] = hn
    hi = hn.astype(BF16)
    lo = (hn - hi.astype(F32)).astype(BF16)
    wr = wr_ref[...]
    t = jnp.dot(hi, wr, preferred_element_type=F32)
    u = jnp.dot(lo, wr[:, :LANES], preferred_element_type=F32)
    lg_ref[...] = t[:, :LANES] + t[:, LANES:] + u + br_ref[...]


def _outproj(mixes, ws, h, g, wr, br, tm=256):
    n, d = h.shape
    n_in = len(mixes)
    in_specs = ([pl.BlockSpec((tm, m.shape[1]), lambda i: (i, 0)) for m in mixes]
                + [pl.BlockSpec(w.shape, lambda i: (0, 0)) for w in ws]
                + [pl.BlockSpec((tm, d), lambda i: (i, 0)), pl.BlockSpec((1, d), lambda i: (0, 0)),
                   pl.BlockSpec((d, 2 * LANES), lambda i: (0, 0)), pl.BlockSpec((1, LANES), lambda i: (0, 0))])
    return pl.pallas_call(
        functools.partial(_outproj_kernel, n_in=n_in),
        grid=(n // tm,),
        in_specs=in_specs,
        out_specs=[pl.BlockSpec((tm, d), lambda i: (i, 0)), pl.BlockSpec((tm, d), lambda i: (i, 0)),
                   pl.BlockSpec((tm, LANES), lambda i: (i, 0))],
        out_shape=[jax.ShapeDtypeStruct((n, d), F32), jax.ShapeDtypeStruct((n, d), F32),
                   jax.ShapeDtypeStruct((n, LANES), F32)],
        compiler_params=_cparams(("parallel",)),
        name="outproj_norm_router",
    )(*mixes, *ws, h, g.reshape(1, d), wr, br)


ROUTE_E1, ROUTE_E2, ROUTE_W1, ROUTE_W2, ROUTE_R1, ROUTE_R2 = range(6)


def _router_kernel(lg_ref, out_ref, cnt_ref, run_ref):
    @pl.when(pl.program_id(0) == 0)
    def _():
        run_ref[...] = jnp.zeros_like(run_ref)

    x = lg_ref[...]
    t = x.shape[0]
    lane = lax.broadcasted_iota(jnp.int32, (t, LANES), 1).astype(F32)
    far = float(LANES)
    gmask = lane < MOE_GROUPS
    gl = jnp.where(gmask, x, NEG_INF)
    gmax = jnp.max(gl, axis=-1, keepdims=True)
    g_top = jnp.min(jnp.where(gl == gmax, lane, far), axis=-1, keepdims=True)
    gsum = jnp.sum(jnp.where(gmask, jnp.exp(gl - gmax), 0.0), axis=-1, keepdims=True)
    g_prob = 1.0 / gsum
    lo = MOE_GROUPS + MOE_PER_GROUP * g_top
    emask = (lane >= lo) & (lane < lo + MOE_PER_GROUP)
    el = jnp.where(emask, x, NEG_INF)
    emax = jnp.max(el, axis=-1, keepdims=True)
    ee = jnp.where(emask, jnp.exp(el - emax), 0.0)
    ep = jnp.where(emask, ee / jnp.sum(ee, axis=-1, keepdims=True), -1.0)
    p1 = jnp.max(ep, axis=-1, keepdims=True)
    i1 = jnp.min(jnp.where(ep == p1, lane, far), axis=-1, keepdims=True)
    ep2 = jnp.where(lane == i1, -1.0, ep)
    p2 = jnp.max(ep2, axis=-1, keepdims=True)
    i2 = jnp.min(jnp.where(ep2 == p2, lane, far), axis=-1, keepdims=True)
    den = p1 + p2
    w1 = g_prob * p1 / den
    w2 = g_prob * p2 / den
    oh1 = jnp.where(lane == i1, 1.0, 0.0)
    oh2 = jnp.where(lane == i2, 1.0, 0.0)
    oh = oh1 + oh2
    row = lax.broadcasted_iota(jnp.int32, (t, t), 0)
    col = lax.broadcasted_iota(jnp.int32, (t, t), 1)
    strict = jnp.where(row > col, 1.0, 0.0).astype(BF16)
    before = jnp.dot(strict, oh.astype(BF16), preferred_element_type=F32) + run_ref[...]
    r1 = jnp.sum(oh1 * before, axis=-1, keepdims=True)
    r2 = jnp.sum(oh2 * before, axis=-1, keepdims=True)
    run_ref[...] = run_ref[...] + jnp.sum(oh, axis=0, keepdims=True)
    cnt_ref[...] = run_ref[...]
    packed = jnp.zeros((t, LANES), F32)
    for k, v in ((ROUTE_E1, i1 - MOE_GROUPS), (ROUTE_E2, i2 - MOE_GROUPS), (ROUTE_W1, w1), (ROUTE_W2, w2),
                 (ROUTE_R1, r1), (ROUTE_R2, r2)):
        packed = jnp.where(lane == k, v, packed)
    out_ref[...] = packed


def _router(logits, tt=ROUTER_TILE):
    n = logits.shape[0]
    return pl.pallas_call(
        _router_kernel,
        grid=(n // tt,),
        in_specs=[pl.BlockSpec((tt, LANES), lambda i: (i, 0))],
        out_specs=[pl.BlockSpec((tt, LANES), lambda i: (i, 0)), pl.BlockSpec((1, LANES), lambda i: (0, 0))],
        out_shape=[jax.ShapeDtypeStruct((n, LANES), F32), jax.ShapeDtypeStruct((1, LANES), F32)],
        scratch_shapes=[pltpu.VMEM((1, LANES), F32)],
        compiler_params=_cparams(("arbitrary",)),
        name="router_rank",
    )(logits)


SLOT_1, SLOT_2 = 0, 1
BLK_HALVES = 2


def _slot_kernel(rt_ref, cnt_ref, slot_ref, blk_ref):
    lane_row = lax.broadcasted_iota(jnp.int32, (1, LANES), 1)
    is_exp = (lane_row >= MOE_GROUPS) & (lane_row < MOE_GROUPS + MOE_EXPERTS)
    padded = jnp.where(is_exp, jnp.floor((cnt_ref[...] + (MOE_ROWS - 1)) * (1.0 / MOE_ROWS)) * MOE_ROWS, 0.0)
    r = lax.broadcasted_iota(jnp.int32, (LANES, LANES), 0)
    c = lax.broadcasted_iota(jnp.int32, (LANES, LANES), 1)
    incl = jnp.where(r <= c, 1.0, 0.0).astype(F32)
    pad_end = jnp.dot(jnp.broadcast_to(padded, (8, LANES)), incl, preferred_element_type=F32,
                      precision=HIGHEST)[0:1]
    pad_start = pad_end - padded
    rt = rt_ref[...]
    lane = lax.broadcasted_iota(jnp.int32, rt.shape, 1).astype(F32)

    def slot(e, rank):
        return jnp.sum(jnp.where(lane == e + MOE_GROUPS, pad_start, 0.0), axis=-1, keepdims=True) + rank

    s1 = slot(rt[:, ROUTE_E1:ROUTE_E1 + 1], rt[:, ROUTE_R1:ROUTE_R1 + 1])
    s2 = slot(rt[:, ROUTE_E2:ROUTE_E2 + 1], rt[:, ROUTE_R2:ROUTE_R2 + 1])
    slot_ref[...] = jnp.where(lane == SLOT_1, s1, jnp.where(lane == SLOT_2, s2, 0.0))

    end_col = jnp.broadcast_to(pad_end, (LANES, LANES)).T
    exp_col = (r >= MOE_GROUPS) & (r < MOE_GROUPS + MOE_EXPERTS)
    last = MOE_GROUPS + MOE_EXPERTS - 1
    total = pad_end[:, last:last + 1]

    def experts_done(start):
        return jnp.sum(jnp.where(exp_col & (end_col <= start), 1.0, 0.0), axis=0, keepdims=True)

    idle = experts_done(total - MOE_ROWS)
    rows = []
    for h in range(BLK_HALVES):
        start = (lane_row + h * LANES).astype(F32) * MOE_ROWS
        valid = start < total
        rows += [jnp.where(valid, experts_done(start), idle), jnp.where(valid, 1.0, 0.0)]
    tail = jnp.where(is_exp & (padded > 0.0), pad_end - MOE_ROWS, -1.0)
    tail = jnp.where(lane_row == MOE_GROUPS + MOE_EXPERTS, total, tail)
    blk_ref[...] = jnp.concatenate(rows + [tail, jnp.zeros((8 - 2 * BLK_HALVES - 1, LANES), F32)], axis=0)


def _slots(routed, counts, tt=ROUTER_TILE):
    n = routed.shape[0]
    return pl.pallas_call(
        _slot_kernel,
        grid=(n // tt,),
        in_specs=[pl.BlockSpec((tt, LANES), lambda i: (i, 0)), pl.BlockSpec((1, LANES), lambda i: (0, 0))],
        out_specs=[pl.BlockSpec((tt, LANES), lambda i: (i, 0)), pl.BlockSpec((8, LANES), lambda i: (0, 0))],
        out_shape=[jax.ShapeDtypeStruct((n, LANES), F32), jax.ShapeDtypeStruct((8, LANES), F32)],
        compiler_params=_cparams(("arbitrary",)),
        name="moe_slots",
    )(routed, counts)


def _moe_kernel(be_ref, bv_ref, x_ref, wg_ref, wu_ref, wd_ref, o_ref):
    i = pl.program_id(0)

    @pl.when(bv_ref[i] > 0)
    def _():
        x = x_ref[...].astype(BF16)
        step = 256
        for n0 in range(0, MOE_FF, step):
            gate = jnp.dot(x, wg_ref[:, n0:n0 + step].astype(BF16), preferred_element_type=F32)
            up = jnp.dot(x, wu_ref[:, n0:n0 + step].astype(BF16), preferred_element_type=F32)
            hid = (_silu(gate) * up).astype(BF16)
            part = jnp.dot(hid, wd_ref[n0:n0 + step, :].astype(BF16), preferred_element_type=F32)
            if n0 == 0:
                o_ref[...] = part
            else:
                o_ref[...] += part

    @pl.when(bv_ref[i] == 0)
    def _():
        o_ref[...] = jnp.zeros_like(o_ref)


def _moe_ffn(xs, blk_expert, blk_valid, layer, wg, wu, wd):
    cap, d = xs.shape
    nb = cap // MOE_ROWS
    ff = wg.shape[3]
    grid_spec = pltpu.PrefetchScalarGridSpec(
        num_scalar_prefetch=2,
        grid=(nb,),
        in_specs=[pl.BlockSpec((MOE_ROWS, d), lambda i, be, bv: (i * bv[i], 0)),
                  pl.BlockSpec((None, None, d, ff), lambda i, be, bv: (layer, be[i], 0, 0)),
                  pl.BlockSpec((None, None, d, ff), lambda i, be, bv: (layer, be[i], 0, 0)),
                  pl.BlockSpec((None, None, ff, d), lambda i, be, bv: (layer, be[i], 0, 0),
                               pipeline_mode=pl.Buffered(1))],
        out_specs=pl.BlockSpec((MOE_ROWS, d), lambda i, be, bv: (i, 0)),
    )
    return pl.pallas_call(
        _moe_kernel,
        grid_spec=grid_spec,
        out_shape=jax.ShapeDtypeStruct((cap, d), F32),
        compiler_params=_cparams(("arbitrary",)),
        name="moe_ffn",
    )(blk_expert, blk_valid, xs, wg, wu, wd)


DISPATCH_TILE = 256
ROW_DMA_UNROLL = 8


def _row_copy(src_ref, src_row, dst_ref, dst_row, sem):
    return pltpu.make_async_copy(src_ref.at[pl.ds(src_row, 1), :], dst_ref.at[pl.ds(dst_row, 1), :], sem)


def _dispatch_kernel(s1_ref, s2_ref, tail_ref, hn_ref, xs_ref, zero_ref, sem, zsem):
    t = hn_ref.shape[0]
    base = pl.program_id(0) * t

    @pl.when(pl.program_id(0) == 0)
    def _():
        zero_ref[...] = jnp.zeros_like(zero_ref)

        def fill(e):
            start = pl.multiple_of(jnp.maximum(tail_ref[e], 0), MOE_ROWS)
            return pltpu.make_async_copy(zero_ref, xs_ref.at[pl.ds(start, MOE_ROWS), :], zsem)

        for e in range(MOE_EXPERTS):
            @pl.when(tail_ref[e] >= 0)
            def _():
                fill(e).start()
        for e in range(MOE_EXPERTS):
            @pl.when(tail_ref[e] >= 0)
            def _():
                fill(e).wait()

        used = tail_ref[MOE_EXPERTS]
        idle = (xs_ref.shape[0] - used) // MOE_ROWS

        def idle_fill(j):
            start = pl.multiple_of(used + j * MOE_ROWS, MOE_ROWS)
            return pltpu.make_async_copy(zero_ref, xs_ref.at[pl.ds(start, MOE_ROWS), :], zsem)

        def idle_start(j, carry):
            idle_fill(j).start()
            return carry

        def idle_wait(j, carry):
            idle_fill(j).wait()
            return carry

        lax.fori_loop(0, idle, idle_start, 0)
        lax.fori_loop(0, idle, idle_wait, 0)

    def issue(r, carry):
        _row_copy(hn_ref, r, xs_ref, s1_ref[base + r], sem.at[0]).start()
        _row_copy(hn_ref, r, xs_ref, s2_ref[base + r], sem.at[1]).start()
        return carry

    lax.fori_loop(0, t, issue, 0, unroll=ROW_DMA_UNROLL)

    def drain(r, carry):
        _row_copy(hn_ref, 0, xs_ref, 0, sem.at[0]).wait()
        _row_copy(hn_ref, 0, xs_ref, 0, sem.at[1]).wait()
        return carry

    lax.fori_loop(0, t, drain, 0, unroll=ROW_DMA_UNROLL)


def _dispatch(hn, slot1, slot2, tail, cap, tt=DISPATCH_TILE):
    n, d = hn.shape
    grid_spec = pltpu.PrefetchScalarGridSpec(
        num_scalar_prefetch=3,
        grid=(n // tt,),
        in_specs=[pl.BlockSpec((tt, d), lambda i, s1, s2, tl: (i, 0))],
        out_specs=pl.BlockSpec(memory_space=pl.ANY),
        scratch_shapes=[pltpu.VMEM((MOE_ROWS, d), F32), pltpu.SemaphoreType.DMA((2,)), pltpu.SemaphoreType.DMA(())],
    )
    return pl.pallas_call(
        _dispatch_kernel,
        grid_spec=grid_spec,
        out_shape=jax.ShapeDtypeStruct((cap, d), F32),
        compiler_params=_cparams(("arbitrary",)),
        name="moe_dispatch",
    )(slot1, slot2, tail, hn)


def _moe(hn, logits, layer, wg, wu, wd):
    n, d = hn.shape
    routed, counts = _router(logits)
    slots, blk = _slots(routed, counts)
    slot1 = slots[:, SLOT_1].astype(jnp.int32)
    slot2 = slots[:, SLOT_2].astype(jnp.int32)
    nb = -(-(2 * n + MOE_EXPERTS * (MOE_ROWS - 1)) // MOE_ROWS)
    assert nb <= BLK_HALVES * LANES
    cap = nb * MOE_ROWS
    blk_expert = blk[0:2 * BLK_HALVES:2].reshape(-1)[:nb].astype(jnp.int32)
    blk_valid = blk[1:2 * BLK_HALVES:2].reshape(-1)[:nb].astype(jnp.int32)
    tail = blk[2 * BLK_HALVES, MOE_GROUPS:MOE_GROUPS + MOE_EXPERTS + 1].astype(jnp.int32)
    xs = _dispatch(hn, slot1, slot2, tail, cap)
    yb = _moe_ffn(xs, blk_expert, blk_valid, layer, wg, wu, wd)
    return yb, slot1, slot2, routed


def _ple_kernel(s1_ref, s2_ref, h_ref, yb_ref, rt_ref, p_ref, gp_ref, wg_ref, wp_ref, gn_ref, hout_ref, nout_ref,
                ybuf_ref, sem):
    tm = h_ref.shape[0]
    i = pl.program_id(0)
    cur = i % 2

    def gather(step, buf, start):
        base = step * tm

        def one(r, carry):
            for k, s_ref in enumerate((s1_ref, s2_ref)):
                row = s_ref[base + r] if start else 0
                cp = _row_copy(yb_ref, row, ybuf_ref.at[buf, k], r if start else 0, sem.at[buf, k])
                if start:
                    cp.start()
                else:
                    cp.wait()
            return carry

        lax.fori_loop(0, tm, one, 0, unroll=ROW_DMA_UNROLL)

    @pl.when(i == 0)
    def _():
        gather(0, 0, True)

    @pl.when(i + 1 < pl.num_programs(0))
    def _():
        gather(i + 1, 1 - cur, True)

    gather(i, cur, False)
    rt = rt_ref[...]
    moe = ybuf_ref[cur, 0] * rt[:, ROUTE_W1:ROUTE_W1 + 1] + ybuf_ref[cur, 1] * rt[:, ROUTE_W2:ROUTE_W2 + 1]
    h = h_ref[...] + moe
    ms = jnp.mean(h * h, axis=-1, keepdims=True)
    hn = (h * lax.rsqrt(ms + NORM_EPS) * gp_ref[...]).astype(BF16)
    gate = jax.nn.sigmoid(jnp.dot(hn, wg_ref[...], preferred_element_type=F32))
    emb = jnp.dot(p_ref[...].astype(BF16), wp_ref[...], preferred_element_type=F32)
    h = h + gate * emb
    hout_ref[...] = h
    ms = jnp.mean(h * h, axis=-1, keepdims=True)
    nout_ref[...] = (h * lax.rsqrt(ms + NORM_EPS) * gn_ref[...]).astype(nout_ref.dtype)


def _ple(h, yb, slot1, slot2, routed, p, layer, g_ple, w_gate, w_proj, g_next, next_dtype, tm=256):
    n, d = h.shape
    row = lambda i, s1, s2: (i, 0)
    fixed = lambda i, s1, s2: (0, 0)
    grid_spec = pltpu.PrefetchScalarGridSpec(
        num_scalar_prefetch=2,
        grid=(n // tm,),
        in_specs=[pl.BlockSpec((tm, d), row), pl.BlockSpec(memory_space=pl.ANY),
                  pl.BlockSpec((tm, LANES), row),
                  pl.BlockSpec((tm, PLE_DIM), lambda i, s1, s2: (layer * (n // tm) + i, 0)),
                  pl.BlockSpec((1, d), fixed), pl.BlockSpec((d, d), fixed), pl.BlockSpec((PLE_DIM, d), fixed),
                  pl.BlockSpec((1, d), fixed)],
        out_specs=[pl.BlockSpec((tm, d), row), pl.BlockSpec((tm, d), row)],
        scratch_shapes=[pltpu.VMEM((2, 2, tm, d), F32), pltpu.SemaphoreType.DMA((2, 2))],
    )
    return pl.pallas_call(
        _ple_kernel,
        grid_spec=grid_spec,
        out_shape=[jax.ShapeDtypeStruct((n, d), F32), jax.ShapeDtypeStruct((n, d), next_dtype)],
        compiler_params=_cparams(("arbitrary",)),
        name="ple_norm",
    )(slot1, slot2, h, yb, routed, p, g_ple.reshape(1, d), w_gate, w_proj, g_next.reshape(1, d))


def _compress_kernel(x_ref, pos_ref, w1_ref, w2_ref, cos_ref, sin_ref, o_ref):
    nchunk = x_ref.shape[0]
    width = 2 * NSA_GROUPS * HEAD_DIM
    lane = lax.broadcasted_iota(jnp.int32, (nchunk, LANES), 1)
    for kv in range(2):
        for g in range(NSA_GROUPS):
            acc = [jnp.zeros((nchunk, HEAD_DIM), F32) for _ in range(CMP_CHUNKS)]
            for half in range(CMP_CHUNKS):
                for j in range(CMP_STRIDE):
                    c0 = j * width + kv * NSA_GROUPS * HEAD_DIM + g * HEAD_DIM
                    jj = half * CMP_STRIDE + j
                    blk = (x_ref[:, c0:c0 + HEAD_DIM] + pos_ref[kv, jj:jj + 1, :]).astype(BF16)
                    acc[half] = acc[half] + jnp.dot(blk, w1_ref[kv, jj * HEAD_DIM:(jj + 1) * HEAD_DIM, :],
                                                    preferred_element_type=F32)
            pre = acc[0] + pltpu.roll(acc[1], nchunk - 1, 0)
            out = jnp.dot(_silu(pre).astype(BF16), w2_ref[kv], preferred_element_type=F32)
            if kv == 0:
                out = _rope_apply(out, cos_ref[...], sin_ref[...], lane)
            o_ref[kv * NSA_GROUPS + g] = out.astype(o_ref.dtype)


def _compress(kv_cmp, batch, seq, cmp_pos, cmp_w1, cmp_w2, cos_c, sin_c):
    nchunk = seq // CMP_STRIDE
    width = kv_cmp.shape[1]
    x = kv_cmp.reshape(batch * nchunk, CMP_STRIDE * width)
    return pl.pallas_call(
        _compress_kernel,
        grid=(batch,),
        in_specs=[pl.BlockSpec((nchunk, CMP_STRIDE * width), lambda b: (b, 0)),
                  pl.BlockSpec(cmp_pos.shape, lambda b: (0, 0, 0)),
                  pl.BlockSpec(cmp_w1.shape, lambda b: (0, 0, 0)),
                  pl.BlockSpec(cmp_w2.shape, lambda b: (0, 0, 0)),
                  pl.BlockSpec((nchunk, LANES), lambda b: (0, 0)),
                  pl.BlockSpec((nchunk, LANES), lambda b: (0, 0))],
        out_specs=pl.BlockSpec((None, 2 * NSA_GROUPS, nchunk, HEAD_DIM), lambda b: (b, 0, 0, 0)),
        out_shape=jax.ShapeDtypeStruct((batch, 2 * NSA_GROUPS, nchunk, HEAD_DIM), BF16),
        compiler_params=_cparams(("parallel",)),
        name="nsa_compress",
    )(x, cmp_pos, cmp_w1.astype(BF16), cmp_w2.astype(BF16), cos_c, sin_c)


NSA_TQ = 256
NSA_TK = 512
EXP2_SCALE = HEAD_DIM ** -0.5 * math.log2(math.e)


def _softmax_tile(s, m_old, l_old):
    m_new = jnp.maximum(m_old, jnp.max(s, axis=-1, keepdims=True))
    alpha = jnp.exp2(EXP2_SCALE * (m_old - m_new))
    p = jnp.exp2(EXP2_SCALE * (s - jnp.concatenate([m_new] * (s.shape[1] // LANES), axis=1)))
    l_new = alpha * l_old + jnp.sum(p, axis=-1, keepdims=True)
    return m_new, l_new, alpha, p


def _nsa_kernel(q_ref, kc_ref, vc_ref, ksel_ref, vsel_ref, kwin_ref, vwin_ref, gate_ref, pool_ref, negexp_ref,
                o_ref, acc_ref):
    tq = NSA_TQ
    rep = NSA_REP
    q0 = pl.program_id(2) * tq
    q = q_ref[...]
    qs = jnp.concatenate([q[:, r * HEAD_DIM:(r + 1) * HEAD_DIM] for r in range(rep)], axis=0)
    tpos = q0 + lax.broadcasted_iota(jnp.int32, (tq, 1), 0)
    heads = [slice(r * tq, (r + 1) * tq) for r in range(rep)]

    ncmp = kc_ref.shape[0]
    cend = lax.broadcasted_iota(jnp.int32, (1, ncmp), 1) * CMP_STRIDE + (CMP_CHUNKS * CMP_STRIDE - 1)
    cvalid = cend <= tpos
    cbias = jnp.where(cvalid, 0.0, NEG_INF).astype(F32)
    s_all = _nt_dot(qs, kc_ref[...])
    probs = []
    for hs in heads:
        s = s_all[hs] + cbias
        m = jnp.max(s, axis=-1, keepdims=True)
        p = jnp.where(cvalid, jnp.exp2(EXP2_SCALE * (s - m)), 0.0)
        l = jnp.sum(p, axis=-1, keepdims=True)
        probs.append(p * jnp.where(l > 0.0, 1.0 / l, 0.0))
    o_cmp = jnp.dot(jnp.concatenate(probs, axis=0).astype(BF16), vc_ref[...], preferred_element_type=F32)
    imp = probs[0]
    for p in probs[1:]:
        imp = imp + p
    p_sel = jnp.dot(imp, pool_ref[...], preferred_element_type=F32, precision=HIGHEST)

    nsel = ksel_ref.shape[0] // SEL_BLOCK
    blk = lax.broadcasted_iota(jnp.int32, (tq, LANES), 1)
    cur = tpos // SEL_BLOCK
    forced = (blk == 0) | (blk == cur) | (blk == cur - 1)
    valid = blk * SEL_BLOCK <= tpos
    score = jnp.where(forced, SEL_FORCE, jnp.where(valid, p_sel, NEG_INF))
    score = jnp.where(blk < nsel, score, 3.0 * NEG_INF)
    st = score.T
    sub = 8
    chunks = [st[c * sub:(c + 1) * sub, :] for c in range(nsel // sub)]
    cnt = [jnp.zeros((sub, tq), F32) for _ in chunks]
    jrow = lax.broadcasted_iota(jnp.int32, (sub, tq), 0)
    for i in range(nsel):
        ci, ii = divmod(i, sub)
        si = chunks[ci][ii:ii + 1, :]
        for c in range(len(chunks)):
            if c > ci:
                beats = si >= chunks[c]
            elif c < ci:
                beats = si > chunks[c]
            else:
                beats = jnp.where(jrow > ii, jnp.where(si >= chunks[c], 1.0, 0.0), jnp.where(si > chunks[c], 1.0, 0.0)) > 0.5
            cnt[c] = cnt[c] + jnp.where(beats, 1.0, 0.0)
    tpos_row = q0 + lax.broadcasted_iota(jnp.int32, (sub, tq), 1)
    dropped = []
    for c, cn in enumerate(cnt):
        keep_blk = (cn < min(SEL_TOPK, nsel)) & ((jrow + c * sub) * SEL_BLOCK <= tpos_row)
        dropped.append(jnp.where(keep_blk, 0.0, 1.0))
    dropped_t = jnp.concatenate(dropped + [jnp.zeros((LANES - nsel, tq), F32)], axis=0)
    dropped_q = dropped_t.T.astype(BF16)
    q_aug = jnp.concatenate([qs, jnp.concatenate([dropped_q] * rep, axis=0)], axis=1)

    tk = NSA_TK
    last = q0 // tk
    acc_ref[...] = jnp.zeros_like(acc_ref)

    def sel_tile(kt, carry, diagonal):
        m_i, l_i = carry
        k0 = pl.multiple_of(kt * tk, tk)
        k_aug = jnp.concatenate([ksel_ref[pl.ds(k0, tk), :], negexp_ref[pl.ds(k0, tk), :]], axis=1)
        v_t = vsel_ref[pl.ds(k0, tk), :]
        s_t = _nt_dot(q_aug, k_aug)
        if diagonal:
            kpos = k0 + lax.broadcasted_iota(jnp.int32, (1, tk), 1)
            causal = jnp.where(kpos <= tpos, 0.0, NEG_INF).astype(F32)
        m_out, l_out, alphas, ps = [], [], [], []
        for hs in heads:
            s = s_t[hs] + causal if diagonal else s_t[hs]
            m_new, l_new, alpha, p = _softmax_tile(s, m_i[hs], l_i[hs])
            m_out.append(m_new)
            l_out.append(l_new)
            alphas.append(alpha)
            ps.append(p.astype(BF16))
        pv = jnp.dot(jnp.concatenate(ps, axis=0), v_t, preferred_element_type=F32)
        acc_ref[...] = jnp.concatenate(alphas, axis=0) * acc_ref[...] + pv
        return jnp.concatenate(m_out, axis=0), jnp.concatenate(l_out, axis=0)

    init = (jnp.full((rep * tq, LANES), NEG_INF, F32), jnp.zeros((rep * tq, LANES), F32))
    carry = lax.fori_loop(0, last, functools.partial(sel_tile, diagonal=False), init)
    _, l_s = sel_tile(last, carry, True)
    o_sel = acc_ref[...] / l_s

    wk = WINDOW + tq
    w0 = pl.multiple_of(jnp.maximum(q0 - WINDOW, 0), tq)
    kw = kwin_ref[pl.ds(w0, wk), :]
    vw = vwin_ref[pl.ds(w0, wk), :]
    kpos = w0 + lax.broadcasted_iota(jnp.int32, (1, wk), 1)
    wbias = jnp.where((kpos <= tpos) & (kpos > tpos - WINDOW), 0.0, NEG_INF).astype(F32)
    sw_all = _nt_dot(qs, kw)
    pws, lws = [], []
    for hs in heads:
        sw = sw_all[hs] + wbias
        pw = jnp.exp2(EXP2_SCALE * (sw - jnp.max(sw, axis=-1, keepdims=True)))
        lws.append(jnp.sum(pw, axis=-1, keepdims=True))
        pws.append(pw.astype(BF16))
    o_win = jnp.dot(jnp.concatenate(pws, axis=0), vw, preferred_element_type=F32) / jnp.concatenate(lws, axis=0)

    gate = jax.nn.sigmoid(gate_ref[...])
    for r, rs in enumerate(heads):
        o = (gate[:, 3 * r:3 * r + 1] * o_cmp[rs] + gate[:, 3 * r + 1:3 * r + 2] * o_sel[rs]
             + gate[:, 3 * r + 2:3 * r + 3] * o_win[rs])
        o_ref[:, r * HEAD_DIM:(r + 1) * HEAD_DIM] = o.astype(o_ref.dtype)


def _nsa(roped, plain, cmp_kv, gates, batch, seq, pool, negexp):
    tq = NSA_TQ
    nq = seq // tq
    gw = NSA_REP * HEAD_DIM
    ksel_cb = NSA_HEADS
    kwin_cb = NSA_HEADS + NSA_GROUPS
    ncmp = seq // CMP_STRIDE
    seq_blk = lambda cb: pl.BlockSpec((seq, HEAD_DIM), functools.partial(lambda b, g, t, cb: (b, cb + g), cb=cb))
    return pl.pallas_call(
        _nsa_kernel,
        grid=(batch, NSA_GROUPS, nq),
        in_specs=[pl.BlockSpec((tq, gw), lambda b, g, t: (b * nq + t, g)),
                  pl.BlockSpec((None, None, ncmp, HEAD_DIM), lambda b, g, t: (b, g, 0, 0)),
                  pl.BlockSpec((None, None, ncmp, HEAD_DIM), lambda b, g, t: (b, NSA_GROUPS + g, 0, 0)),
                  seq_blk(ksel_cb), seq_blk(0), seq_blk(kwin_cb), seq_blk(NSA_GROUPS),
                  pl.BlockSpec((tq, LANES), lambda b, g, t: (b * nq + t, g)),
                  pl.BlockSpec(pool.shape, lambda b, g, t: (0, 0)),
                  pl.BlockSpec(negexp.shape, lambda b, g, t: (0, 0))],
        out_specs=pl.BlockSpec((tq, gw), lambda b, g, t: (b * nq + t, g)),
        out_shape=jax.ShapeDtypeStruct((batch * seq, NSA_HEADS * HEAD_DIM), BF16),
        scratch_shapes=[pltpu.VMEM((NSA_REP * tq, HEAD_DIM), F32)],
        compiler_params=_cparams(("parallel", "parallel", "arbitrary")),
        name="nsa_attention",
    )(roped, cmp_kv, cmp_kv, roped, plain, roped, plain, gates, pool, negexp)


DIFF_TQ = 512
DIFF_TK = 512


def _diff_kernel(q_ref, k_ref, v_ref, lam_ref, subln_ref, o_ref, acc_ref, *, lambda_init):
    tq, tk = DIFF_TQ, DIFF_TK
    q0 = pl.program_id(2) * tq
    last = q0 // tk
    q = q_ref[...]
    qm = (q[:, :HEAD_DIM], q[:, HEAD_DIM:])
    tpos = q0 + lax.broadcasted_iota(jnp.int32, (tq, 1), 0)
    maps = (slice(0, tq), slice(tq, 2 * tq))
    acc_ref[...] = jnp.zeros_like(acc_ref)

    def tile(kt, carry, diagonal):
        m_i, l_i = carry
        k0 = pl.multiple_of(kt * tk, tk)
        k = k_ref[pl.ds(k0, tk), :]
        v = v_ref[pl.ds(k0, tk), :]
        if diagonal:
            kpos = k0 + lax.broadcasted_iota(jnp.int32, (1, tk), 1)
            causal = jnp.where(kpos <= tpos, 0.0, NEG_INF).astype(F32)
        m_out, l_out, alphas, ps = [], [], [], []
        for i, ms in enumerate(maps):
            s = _nt_dot(qm[i], k[:, i * HEAD_DIM:(i + 1) * HEAD_DIM])
            if diagonal:
                s = s + causal
            m_new, l_new, alpha, p = _softmax_tile(s, m_i[ms], l_i[ms])
            m_out.append(m_new)
            l_out.append(l_new)
            alphas.append(jnp.concatenate([alpha] * (DIFF_V // LANES), axis=1))
            ps.append(p.astype(BF16))
        pv = jnp.dot(jnp.concatenate(ps, axis=0), v, preferred_element_type=F32)
        acc_ref[...] = jnp.concatenate(alphas, axis=0) * acc_ref[...] + pv
        return jnp.concatenate(m_out, axis=0), jnp.concatenate(l_out, axis=0)

    init = (jnp.full((2 * tq, LANES), NEG_INF, F32), jnp.zeros((2 * tq, LANES), F32))
    carry = lax.fori_loop(0, last, functools.partial(tile, diagonal=False), init)
    _, l_f = tile(last, carry, True)
    o = acc_ref[...] / jnp.concatenate([l_f] * (DIFF_V // LANES), axis=1)
    lam = lam_ref[...]
    lam_full = (jnp.exp(jnp.sum(lam[0:1] * lam[1:2], axis=-1, keepdims=True))
                - jnp.exp(jnp.sum(lam[2:3] * lam[3:4], axis=-1, keepdims=True)) + lambda_init)
    od = o[:tq] - lam_full * o[tq:]
    ms = jnp.mean(od * od, axis=-1, keepdims=True)
    o_ref[...] = (od * lax.rsqrt(ms + NORM_EPS) * subln_ref[...] * (1.0 - lambda_init)).astype(o_ref.dtype)


def _diff_attn(roped, plain, lam, subln, batch, seq, lambda_init):
    tq = DIFF_TQ
    nq = seq // tq
    pair = 2 * HEAD_DIM
    dq_cb = (NSA_HEADS + 2 * NSA_GROUPS) * HEAD_DIM // pair
    dk_cb = dq_cb + DIFF_HEADS
    dv_cb = 2 * NSA_GROUPS * HEAD_DIM // DIFF_V
    return pl.pallas_call(
        functools.partial(_diff_kernel, lambda_init=lambda_init),
        grid=(batch, DIFF_HEADS, nq),
        in_specs=[pl.BlockSpec((tq, pair), lambda b, h, t: (b * nq + t, dq_cb + h)),
                  pl.BlockSpec((seq, pair), lambda b, h, t: (b, dk_cb + h)),
                  pl.BlockSpec((seq, DIFF_V), lambda b, h, t: (b, dv_cb + h)),
                  pl.BlockSpec(lam.shape, lambda b, h, t: (0, 0)),
                  pl.BlockSpec((1, DIFF_V), lambda b, h, t: (0, 0))],
        out_specs=pl.BlockSpec((tq, DIFF_V), lambda b, h, t: (b * nq + t, h)),
        out_shape=jax.ShapeDtypeStruct((batch * seq, DIFF_HEADS * DIFF_V), BF16),
        scratch_shapes=[pltpu.VMEM((2 * tq, DIFF_V), F32)],
        compiler_params=_cparams(("parallel", "parallel", "arbitrary")),
        name="diff_attention",
    )(roped, roped, plain, lam, subln.reshape(1, DIFF_V))


def _rope_tables(pos):
    inv_freq = ROPE_THETA ** (-jnp.arange(ROPE_HALF, dtype=F32) / ROPE_HALF)
    ang = pos.astype(F32)[:, None] * inv_freq[None, :]
    cos, sin = jnp.cos(ang), jnp.sin(ang)
    rest = LANES - 2 * ROPE_HALF
    n = pos.shape[0]
    cosf = jnp.concatenate([cos, cos, jnp.ones((n, rest), F32)], axis=1)
    sinf = jnp.concatenate([-sin, sin, jnp.zeros((n, rest), F32)], axis=1)
    return cosf, sinf


def _router_weights(w_group, b_group, w_expert, b_expert):
    pad = LANES - MOE_GROUPS - MOE_EXPERTS
    wr = jnp.concatenate([w_group, w_expert, jnp.zeros((w_group.shape[0], pad), F32)], axis=1)
    br = jnp.concatenate([b_group, b_expert, jnp.zeros((pad,), F32)]).reshape(1, LANES)
    w_hi = wr.astype(BF16)
    w_lo = (wr - w_hi.astype(F32)).astype(BF16)
    return jnp.concatenate([w_hi, w_lo], axis=1), br


def _even_layer(h, hn, batch, seq, w_in, conv_w, conv_b, dt_bias, a_log, d_skip, gate_norm, sc_w, w_out,
                g_ffn, wr, br):
    i = SSD_INNER
    conv_ch = i + 2 * SSD_GROUPS * SSD_STATE
    o_dt = i + conv_ch
    o_sc = o_dt + SSD_HEADS
    w_main = jnp.concatenate([w_in[:, :o_dt], w_in[:, o_sc:]], axis=1).astype(BF16)
    w_dt = jnp.pad(w_in[:, o_dt:o_sc], ((0, 0), (0, LANES - SSD_HEADS))).astype(BF16)
    proj = _matmul(hn, w_main, F32, 1024, 1024)
    dt_raw = _matmul(hn, w_dt, F32, 512, LANES)
    mix = _even_core(proj, dt_raw, batch, seq, conv_w, conv_b, dt_bias, a_log, d_skip, gate_norm, sc_w)
    return _outproj([mix], [w_out.astype(BF16)], h, g_ffn, wr, br)


def _odd_layer(h, hn, batch, seq, w_in, cmp_pos, cmp_w1, cmp_w2, lam, subln, w_out, lambda_init, g_ffn, wr, br):
    hd, kvw = HEAD_DIM, NSA_GROUPS * HEAD_DIM
    o = [0]
    for wdt in (NSA_HEADS * hd,) + (kvw,) * 6 + (3 * NSA_HEADS,) + (DIFF_HEADS * 2 * hd,) * 2 + (DIFF_HEADS * DIFF_V,):
        o.append(o[-1] + wdt)
    col = lambda k: w_in[:, o[k]:o[k + 1]]
    q, k_cmp, v_cmp, k_sel, v_sel, k_win, v_win, gates, dq, dk, dv = [col(k) for k in range(11)]
    gcols = []
    for g in range(NSA_GROUPS):
        per = 3 * NSA_REP
        gcols.append(jnp.pad(gates[:, g * per:(g + 1) * per], ((0, 0), (0, LANES - per))))
    w_rope = jnp.concatenate([q, k_sel, k_win, dq, dk], axis=1).astype(BF16)
    w_plain = jnp.concatenate([v_sel, v_win, dv], axis=1).astype(BF16)
    w_cmp = jnp.concatenate([k_cmp, v_cmp], axis=1).astype(BF16)
    w_gates = jnp.concatenate(gcols, axis=1).astype(BF16)
    pos = jnp.arange(seq)
    roped = _matmul(hn, w_rope, BF16, 512, 512, rope=_rope_tables(pos), seq=seq)
    plain = _matmul(hn, w_plain, BF16, 512, 512)
    kv_cmp = _matmul(hn, w_cmp, F32, 512, 512)
    gate_lg = _matmul(hn, w_gates, F32, 512, 2 * LANES)
    ncmp = seq // CMP_STRIDE
    cmp_end = jnp.arange(ncmp) * CMP_STRIDE + CMP_CHUNKS * CMP_STRIDE - 1
    cos_c, sin_c = _rope_tables(cmp_end)
    cmp_kv = _compress(kv_cmp, batch, seq, cmp_pos, cmp_w1, cmp_w2, cos_c, sin_c)
    n_idx = jnp.arange(ncmp)[:, None]
    j_idx = jnp.arange(LANES)[None, :]
    per_sel = SEL_BLOCK // CMP_STRIDE
    pool = ((n_idx >= per_sel * j_idx - 1) & (n_idx <= per_sel * j_idx + per_sel - 1)
            & (n_idx < ncmp - 1) & (j_idx < seq // SEL_BLOCK)).astype(F32)
    negexp = jnp.where((jnp.arange(seq)[:, None] // SEL_BLOCK) == jnp.arange(LANES)[None, :], NEG_INF, 0.0).astype(BF16)
    o_nsa = _nsa(roped, plain, cmp_kv, gate_lg, batch, seq, pool, negexp)
    o_diff = _diff_attn(roped, plain, lam, subln, batch, seq, lambda_init)
    n_nsa = NSA_HEADS * hd
    w_o = w_out.astype(BF16)
    return _outproj([o_nsa, o_diff], [w_o[:n_nsa], w_o[n_nsa:]], h, g_ffn, wr, br)


def kernel(x, p, norm_mix, norm_ffn, norm_ple, norm_final, ev_w_in, ev_conv_w, ev_conv_b, ev_dt_bias, ev_a_log, ev_d_skip, ev_gate_norm, ev_sc_w, ev_w_out, od_w_in, od_cmp_pos, od_cmp_w1, od_cmp_w2, od_lambda, od_subln, od_w_out, moe_w_group, moe_b_group, moe_w_expert, moe_b_expert, moe_w_gate, moe_w_up, moe_w_down, ple_gate, ple_proj):
    batch, seq, d = x.shape
    n = batch * seq
    depth = p.shape[0]
    h = x.reshape(n, d)
    hn = _rmsnorm(h, norm_mix[0], BF16)
    for i in range(depth):
        j = i // 2
        wr, br = _router_weights(moe_w_group[i], moe_b_group[i], moe_w_expert[i], moe_b_expert[i])
        if i % 2 == 0:
            h, hn2, logits = _even_layer(h, hn, batch, seq, ev_w_in[j], ev_conv_w[j], ev_conv_b[j], ev_dt_bias[j],
                                         ev_a_log[j], ev_d_skip[j], ev_gate_norm[j], ev_sc_w[j], ev_w_out[j],
                                         norm_ffn[i], wr, br)
        else:
            lambda_init = 0.8 - 0.6 * math.exp(-0.3 * i)
            h, hn2, logits = _odd_layer(h, hn, batch, seq, od_w_in[j], od_cmp_pos[j], od_cmp_w1[j], od_cmp_w2[j],
                                        od_lambda[j], od_subln[j], od_w_out[j], lambda_init, norm_ffn[i], wr, br)
        yb, slot1, slot2, routed = _moe(hn2, logits, i, moe_w_gate, moe_w_up, moe_w_down)
        last = i == depth - 1
        g_next = norm_final if last else norm_mix[i + 1]
        h, hn = _ple(h, yb, slot1, slot2, routed, p.reshape(depth * n, PLE_DIM), i, norm_ple[i],
                     ple_gate[i].astype(BF16), ple_proj[i].astype(BF16), g_next, F32 if last else BF16)
    return hn.reshape(batch, seq, d)
```

```python
import functools
import math

import jax
import jax.numpy as jnp
from jax import lax
from jax.experimental import pallas as pl
from jax.experimental.pallas import tpu as pltpu

F32 = jnp.float32
BF16 = jnp.bfloat16
HIGHEST = lax.Precision.HIGHEST

LANES = 128
D_MODEL = 2048
NORM_EPS = 1e-6
ROPE_THETA = 500000.0
ROPE_HALF = 16
NEG_INF = -1e30

SSD_HEADS = 32
SSD_HEAD_DIM = 64
SSD_GROUPS = 8
SSD_STATE = 128
SSD_CHUNK = 128
SSD_INNER = 2048
CONV_HALO = 8

HEAD_DIM = 128
NSA_HEADS = 8
NSA_GROUPS = 2
NSA_REP = 4
CMP_STRIDE = 16
CMP_CHUNKS = 2
SEL_BLOCK = 64
SEL_TOPK = 16
SEL_FORCE = 1e4
WINDOW = 512
DIFF_HEADS = 4
DIFF_V = 256

MOE_GROUPS = 4
MOE_PER_GROUP = 8
MOE_EXPERTS = 32
MOE_FF = 1024
MOE_ROWS = 256
ROUTER_TILE = 512
PLE_DIM = 256

VMEM_LIMIT = 56 * 1024 * 1024


def _cparams(sem, vmem=VMEM_LIMIT):
    return pltpu.CompilerParams(dimension_semantics=sem, vmem_limit_bytes=vmem)


def _nt_dot(a, b):
    return lax.dot_general(a, b, (((1,), (1,)), ((), ())), preferred_element_type=F32)


def _silu(x):
    return x * jax.nn.sigmoid(x)


def _rmsnorm_kernel(x_ref, g_ref, o_ref):
    x = x_ref[...]
    ms = jnp.mean(x * x, axis=-1, keepdims=True)
    o_ref[...] = (x * lax.rsqrt(ms + NORM_EPS) * g_ref[...]).astype(o_ref.dtype)


def _rmsnorm(x, g, out_dtype, tm=512):
    n, d = x.shape
    return pl.pallas_call(
        _rmsnorm_kernel,
        grid=(n // tm,),
        in_specs=[pl.BlockSpec((tm, d), lambda i: (i, 0)), pl.BlockSpec((1, d), lambda i: (0, 0))],
        out_specs=pl.BlockSpec((tm, d), lambda i: (i, 0)),
        out_shape=jax.ShapeDtypeStruct((n, d), out_dtype),
        compiler_params=_cparams(("parallel",)),
        name="rmsnorm",
    )(x, g.reshape(1, d))


def _rope_apply(x, cosf, sinf, lane):
    swapped = jnp.where(lane < ROPE_HALF, pltpu.roll(x, LANES - ROPE_HALF, 1), pltpu.roll(x, ROPE_HALF, 1))
    return x * cosf + swapped * sinf


def _mm_kernel(a_ref, w_ref, o_ref):
    o_ref[...] = jnp.dot(a_ref[...], w_ref[...], preferred_element_type=F32).astype(o_ref.dtype)


def _mm_rope_kernel(a_ref, w_ref, cos_ref, sin_ref, o_ref):
    acc = jnp.dot(a_ref[...], w_ref[...], preferred_element_type=F32)
    tm, tn = acc.shape
    cosf = cos_ref[...]
    sinf = sin_ref[...]
    lane = lax.broadcasted_iota(jnp.int32, (tm, LANES), 1)
    for h in range(tn // LANES):
        sl = slice(h * LANES, (h + 1) * LANES)
        o_ref[:, sl] = _rope_apply(acc[:, sl], cosf, sinf, lane).astype(o_ref.dtype)


def _matmul(a, w, out_dtype, tm, tn, rope=None, seq=None):
    m, k = a.shape
    n = w.shape[1]
    in_specs = [pl.BlockSpec((tm, k), lambda i, j: (i, 0)), pl.BlockSpec((k, tn), lambda i, j: (0, j))]
    args = [a, w]
    body = _mm_kernel
    if rope is not None:
        per_seq = seq // tm
        in_specs += [pl.BlockSpec((tm, LANES), lambda i, j: (i % per_seq, 0))] * 2
        args += list(rope)
        body = _mm_rope_kernel
    return pl.pallas_call(
        body,
        grid=(m // tm, n // tn),
        in_specs=in_specs,
        out_specs=pl.BlockSpec((tm, tn), lambda i, j: (i, j)),
        out_shape=jax.ShapeDtypeStruct((m, n), out_dtype),
        compiler_params=_cparams(("parallel", "parallel")),
        name="proj_rope" if rope is not None else "proj",
    )(*args)


def _even_core_kernel(z_ref, xs_ref, bc_ref, scb_ref, scc_ref, sch_ref,
                      xs_h_ref, bc_h_ref, scc_h_ref, sch_h_ref, dt_ref,
                      cwx_ref, cwbc_ref, cbx_ref, cbbc_ref, dtb_ref, alog_ref, dskip_ref, gnorm_ref, scw_ref,
                      o_ref, state_ref, xsc_ref, bcc_ref, y_ref):
    q = SSD_CHUNK
    first = pl.program_id(1) == 0
    keep = jnp.where(first, 0.0, 1.0).astype(F32)

    @pl.when(first)
    def _():
        state_ref[...] = jnp.zeros_like(state_ref)

    strip = 512

    def causal_conv(load_main, load_halo, w_ref, width, finish):
        for c0 in range(0, SSD_INNER, strip):
            cs = slice(c0, c0 + strip)
            ext = jnp.concatenate([load_halo(cs) * keep, load_main(cs)], axis=0)
            acc = w_ref[width - 1:width, cs] * ext[CONV_HALO:]
            for s in range(1, width):
                acc = acc + w_ref[width - 1 - s:width - s, cs] * pltpu.roll(ext, s, 0)[CONV_HALO:]
            finish(cs, acc)

    def fin_xs(cs, acc):
        xsc_ref[:, cs] = _silu(acc + cbx_ref[:, cs])

    def fin_bc(cs, acc):
        bcc_ref[:, cs] = _silu(acc + cbbc_ref[:, cs])

    causal_conv(lambda cs: xs_ref[:, cs], lambda cs: xs_h_ref[:, cs], cwx_ref, 4, fin_xs)
    causal_conv(lambda cs: bc_ref[:, cs], lambda cs: bc_h_ref[:, cs], cwbc_ref, 4, fin_bc)

    def fin_sc(cs, acc):
        o_ref[:, SSD_INNER + cs.start:SSD_INNER + cs.stop] = (scb_ref[:, cs] * acc).astype(o_ref.dtype)

    causal_conv(lambda cs: scc_ref[:, cs] * sch_ref[:, cs], lambda cs: scc_h_ref[:, cs] * sch_h_ref[:, cs],
                scw_ref, 3, fin_sc)

    dt = jax.nn.softplus(dt_ref[...] + dtb_ref[...])
    d_a = dt * (-jnp.exp(alog_ref[...]))
    row = lax.broadcasted_iota(jnp.int32, (q, q), 0)
    col = lax.broadcasted_iota(jnp.int32, (q, q), 1)
    causal = row >= col
    tril = jnp.where(causal, 1.0, 0.0).astype(F32)
    a_cum = jnp.dot(tril, d_a, preferred_element_type=F32, precision=HIGHEST)
    a_last = a_cum[q - 1:q, :]
    decay_end = jnp.exp(a_last - a_cum)
    chunk_decay = jnp.exp(a_last)
    exp_acum = jnp.exp(a_cum)
    a_cum_t = a_cum.T
    lane = lax.broadcasted_iota(jnp.int32, (q, LANES), 1)
    lo = lane < SSD_HEAD_DIM
    lane_row = lax.broadcasted_iota(jnp.int32, (1, LANES), 1)
    lo_row = lane_row < SSD_HEAD_DIM

    def pair_cols(mat, h0):
        return jnp.where(lo, mat[:, h0:h0 + 1], mat[:, h0 + 1:h0 + 2])

    for g in range(SSD_GROUPS):
        gs = slice(g * SSD_STATE, (g + 1) * SSD_STATE)
        b_g = bcc_ref[:, gs]
        c_g = bcc_ref[:, SSD_GROUPS * SSD_STATE + g * SSD_STATE:SSD_GROUPS * SSD_STATE + (g + 1) * SSD_STATE]
        b_gt = b_g.T.astype(BF16)
        c_gb = c_g.astype(BF16)
        cb = jnp.dot(c_gb, b_gt, preferred_element_type=F32)
        for hp in range(2):
            pr = g * 2 + hp
            h0 = 2 * pr
            ps = slice(pr * LANES, (pr + 1) * LANES)
            xp = xsc_ref[:, ps]
            xdt = xp * pair_cols(dt, h0)
            mats = []
            for h in (h0, h0 + 1):
                seg = a_cum[:, h:h + 1] - a_cum_t[h:h + 1, :]
                dec = jnp.where(causal, jnp.exp(jnp.minimum(seg, 0.0)), 0.0)
                mats.append((cb * dec).astype(BF16))
            lhs = jnp.concatenate(mats, axis=1)
            rhs = jnp.concatenate([jnp.where(lo, xdt, 0.0), jnp.where(lo, 0.0, xdt)], axis=0).astype(BF16)
            y = jnp.dot(lhs, rhs, preferred_element_type=F32)
            st = state_ref[pr]
            y = y + jnp.dot(c_gb, st.astype(BF16), preferred_element_type=F32) * pair_cols(exp_acum, h0)
            xw = (xdt * pair_cols(decay_end, h0)).astype(BF16)
            cd = jnp.where(lo_row, chunk_decay[:, h0:h0 + 1], chunk_decay[:, h0 + 1:h0 + 2])
            state_ref[pr] = st * cd + jnp.dot(b_gt, xw, preferred_element_type=F32)
            y_ref[:, ps] = y + dskip_ref[:, ps] * xp

    gw = SSD_INNER // SSD_GROUPS
    for g in range(SSD_GROUPS):
        cs = slice(g * gw, (g + 1) * gw)
        yg = y_ref[:, cs] * _silu(z_ref[:, cs])
        ms = jnp.mean(yg * yg, axis=-1, keepdims=True)
        o_ref[:, cs] = (yg * lax.rsqrt(ms + NORM_EPS) * gnorm_ref[:, cs]).astype(o_ref.dtype)


def _even_core(proj, dt_raw, batch, seq, conv_w, conv_b, dt_bias, a_log, d_skip, gate_norm, sc_w):
    q = SSD_CHUNK
    nc = seq // q
    w = SSD_INNER
    hb = q // CONV_HALO

    def main(cb):
        return pl.BlockSpec((q, w), lambda b, c: (b * nc + c, cb))

    def halo(cb):
        return pl.BlockSpec((CONV_HALO, w), lambda b, c: (jnp.maximum((b * nc + c) * hb - 1, 0), cb))

    def full(shape):
        return pl.BlockSpec(shape, lambda b, c: (0, 0))

    pad = LANES - SSD_HEADS
    dtb = jnp.pad(dt_bias, (0, pad)).reshape(1, LANES)
    alog = jnp.pad(a_log, (0, pad)).reshape(1, LANES)
    dskip = jnp.repeat(d_skip, SSD_HEAD_DIM).reshape(1, w)
    in_specs = [main(0), main(1), main(2), main(3), main(4), main(5),
                halo(1), halo(2), halo(4), halo(5),
                pl.BlockSpec((q, LANES), lambda b, c: (b * nc + c, 0)),
                full((4, w)), full((4, w)), full((1, w)), full((1, w)),
                full((1, LANES)), full((1, LANES)), full((1, w)), full((1, w)), full((3, w))]
    return pl.pallas_call(
        _even_core_kernel,
        grid=(batch, nc),
        in_specs=in_specs,
        out_specs=pl.BlockSpec((q, 2 * w), lambda b, c: (b * nc + c, 0)),
        out_shape=jax.ShapeDtypeStruct((batch * seq, 2 * w), BF16),
        scratch_shapes=[pltpu.VMEM((SSD_HEADS // 2, SSD_STATE, LANES), F32),
                        pltpu.VMEM((q, w), F32), pltpu.VMEM((q, w), F32), pltpu.VMEM((q, w), F32)],
        compiler_params=_cparams(("parallel", "arbitrary")),
        name="ssd_conv_core",
    )(proj, proj, proj, proj, proj, proj, proj, proj, proj, proj, dt_raw,
      conv_w[:, :w], conv_w[:, w:], conv_b[:w].reshape(1, w), conv_b[w:].reshape(1, w),
      dtb, alog, dskip, gate_norm.reshape(1, w), sc_w)


def _outproj_kernel(*refs, n_in):
    mix = refs[:n_in]
    ws = refs[n_in:2 * n_in]
    h_ref, g_ref, wr_ref, br_ref, hnew_ref, hn_ref, lg_ref = refs[2 * n_in:]
    acc = h_ref[...]
    for m_ref, w_ref in zip(mix, ws):
        acc = acc + jnp.dot(m_ref[...], w_ref[...], preferred_element_type=F32)
    hnew_ref[...] = acc
    ms = jnp.mean(acc * acc, axis=-1, keepdims=True)
    hn = acc * lax.rsqrt(ms + NORM_EPS) * g_ref[...]
    hn_ref[...] = hn
    hi = hn.astype(BF16)
    lo = (hn - hi.astype(F32)).astype(BF16)
    wr = wr_ref[...]
    t = jnp.dot(hi, wr, preferred_element_type=F32)
    u = jnp.dot(lo, wr[:, :LANES], preferred_element_type=F32)
    lg_ref[...] = t[:, :LANES] + t[:, LANES:] + u + br_ref[...]


def _outproj(mixes, ws, h, g, wr, br, tm=256):
    n, d = h.shape
    n_in = len(mixes)
    in_specs = ([pl.BlockSpec((tm, m.shape[1]), lambda i: (i, 0)) for m in mixes]
                + [pl.BlockSpec(w.shape, lambda i: (0, 0)) for w in ws]
                + [pl.BlockSpec((tm, d), lambda i: (i, 0)), pl.BlockSpec((1, d), lambda i: (0, 0)),
                   pl.BlockSpec((d, 2 * LANES), lambda i: (0, 0)), pl.BlockSpec((1, LANES), lambda i: (0, 0))])
    return pl.pallas_call(
        functools.partial(_outproj_kernel, n_in=n_in),
        grid=(n // tm,),
        in_specs=in_specs,
        out_specs=[pl.BlockSpec((tm, d), lambda i: (i, 0)), pl.BlockSpec((tm, d), lambda i: (i, 0)),
                   pl.BlockSpec((tm, LANES), lambda i: (i, 0))],
        out_shape=[jax.ShapeDtypeStruct((n, d), F32), jax.ShapeDtypeStruct((n, d), F32),
                   jax.ShapeDtypeStruct((n, LANES), F32)],
        compiler_params=_cparams(("parallel",)),
        name="outproj_norm_router",
    )(*mixes, *ws, h, g.reshape(1, d), wr, br)


ROUTE_E1, ROUTE_E2, ROUTE_W1, ROUTE_W2, ROUTE_R1, ROUTE_R2 = range(6)


def _router_kernel(lg_ref, out_ref, cnt_ref, run_ref):
    @pl.when(pl.program_id(0) == 0)
    def _():
        run_ref[...] = jnp.zeros_like(run_ref)

    x = lg_ref[...]
    t = x.shape[0]
    lane = lax.broadcasted_iota(jnp.int32, (t, LANES), 1).astype(F32)
    far = float(LANES)
    gmask = lane < MOE_GROUPS
    gl = jnp.where(gmask, x, NEG_INF)
    gmax = jnp.max(gl, axis=-1, keepdims=True)
    g_top = jnp.min(jnp.where(gl == gmax, lane, far), axis=-1, keepdims=True)
    gsum = jnp.sum(jnp.where(gmask, jnp.exp(gl - gmax), 0.0), axis=-1, keepdims=True)
    g_prob = 1.0 / gsum
    lo = MOE_GROUPS + MOE_PER_GROUP * g_top
    emask = (lane >= lo) & (lane < lo + MOE_PER_GROUP)
    el = jnp.where(emask, x, NEG_INF)
    emax = jnp.max(el, axis=-1, keepdims=True)
    ee = jnp.where(emask, jnp.exp(el - emax), 0.0)
    ep = jnp.where(emask, ee / jnp.sum(ee, axis=-1, keepdims=True), -1.0)
    p1 = jnp.max(ep, axis=-1, keepdims=True)
    i1 = jnp.min(jnp.where(ep == p1, lane, far), axis=-1, keepdims=True)
    ep2 = jnp.where(lane == i1, -1.0, ep)
    p2 = jnp.max(ep2, axis=-1, keepdims=True)
    i2 = jnp.min(jnp.where(ep2 == p2, lane, far), axis=-1, keepdims=True)
    den = p1 + p2
    w1 = g_prob * p1 / den
    w2 = g_prob * p2 / den
    oh1 = jnp.where(lane == i1, 1.0, 0.0)
    oh2 = jnp.where(lane == i2, 1.0, 0.0)
    oh = oh1 + oh2
    row = lax.broadcasted_iota(jnp.int32, (t, t), 0)
    col = lax.broadcasted_iota(jnp.int32, (t, t), 1)
    strict = jnp.where(row > col, 1.0, 0.0).astype(BF16)
    before = jnp.dot(strict, oh.astype(BF16), preferred_element_type=F32) + run_ref[...]
    r1 = jnp.sum(oh1 * before, axis=-1, keepdims=True)
    r2 = jnp.sum(oh2 * before, axis=-1, keepdims=True)
    run_ref[...] = run_ref[...] + jnp.sum(oh, axis=0, keepdims=True)
    cnt_ref[...] = run_ref[...]
    packed = jnp.zeros((t, LANES), F32)
    for k, v in ((ROUTE_E1, i1 - MOE_GROUPS), (ROUTE_E2, i2 - MOE_GROUPS), (ROUTE_W1, w1), (ROUTE_W2, w2),
                 (ROUTE_R1, r1), (ROUTE_R2, r2)):
        packed = jnp.where(lane == k, v, packed)
    out_ref[...] = packed


def _router(logits, tt=ROUTER_TILE):
    n = logits.shape[0]
    return pl.pallas_call(
        _router_kernel,
        grid=(n // tt,),
        in_specs=[pl.BlockSpec((tt, LANES), lambda i: (i, 0))],
        out_specs=[pl.BlockSpec((tt, LANES), lambda i: (i, 0)), pl.BlockSpec((1, LANES), lambda i: (0, 0))],
        out_shape=[jax.ShapeDtypeStruct((n, LANES), F32), jax.ShapeDtypeStruct((1, LANES), F32)],
        scratch_shapes=[pltpu.VMEM((1, LANES), F32)],
        compiler_params=_cparams(("arbitrary",)),
        name="router_rank",
    )(logits)


SLOT_1, SLOT_2 = 0, 1
BLK_HALVES = 2


def _slot_kernel(rt_ref, cnt_ref, slot_ref, blk_ref):
    lane_row = lax.broadcasted_iota(jnp.int32, (1, LANES), 1)
    is_exp = (lane_row >= MOE_GROUPS) & (lane_row < MOE_GROUPS + MOE_EXPERTS)
    padded = jnp.where(is_exp, jnp.floor((cnt_ref[...] + (MOE_ROWS - 1)) * (1.0 / MOE_ROWS)) * MOE_ROWS, 0.0)
    r = lax.broadcasted_iota(jnp.int32, (LANES, LANES), 0)
    c = lax.broadcasted_iota(jnp.int32, (LANES, LANES), 1)
    incl = jnp.where(r <= c, 1.0, 0.0).astype(F32)
    pad_end = jnp.dot(jnp.broadcast_to(padded, (8, LANES)), incl, preferred_element_type=F32,
                      precision=HIGHEST)[0:1]
    pad_start = pad_end - padded
    rt = rt_ref[...]
    lane = lax.broadcasted_iota(jnp.int32, rt.shape, 1).astype(F32)

    def slot(e, rank):
        return jnp.sum(jnp.where(lane == e + MOE_GROUPS, pad_start, 0.0), axis=-1, keepdims=True) + rank

    s1 = slot(rt[:, ROUTE_E1:ROUTE_E1 + 1], rt[:, ROUTE_R1:ROUTE_R1 + 1])
    s2 = slot(rt[:, ROUTE_E2:ROUTE_E2 + 1], rt[:, ROUTE_R2:ROUTE_R2 + 1])
    slot_ref[...] = jnp.where(lane == SLOT_1, s1, jnp.where(lane == SLOT_2, s2, 0.0))

    end_col = jnp.broadcast_to(pad_end, (LANES, LANES)).T
    exp_col = (r >= MOE_GROUPS) & (r < MOE_GROUPS + MOE_EXPERTS)
    last = MOE_GROUPS + MOE_EXPERTS - 1
    total = pad_end[:, last:last + 1]

    def experts_done(start):
        return jnp.sum(jnp.where(exp_col & (end_col <= start), 1.0, 0.0), axis=0, keepdims=True)

    idle = experts_done(total - MOE_ROWS)
    rows = []
    for h in range(BLK_HALVES):
        start = (lane_row + h * LANES).astype(F32) * MOE_ROWS
        valid = start < total
        rows += [jnp.where(valid, experts_done(start), idle), jnp.where(valid, 1.0, 0.0)]
    tail = jnp.where(is_exp & (padded > 0.0), pad_end - MOE_ROWS, -1.0)
    tail = jnp.where(lane_row == MOE_GROUPS + MOE_EXPERTS, total, tail)
    blk_ref[...] = jnp.concatenate(rows + [tail, jnp.zeros((8 - 2 * BLK_HALVES - 1, LANES), F32)], axis=0)


def _slots(routed, counts, tt=ROUTER_TILE):
    n = routed.shape[0]
    return pl.pallas_call(
        _slot_kernel,
        grid=(n // tt,),
        in_specs=[pl.BlockSpec((tt, LANES), lambda i: (i, 0)), pl.BlockSpec((1, LANES), lambda i: (0, 0))],
        out_specs=[pl.BlockSpec((tt, LANES), lambda i: (i, 0)), pl.BlockSpec((8, LANES), lambda i: (0, 0))],
        out_shape=[jax.ShapeDtypeStruct((n, LANES), F32), jax.ShapeDtypeStruct((8, LANES), F32)],
        compiler_params=_cparams(("arbitrary",)),
        name="moe_slots",
    )(routed, counts)


def _moe_kernel(be_ref, bv_ref, x_ref, wg_ref, wu_ref, wd_ref, o_ref):
    i = pl.program_id(0)

    @pl.when(bv_ref[i] > 0)
    def _():
        x = x_ref[...].astype(BF16)
        step = 256
        for n0 in range(0, MOE_FF, step):
            gate = jnp.dot(x, wg_ref[:, n0:n0 + step].astype(BF16), preferred_element_type=F32)
            up = jnp.dot(x, wu_ref[:, n0:n0 + step].astype(BF16), preferred_element_type=F32)
            hid = (_silu(gate) * up).astype(BF16)
            part = jnp.dot(hid, wd_ref[n0:n0 + step, :].astype(BF16), preferred_element_type=F32)
            if n0 == 0:
                o_ref[...] = part
            else:
                o_ref[...] += part

    @pl.when(bv_ref[i] == 0)
    def _():
        o_ref[...] = jnp.zeros_like(o_ref)


def _moe_ffn(xs, blk_expert, blk_valid, layer, wg, wu, wd):
    cap, d = xs.shape
    nb = cap // MOE_ROWS
    ff = wg.shape[3]
    grid_spec = pltpu.PrefetchScalarGridSpec(
        num_scalar_prefetch=2,
        grid=(nb,),
        in_specs=[pl.BlockSpec((MOE_ROWS, d), lambda i, be, bv: (i * bv[i], 0)),
                  pl.BlockSpec((None, None, d, ff), lambda i, be, bv: (layer, be[i], 0, 0)),
                  pl.BlockSpec((None, None, d, ff), lambda i, be, bv: (layer, be[i], 0, 0)),
                  pl.BlockSpec((None, None, ff, d), lambda i, be, bv: (layer, be[i], 0, 0),
                               pipeline_mode=pl.Buffered(1))],
        out_specs=pl.BlockSpec((MOE_ROWS, d), lambda i, be, bv: (i, 0)),
    )
    return pl.pallas_call(
        _moe_kernel,
        grid_spec=grid_spec,
        out_shape=jax.ShapeDtypeStruct((cap, d), F32),
        compiler_params=_cparams(("arbitrary",)),
        name="moe_ffn",
    )(blk_expert, blk_valid, xs, wg, wu, wd)


DISPATCH_TILE = 256
ROW_DMA_UNROLL = 8


def _row_copy(src_ref, src_row, dst_ref, dst_row, sem):
    return pltpu.make_async_copy(src_ref.at[pl.ds(src_row, 1), :], dst_ref.at[pl.ds(dst_row, 1), :], sem)


def _dispatch_kernel(s1_ref, s2_ref, tail_ref, hn_ref, xs_ref, zero_ref, sem, zsem):
    t = hn_ref.shape[0]
    base = pl.program_id(0) * t

    @pl.when(pl.program_id(0) == 0)
    def _():
        zero_ref[...] = jnp.zeros_like(zero_ref)

        def fill(e):
            start = pl.multiple_of(jnp.maximum(tail_ref[e], 0), MOE_ROWS)
            return pltpu.make_async_copy(zero_ref, xs_ref.at[pl.ds(start, MOE_ROWS), :], zsem)

        for e in range(MOE_EXPERTS):
            @pl.when(tail_ref[e] >= 0)
            def _():
                fill(e).start()
        for e in range(MOE_EXPERTS):
            @pl.when(tail_ref[e] >= 0)
            def _():
                fill(e).wait()

        used = tail_ref[MOE_EXPERTS]
        idle = (xs_ref.shape[0] - used) // MOE_ROWS

        def idle_fill(j):
            start = pl.multiple_of(used + j * MOE_ROWS, MOE_ROWS)
            return pltpu.make_async_copy(zero_ref, xs_ref.at[pl.ds(start, MOE_ROWS), :], zsem)

        def idle_start(j, carry):
            idle_fill(j).start()
            return carry

        def idle_wait(j, carry):
            idle_fill(j).wait()
            return carry

        lax.fori_loop(0, idle, idle_start, 0)
        lax.fori_loop(0, idle, idle_wait, 0)

    def issue(r, carry):
        _row_copy(hn_ref, r, xs_ref, s1_ref[base + r], sem.at[0]).start()
        _row_copy(hn_ref, r, xs_ref, s2_ref[base + r], sem.at[1]).start()
        return carry

    lax.fori_loop(0, t, issue, 0, unroll=ROW_DMA_UNROLL)

    for k in range(2):
        pltpu.make_async_copy(hn_ref, xs_ref.at[pl.ds(0, t), :], sem.at[k]).wait()


def _dispatch(hn, slot1, slot2, tail, cap, tt=DISPATCH_TILE):
    n, d = hn.shape
    grid_spec = pltpu.PrefetchScalarGridSpec(
        num_scalar_prefetch=3,
        grid=(n // tt,),
        in_specs=[pl.BlockSpec((tt, d), lambda i, s1, s2, tl: (i, 0))],
        out_specs=pl.BlockSpec(memory_space=pl.ANY),
        scratch_shapes=[pltpu.VMEM((MOE_ROWS, d), F32), pltpu.SemaphoreType.DMA((2,)), pltpu.SemaphoreType.DMA(())],
    )
    return pl.pallas_call(
        _dispatch_kernel,
        grid_spec=grid_spec,
        out_shape=jax.ShapeDtypeStruct((cap, d), F32),
        compiler_params=_cparams(("arbitrary",)),
        name="moe_dispatch",
    )(slot1, slot2, tail, hn)


def _moe(hn, logits, layer, wg, wu, wd):
    n, d = hn.shape
    routed, counts = _router(logits)
    slots, blk = _slots(routed, counts)
    slot1 = slots[:, SLOT_1].astype(jnp.int32)
    slot2 = slots[:, SLOT_2].astype(jnp.int32)
    nb = -(-(2 * n + MOE_EXPERTS * (MOE_ROWS - 1)) // MOE_ROWS)
    assert nb <= BLK_HALVES * LANES
    cap = nb * MOE_ROWS
    blk_expert = blk[0:2 * BLK_HALVES:2].reshape(-1)[:nb].astype(jnp.int32)
    blk_valid = blk[1:2 * BLK_HALVES:2].reshape(-1)[:nb].astype(jnp.int32)
    tail = blk[2 * BLK_HALVES, MOE_GROUPS:MOE_GROUPS + MOE_EXPERTS + 1].astype(jnp.int32)
    xs = _dispatch(hn, slot1, slot2, tail, cap)
    yb = _moe_ffn(xs, blk_expert, blk_valid, layer, wg, wu, wd)
    return yb, slot1, slot2, routed


def _ple_kernel(s1_ref, s2_ref, h_ref, yb_ref, rt_ref, p_ref, gp_ref, wg_ref, wp_ref, gn_ref, hout_ref, nout_ref,
                ybuf_ref, sem):
    tm = h_ref.shape[0]
    i = pl.program_id(0)
    cur = i % 2

    def gather(step, buf):
        base = step * tm

        def one(r, carry):
            for k, s_ref in enumerate((s1_ref, s2_ref)):
                _row_copy(yb_ref, s_ref[base + r], ybuf_ref.at[buf, k], r, sem.at[buf, k]).start()
            return carry

        lax.fori_loop(0, tm, one, 0, unroll=ROW_DMA_UNROLL)

    @pl.when(i == 0)
    def _():
        gather(0, 0)

    @pl.when(i + 1 < pl.num_programs(0))
    def _():
        gather(i + 1, 1 - cur)

    for k in range(2):
        pltpu.make_async_copy(yb_ref.at[pl.ds(0, tm), :], ybuf_ref.at[cur, k], sem.at[cur, k]).wait()
    rt = rt_ref[...]
    moe = ybuf_ref[cur, 0] * rt[:, ROUTE_W1:ROUTE_W1 + 1] + ybuf_ref[cur, 1] * rt[:, ROUTE_W2:ROUTE_W2 + 1]
    h = h_ref[...] + moe
    ms = jnp.mean(h * h, axis=-1, keepdims=True)
    hn = (h * lax.rsqrt(ms + NORM_EPS) * gp_ref[...]).astype(BF16)
    gate = jax.nn.sigmoid(jnp.dot(hn, wg_ref[...], preferred_element_type=F32))
    emb = jnp.dot(p_ref[...].astype(BF16), wp_ref[...], preferred_element_type=F32)
    h = h + gate * emb
    hout_ref[...] = h
    ms = jnp.mean(h * h, axis=-1, keepdims=True)
    nout_ref[...] = (h * lax.rsqrt(ms + NORM_EPS) * gn_ref[...]).astype(nout_ref.dtype)


def _ple(h, yb, slot1, slot2, routed, p, layer, g_ple, w_gate, w_proj, g_next, next_dtype, tm=256):
    n, d = h.shape
    row = lambda i, s1, s2: (i, 0)
    fixed = lambda i, s1, s2: (0, 0)
    grid_spec = pltpu.PrefetchScalarGridSpec(
        num_scalar_prefetch=2,
        grid=(n // tm,),
        in_specs=[pl.BlockSpec((tm, d), row), pl.BlockSpec(memory_space=pl.ANY),
                  pl.BlockSpec((tm, LANES), row),
                  pl.BlockSpec((tm, PLE_DIM), lambda i, s1, s2: (layer * (n // tm) + i, 0)),
                  pl.BlockSpec((1, d), fixed), pl.BlockSpec((d, d), fixed), pl.BlockSpec((PLE_DIM, d), fixed),
                  pl.BlockSpec((1, d), fixed)],
        out_specs=[pl.BlockSpec((tm, d), row), pl.BlockSpec((tm, d), row)],
        scratch_shapes=[pltpu.VMEM((2, 2, tm, d), F32), pltpu.SemaphoreType.DMA((2, 2))],
    )
    return pl.pallas_call(
        _ple_kernel,
        grid_spec=grid_spec,
        out_shape=[jax.ShapeDtypeStruct((n, d), F32), jax.ShapeDtypeStruct((n, d), next_dtype)],
        compiler_params=_cparams(("arbitrary",)),
        name="ple_norm",
    )(slot1, slot2, h, yb, routed, p, g_ple.reshape(1, d), w_gate, w_proj, g_next.reshape(1, d))


def _compress_kernel(x_ref, pos_ref, w1_ref, w2_ref, cos_ref, sin_ref, o_ref):
    nchunk = x_ref.shape[0]
    width = 2 * NSA_GROUPS * HEAD_DIM
    lane = lax.broadcasted_iota(jnp.int32, (nchunk, LANES), 1)
    for kv in range(2):
        for g in range(NSA_GROUPS):
            acc = [jnp.zeros((nchunk, HEAD_DIM), F32) for _ in range(CMP_CHUNKS)]
            for half in range(CMP_CHUNKS):
                for j in range(CMP_STRIDE):
                    c0 = j * width + kv * NSA_GROUPS * HEAD_DIM + g * HEAD_DIM
                    jj = half * CMP_STRIDE + j
                    blk = (x_ref[:, c0:c0 + HEAD_DIM] + pos_ref[kv, jj:jj + 1, :]).astype(BF16)
                    acc[half] = acc[half] + jnp.dot(blk, w1_ref[kv, jj * HEAD_DIM:(jj + 1) * HEAD_DIM, :],
                                                    preferred_element_type=F32)
            pre = acc[0] + pltpu.roll(acc[1], nchunk - 1, 0)
            out = jnp.dot(_silu(pre).astype(BF16), w2_ref[kv], preferred_element_type=F32)
            if kv == 0:
                out = _rope_apply(out, cos_ref[...], sin_ref[...], lane)
            o_ref[kv * NSA_GROUPS + g] = out.astype(o_ref.dtype)


def _compress(kv_cmp, batch, seq, cmp_pos, cmp_w1, cmp_w2, cos_c, sin_c):
    nchunk = seq // CMP_STRIDE
    width = kv_cmp.shape[1]
    x = kv_cmp.reshape(batch * nchunk, CMP_STRIDE * width)
    return pl.pallas_call(
        _compress_kernel,
        grid=(batch,),
        in_specs=[pl.BlockSpec((nchunk, CMP_STRIDE * width), lambda b: (b, 0)),
                  pl.BlockSpec(cmp_pos.shape, lambda b: (0, 0, 0)),
                  pl.BlockSpec(cmp_w1.shape, lambda b: (0, 0, 0)),
                  pl.BlockSpec(cmp_w2.shape, lambda b: (0, 0, 0)),
                  pl.BlockSpec((nchunk, LANES), lambda b: (0, 0)),
                  pl.BlockSpec((nchunk, LANES), lambda b: (0, 0))],
        out_specs=pl.BlockSpec((None, 2 * NSA_GROUPS, nchunk, HEAD_DIM), lambda b: (b, 0, 0, 0)),
        out_shape=jax.ShapeDtypeStruct((batch, 2 * NSA_GROUPS, nchunk, HEAD_DIM), BF16),
        compiler_params=_cparams(("parallel",)),
        name="nsa_compress",
    )(x, cmp_pos, cmp_w1.astype(BF16), cmp_w2.astype(BF16), cos_c, sin_c)


NSA_TQ = 256
NSA_TK = 512
EXP2_SCALE = HEAD_DIM ** -0.5 * math.log2(math.e)


def _softmax_tile(s, m_old, l_old):
    m_new = jnp.maximum(m_old, jnp.max(s, axis=-1, keepdims=True))
    alpha = jnp.exp2(EXP2_SCALE * (m_old - m_new))
    p = jnp.exp2(EXP2_SCALE * (s - jnp.concatenate([m_new] * (s.shape[1] // LANES), axis=1)))
    l_new = alpha * l_old + jnp.sum(p, axis=-1, keepdims=True)
    return m_new, l_new, alpha, p


def _nsa_kernel(q_ref, kc_ref, vc_ref, ksel_ref, vsel_ref, kwin_ref, vwin_ref, gate_ref, pool_ref, negexp_ref,
                o_ref, acc_ref):
    tq = NSA_TQ
    rep = NSA_REP
    q0 = pl.program_id(2) * tq
    q = q_ref[...]
    qs = jnp.concatenate([q[:, r * HEAD_DIM:(r + 1) * HEAD_DIM] for r in range(rep)], axis=0)
    tpos = q0 + lax.broadcasted_iota(jnp.int32, (tq, 1), 0)
    heads = [slice(r * tq, (r + 1) * tq) for r in range(rep)]

    ncmp = kc_ref.shape[0]
    cend = lax.broadcasted_iota(jnp.int32, (1, ncmp), 1) * CMP_STRIDE + (CMP_CHUNKS * CMP_STRIDE - 1)
    cvalid = cend <= tpos
    cbias = jnp.where(cvalid, 0.0, NEG_INF).astype(F32)
    s_all = _nt_dot(qs, kc_ref[...])
    probs = []
    for hs in heads:
        s = s_all[hs] + cbias
        m = jnp.max(s, axis=-1, keepdims=True)
        p = jnp.where(cvalid, jnp.exp2(EXP2_SCALE * (s - m)), 0.0)
        l = jnp.sum(p, axis=-1, keepdims=True)
        probs.append(p * jnp.where(l > 0.0, 1.0 / l, 0.0))
    o_cmp = jnp.dot(jnp.concatenate(probs, axis=0).astype(BF16), vc_ref[...], preferred_element_type=F32)
    imp = probs[0]
    for p in probs[1:]:
        imp = imp + p
    p_sel = jnp.dot(imp, pool_ref[...], preferred_element_type=F32, precision=HIGHEST)

    nsel = ksel_ref.shape[0] // SEL_BLOCK
    blk = lax.broadcasted_iota(jnp.int32, (tq, LANES), 1)
    cur = tpos // SEL_BLOCK
    forced = (blk == 0) | (blk == cur) | (blk == cur - 1)
    valid = blk * SEL_BLOCK <= tpos
    score = jnp.where(forced, SEL_FORCE, jnp.where(valid, p_sel, NEG_INF))
    score = jnp.where(blk < nsel, score, 3.0 * NEG_INF)
    st = score.T
    sub = 8
    chunks = [st[c * sub:(c + 1) * sub, :] for c in range(nsel // sub)]
    cnt = [jnp.zeros((sub, tq), F32) for _ in chunks]
    jrow = lax.broadcasted_iota(jnp.int32, (sub, tq), 0)
    for i in range(nsel):
        ci, ii = divmod(i, sub)
        si = chunks[ci][ii:ii + 1, :]
        for c in range(len(chunks)):
            if c > ci:
                beats = si >= chunks[c]
            elif c < ci:
                beats = si > chunks[c]
            else:
                beats = jnp.where(jrow > ii, jnp.where(si >= chunks[c], 1.0, 0.0), jnp.where(si > chunks[c], 1.0, 0.0)) > 0.5
            cnt[c] = cnt[c] + jnp.where(beats, 1.0, 0.0)
    tpos_row = q0 + lax.broadcasted_iota(jnp.int32, (sub, tq), 1)
    dropped = []
    for c, cn in enumerate(cnt):
        keep_blk = (cn < min(SEL_TOPK, nsel)) & ((jrow + c * sub) * SEL_BLOCK <= tpos_row)
        dropped.append(jnp.where(keep_blk, 0.0, 1.0))
    dropped_t = jnp.concatenate(dropped + [jnp.zeros((LANES - nsel, tq), F32)], axis=0)
    dropped_q = dropped_t.T.astype(BF16)
    q_aug = jnp.concatenate([qs, jnp.concatenate([dropped_q] * rep, axis=0)], axis=1)

    tk = NSA_TK
    last = q0 // tk
    acc_ref[...] = jnp.zeros_like(acc_ref)

    def sel_tile(kt, carry, diagonal):
        m_i, l_i = carry
        k0 = pl.multiple_of(kt * tk, tk)
        k_aug = jnp.concatenate([ksel_ref[pl.ds(k0, tk), :], negexp_ref[pl.ds(k0, tk), :]], axis=1)
        v_t = vsel_ref[pl.ds(k0, tk), :]
        s_t = _nt_dot(q_aug, k_aug)
        if diagonal:
            kpos = k0 + lax.broadcasted_iota(jnp.int32, (1, tk), 1)
            causal = jnp.where(kpos <= tpos, 0.0, NEG_INF).astype(F32)
        m_out, l_out, alphas, ps = [], [], [], []
        for hs in heads:
            s = s_t[hs] + causal if diagonal else s_t[hs]
            m_new, l_new, alpha, p = _softmax_tile(s, m_i[hs], l_i[hs])
            m_out.append(m_new)
            l_out.append(l_new)
            alphas.append(alpha)
            ps.append(p.astype(BF16))
        pv = jnp.dot(jnp.concatenate(ps, axis=0), v_t, preferred_element_type=F32)
        acc_ref[...] = jnp.concatenate(alphas, axis=0) * acc_ref[...] + pv
        return jnp.concatenate(m_out, axis=0), jnp.concatenate(l_out, axis=0)

    init = (jnp.full((rep * tq, LANES), NEG_INF, F32), jnp.zeros((rep * tq, LANES), F32))
    carry = lax.fori_loop(0, last, functools.partial(sel_tile, diagonal=False), init)
    _, l_s = sel_tile(last, carry, True)
    o_sel = acc_ref[...] / l_s

    wk = WINDOW + tq
    w0 = pl.multiple_of(jnp.maximum(q0 - WINDOW, 0), tq)
    kw = kwin_ref[pl.ds(w0, wk), :]
    vw = vwin_ref[pl.ds(w0, wk), :]
    kpos = w0 + lax.broadcasted_iota(jnp.int32, (1, wk), 1)
    wbias = jnp.where((kpos <= tpos) & (kpos > tpos - WINDOW), 0.0, NEG_INF).astype(F32)
    sw_all = _nt_dot(qs, kw)
    pws, lws = [], []
    for hs in heads:
        sw = sw_all[hs] + wbias
        pw = jnp.exp2(EXP2_SCALE * (sw - jnp.max(sw, axis=-1, keepdims=True)))
        lws.append(jnp.sum(pw, axis=-1, keepdims=True))
        pws.append(pw.astype(BF16))
    o_win = jnp.dot(jnp.concatenate(pws, axis=0), vw, preferred_element_type=F32) / jnp.concatenate(lws, axis=0)

    gate = jax.nn.sigmoid(gate_ref[...])
    for r, rs in enumerate(heads):
        o = (gate[:, 3 * r:3 * r + 1] * o_cmp[rs] + gate[:, 3 * r + 1:3 * r + 2] * o_sel[rs]
             + gate[:, 3 * r + 2:3 * r + 3] * o_win[rs])
        o_ref[:, r * HEAD_DIM:(r + 1) * HEAD_DIM] = o.astype(o_ref.dtype)


def _nsa(roped, plain, cmp_kv, gates, batch, seq, pool, negexp):
    tq = NSA_TQ
    nq = seq // tq
    gw = NSA_REP * HEAD_DIM
    ksel_cb = NSA_HEADS
    kwin_cb = NSA_HEADS + NSA_GROUPS
    ncmp = seq // CMP_STRIDE
    seq_blk = lambda cb: pl.BlockSpec((seq, HEAD_DIM), functools.partial(lambda b, g, t, cb: (b, cb + g), cb=cb))
    return pl.pallas_call(
        _nsa_kernel,
        grid=(batch, NSA_GROUPS, nq),
        in_specs=[pl.BlockSpec((tq, gw), lambda b, g, t: (b * nq + t, g)),
                  pl.BlockSpec((None, None, ncmp, HEAD_DIM), lambda b, g, t: (b, g, 0, 0)),
                  pl.BlockSpec((None, None, ncmp, HEAD_DIM), lambda b, g, t: (b, NSA_GROUPS + g, 0, 0)),
                  seq_blk(ksel_cb), seq_blk(0), seq_blk(kwin_cb), seq_blk(NSA_GROUPS),
                  pl.BlockSpec((tq, LANES), lambda b, g, t: (b * nq + t, g)),
                  pl.BlockSpec(pool.shape, lambda b, g, t: (0, 0)),
                  pl.BlockSpec(negexp.shape, lambda b, g, t: (0, 0))],
        out_specs=pl.BlockSpec((tq, gw), lambda b, g, t: (b * nq + t, g)),
        out_shape=jax.ShapeDtypeStruct((batch * seq, NSA_HEADS * HEAD_DIM), BF16),
        scratch_shapes=[pltpu.VMEM((NSA_REP * tq, HEAD_DIM), F32)],
        compiler_params=_cparams(("parallel", "parallel", "arbitrary")),
        name="nsa_attention",
    )(roped, cmp_kv, cmp_kv, roped, plain, roped, plain, gates, pool, negexp)


DIFF_TQ = 512
DIFF_TK = 512


def _diff_kernel(q_ref, k_ref, v_ref, lam_ref, subln_ref, o_ref, acc_ref, *, lambda_init):
    tq, tk = DIFF_TQ, DIFF_TK
    q0 = pl.program_id(2) * tq
    last = q0 // tk
    q = q_ref[...]
    qm = (q[:, :HEAD_DIM], q[:, HEAD_DIM:])
    tpos = q0 + lax.broadcasted_iota(jnp.int32, (tq, 1), 0)
    maps = (slice(0, tq), slice(tq, 2 * tq))
    acc_ref[...] = jnp.zeros_like(acc_ref)

    def tile(kt, carry, diagonal):
        m_i, l_i = carry
        k0 = pl.multiple_of(kt * tk, tk)
        k = k_ref[pl.ds(k0, tk), :]
        v = v_ref[pl.ds(k0, tk), :]
        if diagonal:
            kpos = k0 + lax.broadcasted_iota(jnp.int32, (1, tk), 1)
            causal = jnp.where(kpos <= tpos, 0.0, NEG_INF).astype(F32)
        m_out, l_out, alphas, ps = [], [], [], []
        for i, ms in enumerate(maps):
            s = _nt_dot(qm[i], k[:, i * HEAD_DIM:(i + 1) * HEAD_DIM])
            if diagonal:
                s = s + causal
            m_new, l_new, alpha, p = _softmax_tile(s, m_i[ms], l_i[ms])
            m_out.append(m_new)
            l_out.append(l_new)
            alphas.append(jnp.concatenate([alpha] * (DIFF_V // LANES), axis=1))
            ps.append(p.astype(BF16))
        pv = jnp.dot(jnp.concatenate(ps, axis=0), v, preferred_element_type=F32)
        acc_ref[...] = jnp.concatenate(alphas, axis=0) * acc_ref[...] + pv
        return jnp.concatenate(m_out, axis=0), jnp.concatenate(l_out, axis=0)

    init = (jnp.full((2 * tq, LANES), NEG_INF, F32), jnp.zeros((2 * tq, LANES), F32))
    carry = lax.fori_loop(0, last, functools.partial(tile, diagonal=False), init)
    _, l_f = tile(last, carry, True)
    o = acc_ref[...] / jnp.concatenate([l_f] * (DIFF_V // LANES), axis=1)
    lam = lam_ref[...]
    lam_full = (jnp.exp(jnp.sum(lam[0:1] * lam[1:2], axis=-1, keepdims=True))
                - jnp.exp(jnp.sum(lam[2:3] * lam[3:4], axis=-1, keepdims=True)) + lambda_init)
    od = o[:tq] - lam_full * o[tq:]
    ms = jnp.mean(od * od, axis=-1, keepdims=True)
    o_ref[...] = (od * lax.rsqrt(ms + NORM_EPS) * subln_ref[...] * (1.0 - lambda_init)).astype(o_ref.dtype)


def _diff_attn(roped, plain, lam, subln, batch, seq, lambda_init):
    tq = DIFF_TQ
    nq = seq // tq
    pair = 2 * HEAD_DIM
    dq_cb = (NSA_HEADS + 2 * NSA_GROUPS) * HEAD_DIM // pair
    dk_cb = dq_cb + DIFF_HEADS
    dv_cb = 2 * NSA_GROUPS * HEAD_DIM // DIFF_V
    return pl.pallas_call(
        functools.partial(_diff_kernel, lambda_init=lambda_init),
        grid=(batch, DIFF_HEADS, nq),
        in_specs=[pl.BlockSpec((tq, pair), lambda b, h, t: (b * nq + t, dq_cb + h)),
                  pl.BlockSpec((seq, pair), lambda b, h, t: (b, dk_cb + h)),
                  pl.BlockSpec((seq, DIFF_V), lambda b, h, t: (b, dv_cb + h)),
                  pl.BlockSpec(lam.shape, lambda b, h, t: (0, 0)),
                  pl.BlockSpec((1, DIFF_V), lambda b, h, t: (0, 0))],
        out_specs=pl.BlockSpec((tq, DIFF_V), lambda b, h, t: (b * nq + t, h)),
        out_shape=jax.ShapeDtypeStruct((batch * seq, DIFF_HEADS * DIFF_V), BF16),
        scratch_shapes=[pltpu.VMEM((2 * tq, DIFF_V), F32)],
        compiler_params=_cparams(("parallel", "parallel", "arbitrary")),
        name="diff_attention",
    )(roped, roped, plain, lam, subln.reshape(1, DIFF_V))


def _rope_tables(pos):
    inv_freq = ROPE_THETA ** (-jnp.arange(ROPE_HALF, dtype=F32) / ROPE_HALF)
    ang = pos.astype(F32)[:, None] * inv_freq[None, :]
    cos, sin = jnp.cos(ang), jnp.sin(ang)
    rest = LANES - 2 * ROPE_HALF
    n = pos.shape[0]
    cosf = jnp.concatenate([cos, cos, jnp.ones((n, rest), F32)], axis=1)
    sinf = jnp.concatenate([-sin, sin, jnp.zeros((n, rest), F32)], axis=1)
    return cosf, sinf


def _router_weights(w_group, b_group, w_expert, b_expert):
    pad = LANES - MOE_GROUPS - MOE_EXPERTS
    wr = jnp.concatenate([w_group, w_expert, jnp.zeros((w_group.shape[0], pad), F32)], axis=1)
    br = jnp.concatenate([b_group, b_expert, jnp.zeros((pad,), F32)]).reshape(1, LANES)
    w_hi = wr.astype(BF16)
    w_lo = (wr - w_hi.astype(F32)).astype(BF16)
    return jnp.concatenate([w_hi, w_lo], axis=1), br


def _even_layer(h, hn, batch, seq, w_in, conv_w, conv_b, dt_bias, a_log, d_skip, gate_norm, sc_w, w_out,
                g_ffn, wr, br):
    i = SSD_INNER
    conv_ch = i + 2 * SSD_GROUPS * SSD_STATE
    o_dt = i + conv_ch
    o_sc = o_dt + SSD_HEADS
    w_main = jnp.concatenate([w_in[:, :o_dt], w_in[:, o_sc:]], axis=1).astype(BF16)
    w_dt = jnp.pad(w_in[:, o_dt:o_sc], ((0, 0), (0, LANES - SSD_HEADS))).astype(BF16)
    proj = _matmul(hn, w_main, F32, 1024, 1024)
    dt_raw = _matmul(hn, w_dt, F32, 512, LANES)
    mix = _even_core(proj, dt_raw, batch, seq, conv_w, conv_b, dt_bias, a_log, d_skip, gate_norm, sc_w)
    return _outproj([mix], [w_out.astype(BF16)], h, g_ffn, wr, br)


def _odd_layer(h, hn, batch, seq, w_in, cmp_pos, cmp_w1, cmp_w2, lam, subln, w_out, lambda_init, g_ffn, wr, br):
    hd, kvw = HEAD_DIM, NSA_GROUPS * HEAD_DIM
    o = [0]
    for wdt in (NSA_HEADS * hd,) + (kvw,) * 6 + (3 * NSA_HEADS,) + (DIFF_HEADS * 2 * hd,) * 2 + (DIFF_HEADS * DIFF_V,):
        o.append(o[-1] + wdt)
    col = lambda k: w_in[:, o[k]:o[k + 1]]
    q, k_cmp, v_cmp, k_sel, v_sel, k_win, v_win, gates, dq, dk, dv = [col(k) for k in range(11)]
    gcols = []
    for g in range(NSA_GROUPS):
        per = 3 * NSA_REP
        gcols.append(jnp.pad(gates[:, g * per:(g + 1) * per], ((0, 0), (0, LANES - per))))
    w_rope = jnp.concatenate([q, k_sel, k_win, dq, dk], axis=1).astype(BF16)
    w_plain = jnp.concatenate([v_sel, v_win, dv], axis=1).astype(BF16)
    w_cmp = jnp.concatenate([k_cmp, v_cmp], axis=1).astype(BF16)
    w_gates = jnp.concatenate(gcols, axis=1).astype(BF16)
    pos = jnp.arange(seq)
    roped = _matmul(hn, w_rope, BF16, 512, 512, rope=_rope_tables(pos), seq=seq)
    plain = _matmul(hn, w_plain, BF16, 512, 512)
    kv_cmp = _matmul(hn, w_cmp, F32, 512, 512)
    gate_lg = _matmul(hn, w_gates, F32, 512, 2 * LANES)
    ncmp = seq // CMP_STRIDE
    cmp_end = jnp.arange(ncmp) * CMP_STRIDE + CMP_CHUNKS * CMP_STRIDE - 1
    cos_c, sin_c = _rope_tables(cmp_end)
    cmp_kv = _compress(kv_cmp, batch, seq, cmp_pos, cmp_w1, cmp_w2, cos_c, sin_c)
    n_idx = jnp.arange(ncmp)[:, None]
    j_idx = jnp.arange(LANES)[None, :]
    per_sel = SEL_BLOCK // CMP_STRIDE
    pool = ((n_idx >= per_sel * j_idx - 1) & (n_idx <= per_sel * j_idx + per_sel - 1)
            & (n_idx < ncmp - 1) & (j_idx < seq // SEL_BLOCK)).astype(F32)
    negexp = jnp.where((jnp.arange(seq)[:, None] // SEL_BLOCK) == jnp.arange(LANES)[None, :], NEG_INF, 0.0).astype(BF16)
    o_nsa = _nsa(roped, plain, cmp_kv, gate_lg, batch, seq, pool, negexp)
    o_diff = _diff_attn(roped, plain, lam, subln, batch, seq, lambda_init)
    n_nsa = NSA_HEADS * hd
    w_o = w_out.astype(BF16)
    return _outproj([o_nsa, o_diff], [w_o[:n_nsa], w_o[n_nsa:]], h, g_ffn, wr, br)


def kernel(x, p, norm_mix, norm_ffn, norm_ple, norm_final, ev_w_in, ev_conv_w, ev_conv_b, ev_dt_bias, ev_a_log, ev_d_skip, ev_gate_norm, ev_sc_w, ev_w_out, od_w_in, od_cmp_pos, od_cmp_w1, od_cmp_w2, od_lambda, od_subln, od_w_out, moe_w_group, moe_b_group, moe_w_expert, moe_b_expert, moe_w_gate, moe_w_up, moe_w_down, ple_gate, ple_proj):
    batch, seq, d = x.shape
    n = batch * seq
    depth = p.shape[0]
    h = x.reshape(n, d)
    hn = _rmsnorm(h, norm_mix[0], BF16)
    for i in range(depth):
        j = i // 2
        wr, br = _router_weights(moe_w_group[i], moe_b_group[i], moe_w_expert[i], moe_b_expert[i])
        if i % 2 == 0:
            h, hn2, logits = _even_layer(h, hn, batch, seq, ev_w_in[j], ev_conv_w[j], ev_conv_b[j], ev_dt_bias[j],
                                         ev_a_log[j], ev_d_skip[j], ev_gate_norm[j], ev_sc_w[j], ev_w_out[j],
                                         norm_ffn[i], wr, br)
        else:
            lambda_init = 0.8 - 0.6 * math.exp(-0.3 * i)
            h, hn2, logits = _odd_layer(h, hn, batch, seq, od_w_in[j], od_cmp_pos[j], od_cmp_w1[j], od_cmp_w2[j],
                                        od_lambda[j], od_subln[j], od_w_out[j], lambda_init, norm_ffn[i], wr, br)
        yb, slot1, slot2, routed = _moe(hn2, logits, i, moe_w_gate, moe_w_up, moe_w_down)
        last = i == depth - 1
        g_next = norm_final if last else norm_mix[i + 1]
        h, hn = _ple(h, yb, slot1, slot2, routed, p.reshape(depth * n, PLE_DIM), i, norm_ple[i],
                     ple_gate[i].astype(BF16), ple_proj[i].astype(BF16), g_next, F32 if last else BF16)
    return hn.reshape(batch, seq, d)
```

```python
import functools
import math

import jax
import jax.numpy as jnp
from jax import lax
from jax.experimental import pallas as pl
from jax.experimental.pallas import tpu as pltpu

F32 = jnp.float32
BF16 = jnp.bfloat16
HIGHEST = lax.Precision.HIGHEST

LANES = 128
D_MODEL = 2048
NORM_EPS = 1e-6
ROPE_THETA = 500000.0
ROPE_HALF = 16
NEG_INF = -1e30

SSD_HEADS = 32
SSD_HEAD_DIM = 64
SSD_GROUPS = 8
SSD_STATE = 128
SSD_CHUNK = 128
SSD_INNER = 2048
CONV_HALO = 8

HEAD_DIM = 128
NSA_HEADS = 8
NSA_GROUPS = 2
NSA_REP = 4
CMP_STRIDE = 16
CMP_CHUNKS = 2
SEL_BLOCK = 64
SEL_TOPK = 16
SEL_FORCE = 1e4
WINDOW = 512
DIFF_HEADS = 4
DIFF_V = 256

MOE_GROUPS = 4
MOE_PER_GROUP = 8
MOE_EXPERTS = 32
MOE_FF = 1024
MOE_ROWS = 256
ROUTER_TILE = 512
PLE_DIM = 256

VMEM_LIMIT = 56 * 1024 * 1024


def _cparams(sem, vmem=VMEM_LIMIT):
    return pltpu.CompilerParams(dimension_semantics=sem, vmem_limit_bytes=vmem)


def _nt_dot(a, b):
    return lax.dot_general(a, b, (((1,), (1,)), ((), ())), preferred_element_type=F32)


def _silu(x):
    return x * jax.nn.sigmoid(x)


def _rmsnorm_kernel(x_ref, g_ref, o_ref):
    x = x_ref[...]
    ms = jnp.mean(x * x, axis=-1, keepdims=True)
    o_ref[...] = (x * lax.rsqrt(ms + NORM_EPS) * g_ref[...]).astype(o_ref.dtype)


def _rmsnorm(x, g, out_dtype, tm=512):
    n, d = x.shape
    return pl.pallas_call(
        _rmsnorm_kernel,
        grid=(n // tm,),
        in_specs=[pl.BlockSpec((tm, d), lambda i: (i, 0)), pl.BlockSpec((1, d), lambda i: (0, 0))],
        out_specs=pl.BlockSpec((tm, d), lambda i: (i, 0)),
        out_shape=jax.ShapeDtypeStruct((n, d), out_dtype),
        compiler_params=_cparams(("parallel",)),
        name="rmsnorm",
    )(x, g.reshape(1, d))


def _rope_apply(x, cosf, sinf, lane):
    swapped = jnp.where(lane < ROPE_HALF, pltpu.roll(x, LANES - ROPE_HALF, 1), pltpu.roll(x, ROPE_HALF, 1))
    return x * cosf + swapped * sinf


def _mm_kernel(a_ref, w_ref, o_ref):
    o_ref[...] = jnp.dot(a_ref[...], w_ref[...], preferred_element_type=F32).astype(o_ref.dtype)


def _mm_rope_kernel(a_ref, w_ref, cos_ref, sin_ref, o_ref):
    acc = jnp.dot(a_ref[...], w_ref[...], preferred_element_type=F32)
    tm, tn = acc.shape
    cosf = cos_ref[...]
    sinf = sin_ref[...]
    lane = lax.broadcasted_iota(jnp.int32, (tm, LANES), 1)
    for h in range(tn // LANES):
        sl = slice(h * LANES, (h + 1) * LANES)
        o_ref[:, sl] = _rope_apply(acc[:, sl], cosf, sinf, lane).astype(o_ref.dtype)


def _matmul(a, w, out_dtype, tm, tn, rope=None, seq=None):
    m, k = a.shape
    n = w.shape[1]
    in_specs = [pl.BlockSpec((tm, k), lambda i, j: (i, 0)), pl.BlockSpec((k, tn), lambda i, j: (0, j))]
    args = [a, w]
    body = _mm_kernel
    if rope is not None:
        per_seq = seq // tm
        in_specs += [pl.BlockSpec((tm, LANES), lambda i, j: (i % per_seq, 0))] * 2
        args += list(rope)
        body = _mm_rope_kernel
    return pl.pallas_call(
        body,
        grid=(m // tm, n // tn),
        in_specs=in_specs,
        out_specs=pl.BlockSpec((tm, tn), lambda i, j: (i, j)),
        out_shape=jax.ShapeDtypeStruct((m, n), out_dtype),
        compiler_params=_cparams(("parallel", "parallel")),
        name="proj_rope" if rope is not None else "proj",
    )(*args)


def _even_core_kernel(z_ref, xs_ref, bc_ref, scb_ref, scc_ref, sch_ref,
                      xs_h_ref, bc_h_ref, scc_h_ref, sch_h_ref, hn_ref, wdt_ref,
                      cwx_ref, cwbc_ref, cbx_ref, cbbc_ref, dtb_ref, alog_ref, dskip_ref, gnorm_ref, scw_ref,
                      o_ref, state_ref, xsc_ref, bcc_ref, y_ref):
    q = SSD_CHUNK
    first = pl.program_id(1) == 0
    keep = jnp.where(first, 0.0, 1.0).astype(F32)

    @pl.when(first)
    def _():
        state_ref[...] = jnp.zeros_like(state_ref)

    strip = 512

    def causal_conv(load_main, load_halo, w_ref, width, finish):
        for c0 in range(0, SSD_INNER, strip):
            cs = slice(c0, c0 + strip)
            ext = jnp.concatenate([load_halo(cs) * keep, load_main(cs)], axis=0)
            acc = w_ref[width - 1:width, cs] * ext[CONV_HALO:]
            for s in range(1, width):
                acc = acc + w_ref[width - 1 - s:width - s, cs] * pltpu.roll(ext, s, 0)[CONV_HALO:]
            finish(cs, acc)

    def fin_xs(cs, acc):
        xsc_ref[:, cs] = _silu(acc + cbx_ref[:, cs])

    def fin_bc(cs, acc):
        bcc_ref[:, cs] = _silu(acc + cbbc_ref[:, cs])

    causal_conv(lambda cs: xs_ref[:, cs], lambda cs: xs_h_ref[:, cs], cwx_ref, 4, fin_xs)
    causal_conv(lambda cs: bc_ref[:, cs], lambda cs: bc_h_ref[:, cs], cwbc_ref, 4, fin_bc)

    def fin_sc(cs, acc):
        o_ref[:, SSD_INNER + cs.start:SSD_INNER + cs.stop] = (scb_ref[:, cs] * acc).astype(o_ref.dtype)

    causal_conv(lambda cs: scc_ref[:, cs] * sch_ref[:, cs], lambda cs: scc_h_ref[:, cs] * sch_h_ref[:, cs],
                scw_ref, 3, fin_sc)

    dt_raw = jnp.dot(hn_ref[...], wdt_ref[...], preferred_element_type=F32)
    dt = jax.nn.softplus(dt_raw + dtb_ref[...])
    d_a = dt * (-jnp.exp(alog_ref[...]))
    row = lax.broadcasted_iota(jnp.int32, (q, q), 0)
    col = lax.broadcasted_iota(jnp.int32, (q, q), 1)
    causal = row >= col
    tril = jnp.where(causal, 1.0, 0.0).astype(F32)
    a_cum = jnp.dot(tril, d_a, preferred_element_type=F32, precision=HIGHEST)
    a_last = a_cum[q - 1:q, :]
    decay_end = jnp.exp(a_last - a_cum)
    chunk_decay = jnp.exp(a_last)
    exp_acum = jnp.exp(a_cum)
    a_cum_t = a_cum.T
    lane = lax.broadcasted_iota(jnp.int32, (q, LANES), 1)
    lo = lane < SSD_HEAD_DIM
    lane_row = lax.broadcasted_iota(jnp.int32, (1, LANES), 1)
    lo_row = lane_row < SSD_HEAD_DIM

    def pair_cols(mat, h0):
        return jnp.where(lo, mat[:, h0:h0 + 1], mat[:, h0 + 1:h0 + 2])

    for g in range(SSD_GROUPS):
        gs = slice(g * SSD_STATE, (g + 1) * SSD_STATE)
        b_g = bcc_ref[:, gs]
        c_g = bcc_ref[:, SSD_GROUPS * SSD_STATE + g * SSD_STATE:SSD_GROUPS * SSD_STATE + (g + 1) * SSD_STATE]
        b_gt = b_g.T.astype(BF16)
        c_gb = c_g.astype(BF16)
        cb = jnp.dot(c_gb, b_gt, preferred_element_type=F32)
        for hp in range(2):
            pr = g * 2 + hp
            h0 = 2 * pr
            ps = slice(pr * LANES, (pr + 1) * LANES)
            xp = xsc_ref[:, ps]
            xdt = xp * pair_cols(dt, h0)
            mats = []
            for h in (h0, h0 + 1):
                seg = a_cum[:, h:h + 1] - a_cum_t[h:h + 1, :]
                dec = jnp.where(causal, jnp.exp(jnp.minimum(seg, 0.0)), 0.0)
                mats.append((cb * dec).astype(BF16))
            lhs = jnp.concatenate(mats, axis=1)
            rhs = jnp.concatenate([jnp.where(lo, xdt, 0.0), jnp.where(lo, 0.0, xdt)], axis=0).astype(BF16)
            y = jnp.dot(lhs, rhs, preferred_element_type=F32)
            st = state_ref[pr]
            y = y + jnp.dot(c_gb, st.astype(BF16), preferred_element_type=F32) * pair_cols(exp_acum, h0)
            xw = (xdt * pair_cols(decay_end, h0)).astype(BF16)
            cd = jnp.where(lo_row, chunk_decay[:, h0:h0 + 1], chunk_decay[:, h0 + 1:h0 + 2])
            state_ref[pr] = st * cd + jnp.dot(b_gt, xw, preferred_element_type=F32)
            y_ref[:, ps] = y + dskip_ref[:, ps] * xp

    gw = SSD_INNER // SSD_GROUPS
    for g in range(SSD_GROUPS):
        cs = slice(g * gw, (g + 1) * gw)
        yg = y_ref[:, cs] * _silu(z_ref[:, cs])
        ms = jnp.mean(yg * yg, axis=-1, keepdims=True)
        o_ref[:, cs] = (yg * lax.rsqrt(ms + NORM_EPS) * gnorm_ref[:, cs]).astype(o_ref.dtype)


def _even_core(proj, hn, w_dt, batch, seq, conv_w, conv_b, dt_bias, a_log, d_skip, gate_norm, sc_w):
    q = SSD_CHUNK
    nc = seq // q
    w = SSD_INNER
    hb = q // CONV_HALO

    def main(cb):
        return pl.BlockSpec((q, w), lambda b, c: (b * nc + c, cb))

    def halo(cb):
        return pl.BlockSpec((CONV_HALO, w), lambda b, c: (jnp.maximum((b * nc + c) * hb - 1, 0), cb))

    def full(shape):
        return pl.BlockSpec(shape, lambda b, c: (0, 0))

    pad = LANES - SSD_HEADS
    dtb = jnp.pad(dt_bias, (0, pad)).reshape(1, LANES)
    alog = jnp.pad(a_log, (0, pad)).reshape(1, LANES)
    dskip = jnp.repeat(d_skip, SSD_HEAD_DIM).reshape(1, w)
    in_specs = [main(0), main(1), main(2), main(3), main(4), main(5),
                halo(1), halo(2), halo(4), halo(5),
                pl.BlockSpec((q, hn.shape[1]), lambda b, c: (b * nc + c, 0)), full(w_dt.shape),
                full((4, w)), full((4, w)), full((1, w)), full((1, w)),
                full((1, LANES)), full((1, LANES)), full((1, w)), full((1, w)), full((3, w))]
    return pl.pallas_call(
        _even_core_kernel,
        grid=(batch, nc),
        in_specs=in_specs,
        out_specs=pl.BlockSpec((q, 2 * w), lambda b, c: (b * nc + c, 0)),
        out_shape=jax.ShapeDtypeStruct((batch * seq, 2 * w), BF16),
        scratch_shapes=[pltpu.VMEM((SSD_HEADS // 2, SSD_STATE, LANES), F32),
                        pltpu.VMEM((q, w), F32), pltpu.VMEM((q, w), F32), pltpu.VMEM((q, w), F32)],
        compiler_params=_cparams(("parallel", "arbitrary")),
        name="ssd_conv_core",
    )(proj, proj, proj, proj, proj, proj, proj, proj, proj, proj, hn, w_dt,
      conv_w[:, :w], conv_w[:, w:], conv_b[:w].reshape(1, w), conv_b[w:].reshape(1, w),
      dtb, alog, dskip, gate_norm.reshape(1, w), sc_w)


def _outproj_kernel(*refs, n_in):
    mix = refs[:n_in]
    ws = refs[n_in:2 * n_in]
    h_ref, g_ref, wr_ref, br_ref, hnew_ref, hn_ref, lg_ref = refs[2 * n_in:]
    acc = h_ref[...]
    for m_ref, w_ref in zip(mix, ws):
        acc = acc + jnp.dot(m_ref[...], w_ref[...], preferred_element_type=F32)
    hnew_ref[...] = acc
    ms = jnp.mean(acc * acc, axis=-1, keepdims=True)
    hn = acc * lax.rsqrt(ms + NORM_EPS) * g_ref[...]
    hn_ref[...] = hn
    hi = hn.astype(BF16)
    lo = (hn - hi.astype(F32)).astype(BF16)
    wr = wr_ref[...]
    t = jnp.dot(hi, wr, preferred_element_type=F32)
    u = jnp.dot(lo, wr[:, :LANES], preferred_element_type=F32)
    lg_ref[...] = t[:, :LANES] + t[:, LANES:] + u + br_ref[...]


def _outproj(mixes, ws, h, g, wr, br, tm=256):
    n, d = h.shape
    n_in = len(mixes)
    in_specs = ([pl.BlockSpec((tm, m.shape[1]), lambda i: (i, 0)) for m in mixes]
                + [pl.BlockSpec(w.shape, lambda i: (0, 0)) for w in ws]
                + [pl.BlockSpec((tm, d), lambda i: (i, 0)), pl.BlockSpec((1, d), lambda i: (0, 0)),
                   pl.BlockSpec((d, 2 * LANES), lambda i: (0, 0)), pl.BlockSpec((1, LANES), lambda i: (0, 0))])
    return pl.pallas_call(
        functools.partial(_outproj_kernel, n_in=n_in),
        grid=(n // tm,),
        in_specs=in_specs,
        out_specs=[pl.BlockSpec((tm, d), lambda i: (i, 0)), pl.BlockSpec((tm, d), lambda i: (i, 0)),
                   pl.BlockSpec((tm, LANES), lambda i: (i, 0))],
        out_shape=[jax.ShapeDtypeStruct((n, d), F32), jax.ShapeDtypeStruct((n, d), F32),
                   jax.ShapeDtypeStruct((n, LANES), F32)],
        compiler_params=_cparams(("parallel",)),
        name="outproj_norm_router",
    )(*mixes, *ws, h, g.reshape(1, d), wr, br)


ROUTE_E1, ROUTE_E2, ROUTE_W1, ROUTE_W2, ROUTE_R1, ROUTE_R2 = range(6)


def _router_kernel(lg_ref, out_ref, cnt_ref, run_ref):
    @pl.when(pl.program_id(0) == 0)
    def _():
        run_ref[...] = jnp.zeros_like(run_ref)

    x = lg_ref[...]
    t = x.shape[0]
    lane = lax.broadcasted_iota(jnp.int32, (t, LANES), 1).astype(F32)
    far = float(LANES)
    gmask = lane < MOE_GROUPS
    gl = jnp.where(gmask, x, NEG_INF)
    gmax = jnp.max(gl, axis=-1, keepdims=True)
    g_top = jnp.min(jnp.where(gl == gmax, lane, far), axis=-1, keepdims=True)
    gsum = jnp.sum(jnp.where(gmask, jnp.exp(gl - gmax), 0.0), axis=-1, keepdims=True)
    g_prob = 1.0 / gsum
    lo = MOE_GROUPS + MOE_PER_GROUP * g_top
    emask = (lane >= lo) & (lane < lo + MOE_PER_GROUP)
    el = jnp.where(emask, x, NEG_INF)
    emax = jnp.max(el, axis=-1, keepdims=True)
    ee = jnp.where(emask, jnp.exp(el - emax), 0.0)
    ep = jnp.where(emask, ee / jnp.sum(ee, axis=-1, keepdims=True), -1.0)
    p1 = jnp.max(ep, axis=-1, keepdims=True)
    i1 = jnp.min(jnp.where(ep == p1, lane, far), axis=-1, keepdims=True)
    ep2 = jnp.where(lane == i1, -1.0, ep)
    p2 = jnp.max(ep2, axis=-1, keepdims=True)
    i2 = jnp.min(jnp.where(ep2 == p2, lane, far), axis=-1, keepdims=True)
    den = p1 + p2
    w1 = g_prob * p1 / den
    w2 = g_prob * p2 / den
    oh1 = jnp.where(lane == i1, 1.0, 0.0)
    oh2 = jnp.where(lane == i2, 1.0, 0.0)
    oh = oh1 + oh2
    row = lax.broadcasted_iota(jnp.int32, (t, t), 0)
    col = lax.broadcasted_iota(jnp.int32, (t, t), 1)
    strict = jnp.where(row > col, 1.0, 0.0).astype(BF16)
    before = jnp.dot(strict, oh.astype(BF16), preferred_element_type=F32) + run_ref[...]
    r1 = jnp.sum(oh1 * before, axis=-1, keepdims=True)
    r2 = jnp.sum(oh2 * before, axis=-1, keepdims=True)
    run_ref[...] = run_ref[...] + jnp.sum(oh, axis=0, keepdims=True)
    cnt_ref[...] = run_ref[...]
    packed = jnp.zeros((t, LANES), F32)
    for k, v in ((ROUTE_E1, i1 - MOE_GROUPS), (ROUTE_E2, i2 - MOE_GROUPS), (ROUTE_W1, w1), (ROUTE_W2, w2),
                 (ROUTE_R1, r1), (ROUTE_R2, r2)):
        packed = jnp.where(lane == k, v, packed)
    out_ref[...] = packed


def _router(logits, tt=ROUTER_TILE):
    n = logits.shape[0]
    return pl.pallas_call(
        _router_kernel,
        grid=(n // tt,),
        in_specs=[pl.BlockSpec((tt, LANES), lambda i: (i, 0))],
        out_specs=[pl.BlockSpec((tt, LANES), lambda i: (i, 0)), pl.BlockSpec((1, LANES), lambda i: (0, 0))],
        out_shape=[jax.ShapeDtypeStruct((n, LANES), F32), jax.ShapeDtypeStruct((1, LANES), F32)],
        scratch_shapes=[pltpu.VMEM((1, LANES), F32)],
        compiler_params=_cparams(("arbitrary",)),
        name="router_rank",
    )(logits)


SLOT_1, SLOT_2 = 0, 1
BLK_HALVES = 2


def _slot_kernel(rt_ref, cnt_ref, slot_ref, blk_ref):
    lane_row = lax.broadcasted_iota(jnp.int32, (1, LANES), 1)
    is_exp = (lane_row >= MOE_GROUPS) & (lane_row < MOE_GROUPS + MOE_EXPERTS)
    padded = jnp.where(is_exp, jnp.floor((cnt_ref[...] + (MOE_ROWS - 1)) * (1.0 / MOE_ROWS)) * MOE_ROWS, 0.0)
    r = lax.broadcasted_iota(jnp.int32, (LANES, LANES), 0)
    c = lax.broadcasted_iota(jnp.int32, (LANES, LANES), 1)
    incl = jnp.where(r <= c, 1.0, 0.0).astype(F32)
    pad_end = jnp.dot(jnp.broadcast_to(padded, (8, LANES)), incl, preferred_element_type=F32,
                      precision=HIGHEST)[0:1]
    pad_start = pad_end - padded
    rt = rt_ref[...]
    lane = lax.broadcasted_iota(jnp.int32, rt.shape, 1).astype(F32)

    def slot(e, rank):
        return jnp.sum(jnp.where(lane == e + MOE_GROUPS, pad_start, 0.0), axis=-1, keepdims=True) + rank

    s1 = slot(rt[:, ROUTE_E1:ROUTE_E1 + 1], rt[:, ROUTE_R1:ROUTE_R1 + 1])
    s2 = slot(rt[:, ROUTE_E2:ROUTE_E2 + 1], rt[:, ROUTE_R2:ROUTE_R2 + 1])
    slot_ref[...] = jnp.where(lane == SLOT_1, s1, jnp.where(lane == SLOT_2, s2, 0.0))

    end_col = jnp.broadcast_to(pad_end, (LANES, LANES)).T
    exp_col = (r >= MOE_GROUPS) & (r < MOE_GROUPS + MOE_EXPERTS)
    last = MOE_GROUPS + MOE_EXPERTS - 1
    total = pad_end[:, last:last + 1]

    def experts_done(start):
        return jnp.sum(jnp.where(exp_col & (end_col <= start), 1.0, 0.0), axis=0, keepdims=True)

    idle = experts_done(total - MOE_ROWS)
    rows = []
    for h in range(BLK_HALVES):
        start = (lane_row + h * LANES).astype(F32) * MOE_ROWS
        valid = start < total
        rows += [jnp.where(valid, experts_done(start), idle), jnp.where(valid, 1.0, 0.0)]
    tail = jnp.where(is_exp & (padded > 0.0), pad_end - MOE_ROWS, -1.0)
    tail = jnp.where(lane_row == MOE_GROUPS + MOE_EXPERTS, total, tail)
    blk_ref[...] = jnp.concatenate(rows + [tail, jnp.zeros((8 - 2 * BLK_HALVES - 1, LANES), F32)], axis=0)


def _slots(routed, counts, tt=ROUTER_TILE):
    n = routed.shape[0]
    return pl.pallas_call(
        _slot_kernel,
        grid=(n // tt,),
        in_specs=[pl.BlockSpec((tt, LANES), lambda i: (i, 0)), pl.BlockSpec((1, LANES), lambda i: (0, 0))],
        out_specs=[pl.BlockSpec((tt, LANES), lambda i: (i, 0)), pl.BlockSpec((8, LANES), lambda i: (0, 0))],
        out_shape=[jax.ShapeDtypeStruct((n, LANES), F32), jax.ShapeDtypeStruct((8, LANES), F32)],
        compiler_params=_cparams(("arbitrary",)),
        name="moe_slots",
    )(routed, counts)


def _moe_kernel(be_ref, bv_ref, x_ref, wg_ref, wu_ref, wd_ref, o_ref):
    i = pl.program_id(0)

    @pl.when(bv_ref[i] > 0)
    def _():
        x = x_ref[...].astype(BF16)
        step = 256
        for n0 in range(0, MOE_FF, step):
            gate = jnp.dot(x, wg_ref[:, n0:n0 + step].astype(BF16), preferred_element_type=F32)
            up = jnp.dot(x, wu_ref[:, n0:n0 + step].astype(BF16), preferred_element_type=F32)
            hid = (_silu(gate) * up).astype(BF16)
            part = jnp.dot(hid, wd_ref[n0:n0 + step, :].astype(BF16), preferred_element_type=F32)
            if n0 == 0:
                o_ref[...] = part
            else:
                o_ref[...] += part

    @pl.when(bv_ref[i] == 0)
    def _():
        o_ref[...] = jnp.zeros_like(o_ref)


def _moe_ffn(xs, blk_expert, blk_valid, layer, wg, wu, wd):
    cap, d = xs.shape
    nb = cap // MOE_ROWS
    ff = wg.shape[3]
    grid_spec = pltpu.PrefetchScalarGridSpec(
        num_scalar_prefetch=2,
        grid=(nb,),
        in_specs=[pl.BlockSpec((MOE_ROWS, d), lambda i, be, bv: (i * bv[i], 0)),
                  pl.BlockSpec((None, None, d, ff), lambda i, be, bv: (layer, be[i], 0, 0)),
                  pl.BlockSpec((None, None, d, ff), lambda i, be, bv: (layer, be[i], 0, 0)),
                  pl.BlockSpec((None, None, ff, d), lambda i, be, bv: (layer, be[i], 0, 0),
                               pipeline_mode=pl.Buffered(1))],
        out_specs=pl.BlockSpec((MOE_ROWS, d), lambda i, be, bv: (i, 0)),
    )
    return pl.pallas_call(
        _moe_kernel,
        grid_spec=grid_spec,
        out_shape=jax.ShapeDtypeStruct((cap, d), F32),
        compiler_params=_cparams(("arbitrary",)),
        name="moe_ffn",
    )(blk_expert, blk_valid, xs, wg, wu, wd)


DISPATCH_TILE = 256
ROW_DMA_UNROLL = 8


def _row_copy(src_ref, src_row, dst_ref, dst_row, sem):
    return pltpu.make_async_copy(src_ref.at[pl.ds(src_row, 1), :], dst_ref.at[pl.ds(dst_row, 1), :], sem)


def _dispatch_kernel(s1_ref, s2_ref, tail_ref, hn_ref, xs_ref, zero_ref, sem, zsem):
    t = hn_ref.shape[0]
    base = pl.program_id(0) * t

    @pl.when(pl.program_id(0) == 0)
    def _():
        zero_ref[...] = jnp.zeros_like(zero_ref)

        def fill(e):
            start = pl.multiple_of(jnp.maximum(tail_ref[e], 0), MOE_ROWS)
            return pltpu.make_async_copy(zero_ref, xs_ref.at[pl.ds(start, MOE_ROWS), :], zsem)

        for e in range(MOE_EXPERTS):
            @pl.when(tail_ref[e] >= 0)
            def _():
                fill(e).start()
        for e in range(MOE_EXPERTS):
            @pl.when(tail_ref[e] >= 0)
            def _():
                fill(e).wait()

        used = tail_ref[MOE_EXPERTS]
        idle = (xs_ref.shape[0] - used) // MOE_ROWS

        def idle_fill(j):
            start = pl.multiple_of(used + j * MOE_ROWS, MOE_ROWS)
            return pltpu.make_async_copy(zero_ref, xs_ref.at[pl.ds(start, MOE_ROWS), :], zsem)

        def idle_start(j, carry):
            idle_fill(j).start()
            return carry

        def idle_wait(j, carry):
            idle_fill(j).wait()
            return carry

        lax.fori_loop(0, idle, idle_start, 0)
        lax.fori_loop(0, idle, idle_wait, 0)

    def issue(r, carry):
        _row_copy(hn_ref, r, xs_ref, s1_ref[base + r], sem.at[0]).start()
        _row_copy(hn_ref, r, xs_ref, s2_ref[base + r], sem.at[1]).start()
        return carry

    lax.fori_loop(0, t, issue, 0, unroll=ROW_DMA_UNROLL)

    for k in range(2):
        pltpu.make_async_copy(hn_ref, xs_ref.at[pl.ds(0, t), :], sem.at[k]).wait()


def _dispatch(hn, slot1, slot2, tail, cap, tt=DISPATCH_TILE):
    n, d = hn.shape
    grid_spec = pltpu.PrefetchScalarGridSpec(
        num_scalar_prefetch=3,
        grid=(n // tt,),
        in_specs=[pl.BlockSpec((tt, d), lambda i, s1, s2, tl: (i, 0))],
        out_specs=pl.BlockSpec(memory_space=pl.ANY),
        scratch_shapes=[pltpu.VMEM((MOE_ROWS, d), F32), pltpu.SemaphoreType.DMA((2,)), pltpu.SemaphoreType.DMA(())],
    )
    return pl.pallas_call(
        _dispatch_kernel,
        grid_spec=grid_spec,
        out_shape=jax.ShapeDtypeStruct((cap, d), F32),
        compiler_params=_cparams(("arbitrary",)),
        name="moe_dispatch",
    )(slot1, slot2, tail, hn)


def _moe(hn, logits, layer, wg, wu, wd):
    n, d = hn.shape
    routed, counts = _router(logits)
    slots, blk = _slots(routed, counts)
    slot1 = slots[:, SLOT_1].astype(jnp.int32)
    slot2 = slots[:, SLOT_2].astype(jnp.int32)
    nb = -(-(2 * n + MOE_EXPERTS * (MOE_ROWS - 1)) // MOE_ROWS)
    assert nb <= BLK_HALVES * LANES
    cap = nb * MOE_ROWS
    blk_expert = blk[0:2 * BLK_HALVES:2].reshape(-1)[:nb].astype(jnp.int32)
    blk_valid = blk[1:2 * BLK_HALVES:2].reshape(-1)[:nb].astype(jnp.int32)
    tail = blk[2 * BLK_HALVES, MOE_GROUPS:MOE_GROUPS + MOE_EXPERTS + 1].astype(jnp.int32)
    xs = _dispatch(hn, slot1, slot2, tail, cap)
    yb = _moe_ffn(xs, blk_expert, blk_valid, layer, wg, wu, wd)
    return yb, slot1, slot2, routed


def _ple_kernel(s1_ref, s2_ref, h_ref, yb_ref, rt_ref, p_ref, gp_ref, wg_ref, wp_ref, gn_ref, hout_ref, nout_ref,
                ybuf_ref, sem):
    tm = h_ref.shape[0]
    i = pl.program_id(0)
    cur = i % 2

    def gather(step, buf):
        base = step * tm

        def one(r, carry):
            for k, s_ref in enumerate((s1_ref, s2_ref)):
                _row_copy(yb_ref, s_ref[base + r], ybuf_ref.at[buf, k], r, sem.at[buf, k]).start()
            return carry

        lax.fori_loop(0, tm, one, 0, unroll=ROW_DMA_UNROLL)

    @pl.when(i == 0)
    def _():
        gather(0, 0)

    @pl.when(i + 1 < pl.num_programs(0))
    def _():
        gather(i + 1, 1 - cur)

    for k in range(2):
        pltpu.make_async_copy(yb_ref.at[pl.ds(0, tm), :], ybuf_ref.at[cur, k], sem.at[cur, k]).wait()
    rt = rt_ref[...]
    moe = ybuf_ref[cur, 0] * rt[:, ROUTE_W1:ROUTE_W1 + 1] + ybuf_ref[cur, 1] * rt[:, ROUTE_W2:ROUTE_W2 + 1]
    h = h_ref[...] + moe
    ms = jnp.mean(h * h, axis=-1, keepdims=True)
    hn = (h * lax.rsqrt(ms + NORM_EPS) * gp_ref[...]).astype(BF16)
    gate = jax.nn.sigmoid(jnp.dot(hn, wg_ref[...], preferred_element_type=F32))
    emb = jnp.dot(p_ref[...].astype(BF16), wp_ref[...], preferred_element_type=F32)
    h = h + gate * emb
    hout_ref[...] = h
    ms = jnp.mean(h * h, axis=-1, keepdims=True)
    nout_ref[...] = (h * lax.rsqrt(ms + NORM_EPS) * gn_ref[...]).astype(nout_ref.dtype)


def _ple(h, yb, slot1, slot2, routed, p, layer, g_ple, w_gate, w_proj, g_next, next_dtype, tm=256):
    n, d = h.shape
    row = lambda i, s1, s2: (i, 0)
    fixed = lambda i, s1, s2: (0, 0)
    grid_spec = pltpu.PrefetchScalarGridSpec(
        num_scalar_prefetch=2,
        grid=(n // tm,),
        in_specs=[pl.BlockSpec((tm, d), row), pl.BlockSpec(memory_space=pl.ANY),
                  pl.BlockSpec((tm, LANES), row),
                  pl.BlockSpec((tm, PLE_DIM), lambda i, s1, s2: (layer * (n // tm) + i, 0)),
                  pl.BlockSpec((1, d), fixed), pl.BlockSpec((d, d), fixed), pl.BlockSpec((PLE_DIM, d), fixed),
                  pl.BlockSpec((1, d), fixed)],
        out_specs=[pl.BlockSpec((tm, d), row), pl.BlockSpec((tm, d), row)],
        scratch_shapes=[pltpu.VMEM((2, 2, tm, d), F32), pltpu.SemaphoreType.DMA((2, 2))],
    )
    return pl.pallas_call(
        _ple_kernel,
        grid_spec=grid_spec,
        out_shape=[jax.ShapeDtypeStruct((n, d), F32), jax.ShapeDtypeStruct((n, d), next_dtype)],
        compiler_params=_cparams(("arbitrary",)),
        name="ple_norm",
    )(slot1, slot2, h, yb, routed, p, g_ple.reshape(1, d), w_gate, w_proj, g_next.reshape(1, d))


def _compress_kernel(*refs):
    x_refs = refs[:2 * NSA_GROUPS]
    pos_ref, w1_ref, w2_ref, cos_ref, sin_ref, o_ref = refs[2 * NSA_GROUPS:]
    nchunk = x_refs[0].shape[0] // CMP_STRIDE
    lane = lax.broadcasted_iota(jnp.int32, (nchunk, LANES), 1)
    for kv in range(2):
        for g in range(NSA_GROUPS):
            acc = [jnp.zeros((nchunk, HEAD_DIM), F32) for _ in range(CMP_CHUNKS)]
            x_ref = x_refs[kv * NSA_GROUPS + g]
            for half in range(CMP_CHUNKS):
                for j in range(CMP_STRIDE):
                    jj = half * CMP_STRIDE + j
                    rows = x_ref[pl.ds(j, nchunk, stride=CMP_STRIDE), :]
                    blk = (rows + pos_ref[kv, jj:jj + 1, :]).astype(BF16)
                    acc[half] = acc[half] + jnp.dot(blk, w1_ref[kv, jj * HEAD_DIM:(jj + 1) * HEAD_DIM, :],
                                                    preferred_element_type=F32)
            pre = acc[0] + pltpu.roll(acc[1], nchunk - 1, 0)
            out = jnp.dot(_silu(pre).astype(BF16), w2_ref[kv], preferred_element_type=F32)
            if kv == 0:
                out = _rope_apply(out, cos_ref[...], sin_ref[...], lane)
            o_ref[kv * NSA_GROUPS + g] = out.astype(o_ref.dtype)


def _compress(kv_cmp, batch, seq, cmp_pos, cmp_w1, cmp_w2, cos_c, sin_c):
    nchunk = seq // CMP_STRIDE
    return pl.pallas_call(
        _compress_kernel,
        grid=(batch,),
        in_specs=[pl.BlockSpec((seq, HEAD_DIM), functools.partial(lambda b, cb: (b, cb), cb=cb))
                  for cb in range(2 * NSA_GROUPS)] + [
                  pl.BlockSpec(cmp_pos.shape, lambda b: (0, 0, 0)),
                  pl.BlockSpec(cmp_w1.shape, lambda b: (0, 0, 0)),
                  pl.BlockSpec(cmp_w2.shape, lambda b: (0, 0, 0)),
                  pl.BlockSpec((nchunk, LANES), lambda b: (0, 0)),
                  pl.BlockSpec((nchunk, LANES), lambda b: (0, 0))],
        out_specs=pl.BlockSpec((None, 2 * NSA_GROUPS, nchunk, HEAD_DIM), lambda b: (b, 0, 0, 0)),
        out_shape=jax.ShapeDtypeStruct((batch, 2 * NSA_GROUPS, nchunk, HEAD_DIM), BF16),
        compiler_params=_cparams(("parallel",)),
        name="nsa_compress",
    )(*([kv_cmp] * (2 * NSA_GROUPS)), cmp_pos, cmp_w1.astype(BF16), cmp_w2.astype(BF16), cos_c, sin_c)


NSA_TQ = 256
NSA_TK = 512
EXP2_SCALE = HEAD_DIM ** -0.5 * math.log2(math.e)


def _softmax_tile(s, m_old, l_old):
    m_new = jnp.maximum(m_old, jnp.max(s, axis=-1, keepdims=True))
    alpha = jnp.exp2(EXP2_SCALE * (m_old - m_new))
    p = jnp.exp2(EXP2_SCALE * (s - jnp.concatenate([m_new] * (s.shape[1] // LANES), axis=1)))
    l_new = alpha * l_old + jnp.sum(p, axis=-1, keepdims=True)
    return m_new, l_new, alpha, p


def _nsa_kernel(q_ref, kc_ref, vc_ref, ksel_ref, vsel_ref, kwin_ref, vwin_ref, gate_ref, pool_ref, negexp_ref,
                o_ref, acc_ref):
    tq = NSA_TQ
    rep = NSA_REP
    q0 = pl.program_id(2) * tq
    q = q_ref[...]
    qs = jnp.concatenate([q[:, r * HEAD_DIM:(r + 1) * HEAD_DIM] for r in range(rep)], axis=0)
    tpos = q0 + lax.broadcasted_iota(jnp.int32, (tq, 1), 0)
    heads = [slice(r * tq, (r + 1) * tq) for r in range(rep)]

    ncmp = kc_ref.shape[0]
    cend = lax.broadcasted_iota(jnp.int32, (1, ncmp), 1) * CMP_STRIDE + (CMP_CHUNKS * CMP_STRIDE - 1)
    cvalid = cend <= tpos
    cbias = jnp.where(cvalid, 0.0, NEG_INF).astype(F32)
    s_all = _nt_dot(qs, kc_ref[...])
    probs = []
    for hs in heads:
        s = s_all[hs] + cbias
        m = jnp.max(s, axis=-1, keepdims=True)
        p = jnp.where(cvalid, jnp.exp2(EXP2_SCALE * (s - m)), 0.0)
        l = jnp.sum(p, axis=-1, keepdims=True)
        probs.append(p * jnp.where(l > 0.0, 1.0 / l, 0.0))
    o_cmp = jnp.dot(jnp.concatenate(probs, axis=0).astype(BF16), vc_ref[...], preferred_element_type=F32)
    imp = probs[0]
    for p in probs[1:]:
        imp = imp + p
    p_sel = jnp.dot(imp, pool_ref[...], preferred_element_type=F32, precision=HIGHEST)

    nsel = ksel_ref.shape[0] // SEL_BLOCK
    blk = lax.broadcasted_iota(jnp.int32, (tq, LANES), 1)
    cur = tpos // SEL_BLOCK
    forced = (blk == 0) | (blk == cur) | (blk == cur - 1)
    valid = blk * SEL_BLOCK <= tpos
    score = jnp.where(forced, SEL_FORCE, jnp.where(valid, p_sel, NEG_INF))
    score = jnp.where(blk < nsel, score, 3.0 * NEG_INF)
    st = score.T
    sub = 8
    chunks = [st[c * sub:(c + 1) * sub, :] for c in range(nsel // sub)]
    cnt = [jnp.zeros((sub, tq), F32) for _ in chunks]
    jrow = lax.broadcasted_iota(jnp.int32, (sub, tq), 0)
    for i in range(nsel):
        ci, ii = divmod(i, sub)
        si = chunks[ci][ii:ii + 1, :]
        for c in range(len(chunks)):
            if c > ci:
                beats = si >= chunks[c]
            elif c < ci:
                beats = si > chunks[c]
            else:
                beats = jnp.where(jrow > ii, jnp.where(si >= chunks[c], 1.0, 0.0), jnp.where(si > chunks[c], 1.0, 0.0)) > 0.5
            cnt[c] = cnt[c] + jnp.where(beats, 1.0, 0.0)
    tpos_row = q0 + lax.broadcasted_iota(jnp.int32, (sub, tq), 1)
    dropped = []
    for c, cn in enumerate(cnt):
        keep_blk = (cn < min(SEL_TOPK, nsel)) & ((jrow + c * sub) * SEL_BLOCK <= tpos_row)
        dropped.append(jnp.where(keep_blk, 0.0, 1.0))
    dropped_t = jnp.concatenate(dropped + [jnp.zeros((LANES - nsel, tq), F32)], axis=0)
    dropped_q = dropped_t.T.astype(BF16)
    q_aug = jnp.concatenate([qs, jnp.concatenate([dropped_q] * rep, axis=0)], axis=1)

    tk = NSA_TK
    last = q0 // tk
    acc_ref[...] = jnp.zeros_like(acc_ref)

    def sel_tile(kt, carry, diagonal):
        m_i, l_i = carry
        k0 = pl.multiple_of(kt * tk, tk)
        k_aug = jnp.concatenate([ksel_ref[pl.ds(k0, tk), :], negexp_ref[pl.ds(k0, tk), :]], axis=1)
        v_t = vsel_ref[pl.ds(k0, tk), :]
        s_t = _nt_dot(q_aug, k_aug)
        if diagonal:
            kpos = k0 + lax.broadcasted_iota(jnp.int32, (1, tk), 1)
            causal = jnp.where(kpos <= tpos, 0.0, NEG_INF).astype(F32)
        m_out, l_out, alphas, ps = [], [], [], []
        for hs in heads:
            s = s_t[hs] + causal if diagonal else s_t[hs]
            m_new, l_new, alpha, p = _softmax_tile(s, m_i[hs], l_i[hs])
            m_out.append(m_new)
            l_out.append(l_new)
            alphas.append(alpha)
            ps.append(p.astype(BF16))
        pv = jnp.dot(jnp.concatenate(ps, axis=0), v_t, preferred_element_type=F32)
        acc_ref[...] = jnp.concatenate(alphas, axis=0) * acc_ref[...] + pv
        return jnp.concatenate(m_out, axis=0), jnp.concatenate(l_out, axis=0)

    init = (jnp.full((rep * tq, LANES), NEG_INF, F32), jnp.zeros((rep * tq, LANES), F32))
    carry = lax.fori_loop(0, last, functools.partial(sel_tile, diagonal=False), init)
    _, l_s = sel_tile(last, carry, True)
    o_sel = acc_ref[...] / l_s

    wk = WINDOW + tq
    w0 = pl.multiple_of(jnp.maximum(q0 - WINDOW, 0), tq)
    kw = kwin_ref[pl.ds(w0, wk), :]
    vw = vwin_ref[pl.ds(w0, wk), :]
    kpos = w0 + lax.broadcasted_iota(jnp.int32, (1, wk), 1)
    wbias = jnp.where((kpos <= tpos) & (kpos > tpos - WINDOW), 0.0, NEG_INF).astype(F32)
    sw_all = _nt_dot(qs, kw)
    pws, lws = [], []
    for hs in heads:
        sw = sw_all[hs] + wbias
        pw = jnp.exp2(EXP2_SCALE * (sw - jnp.max(sw, axis=-1, keepdims=True)))
        lws.append(jnp.sum(pw, axis=-1, keepdims=True))
        pws.append(pw.astype(BF16))
    o_win = jnp.dot(jnp.concatenate(pws, axis=0), vw, preferred_element_type=F32) / jnp.concatenate(lws, axis=0)

    gate = jax.nn.sigmoid(gate_ref[...])
    for r, rs in enumerate(heads):
        o = (gate[:, 3 * r:3 * r + 1] * o_cmp[rs] + gate[:, 3 * r + 1:3 * r + 2] * o_sel[rs]
             + gate[:, 3 * r + 2:3 * r + 3] * o_win[rs])
        o_ref[:, r * HEAD_DIM:(r + 1) * HEAD_DIM] = o.astype(o_ref.dtype)


def _nsa(roped, plain, cmp_kv, gates, batch, seq, pool, negexp):
    tq = NSA_TQ
    nq = seq // tq
    gw = NSA_REP * HEAD_DIM
    ksel_cb = NSA_HEADS
    kwin_cb = NSA_HEADS + NSA_GROUPS
    gate_cb = 2 * NSA_GROUPS
    ncmp = seq // CMP_STRIDE
    seq_blk = lambda cb: pl.BlockSpec((seq, HEAD_DIM), functools.partial(lambda b, g, t, cb: (b, cb + g), cb=cb))
    return pl.pallas_call(
        _nsa_kernel,
        grid=(batch, NSA_GROUPS, nq),
        in_specs=[pl.BlockSpec((tq, gw), lambda b, g, t: (b * nq + t, g)),
                  pl.BlockSpec((None, None, ncmp, HEAD_DIM), lambda b, g, t: (b, g, 0, 0)),
                  pl.BlockSpec((None, None, ncmp, HEAD_DIM), lambda b, g, t: (b, NSA_GROUPS + g, 0, 0)),
                  seq_blk(ksel_cb), seq_blk(0), seq_blk(kwin_cb), seq_blk(NSA_GROUPS),
                  pl.BlockSpec((tq, LANES), lambda b, g, t: (b * nq + t, gate_cb + g)),
                  pl.BlockSpec(pool.shape, lambda b, g, t: (0, 0)),
                  pl.BlockSpec(negexp.shape, lambda b, g, t: (0, 0))],
        out_specs=pl.BlockSpec((tq, gw), lambda b, g, t: (b * nq + t, g)),
        out_shape=jax.ShapeDtypeStruct((batch * seq, NSA_HEADS * HEAD_DIM), BF16),
        scratch_shapes=[pltpu.VMEM((NSA_REP * tq, HEAD_DIM), F32)],
        compiler_params=_cparams(("parallel", "parallel", "arbitrary")),
        name="nsa_attention",
    )(roped, cmp_kv, cmp_kv, roped, plain, roped, plain, gates, pool, negexp)


DIFF_TQ = 512
DIFF_TK = 512


def _diff_kernel(q_ref, k_ref, v_ref, lam_ref, subln_ref, o_ref, acc_ref, *, lambda_init):
    tq, tk = DIFF_TQ, DIFF_TK
    q0 = pl.program_id(2) * tq
    last = q0 // tk
    q = q_ref[...]
    qm = (q[:, :HEAD_DIM], q[:, HEAD_DIM:])
    tpos = q0 + lax.broadcasted_iota(jnp.int32, (tq, 1), 0)
    maps = (slice(0, tq), slice(tq, 2 * tq))
    acc_ref[...] = jnp.zeros_like(acc_ref)

    def tile(kt, carry, diagonal):
        m_i, l_i = carry
        k0 = pl.multiple_of(kt * tk, tk)
        k = k_ref[pl.ds(k0, tk), :]
        v = v_ref[pl.ds(k0, tk), :]
        if diagonal:
            kpos = k0 + lax.broadcasted_iota(jnp.int32, (1, tk), 1)
            causal = jnp.where(kpos <= tpos, 0.0, NEG_INF).astype(F32)
        m_out, l_out, alphas, ps = [], [], [], []
        for i, ms in enumerate(maps):
            s = _nt_dot(qm[i], k[:, i * HEAD_DIM:(i + 1) * HEAD_DIM])
            if diagonal:
                s = s + causal
            m_new, l_new, alpha, p = _softmax_tile(s, m_i[ms], l_i[ms])
            m_out.append(m_new)
            l_out.append(l_new)
            alphas.append(jnp.concatenate([alpha] * (DIFF_V // LANES), axis=1))
            ps.append(p.astype(BF16))
        pv = jnp.dot(jnp.concatenate(ps, axis=0), v, preferred_element_type=F32)
        acc_ref[...] = jnp.concatenate(alphas, axis=0) * acc_ref[...] + pv
        return jnp.concatenate(m_out, axis=0), jnp.concatenate(l_out, axis=0)

    init = (jnp.full((2 * tq, LANES), NEG_INF, F32), jnp.zeros((2 * tq, LANES), F32))
    carry = lax.fori_loop(0, last, functools.partial(tile, diagonal=False), init)
    _, l_f = tile(last, carry, True)
    o = acc_ref[...] / jnp.concatenate([l_f] * (DIFF_V // LANES), axis=1)
    lam = lam_ref[...]
    lam_full = (jnp.exp(jnp.sum(lam[0:1] * lam[1:2], axis=-1, keepdims=True))
                - jnp.exp(jnp.sum(lam[2:3] * lam[3:4], axis=-1, keepdims=True)) + lambda_init)
    od = o[:tq] - lam_full * o[tq:]
    ms = jnp.mean(od * od, axis=-1, keepdims=True)
    o_ref[...] = (od * lax.rsqrt(ms + NORM_EPS) * subln_ref[...] * (1.0 - lambda_init)).astype(o_ref.dtype)


def _diff_attn(roped, plain, lam, subln, batch, seq, lambda_init):
    tq = DIFF_TQ
    nq = seq // tq
    pair = 2 * HEAD_DIM
    dq_cb = (NSA_HEADS + 2 * NSA_GROUPS) * HEAD_DIM // pair
    dk_cb = dq_cb + DIFF_HEADS
    dv_cb = 2 * NSA_GROUPS * HEAD_DIM // DIFF_V
    return pl.pallas_call(
        functools.partial(_diff_kernel, lambda_init=lambda_init),
        grid=(batch, DIFF_HEADS, nq),
        in_specs=[pl.BlockSpec((tq, pair), lambda b, h, t: (b * nq + t, dq_cb + h)),
                  pl.BlockSpec((seq, pair), lambda b, h, t: (b, dk_cb + h)),
                  pl.BlockSpec((seq, DIFF_V), lambda b, h, t: (b, dv_cb + h)),
                  pl.BlockSpec(lam.shape, lambda b, h, t: (0, 0)),
                  pl.BlockSpec((1, DIFF_V), lambda b, h, t: (0, 0))],
        out_specs=pl.BlockSpec((tq, DIFF_V), lambda b, h, t: (b * nq + t, h)),
        out_shape=jax.ShapeDtypeStruct((batch * seq, DIFF_HEADS * DIFF_V), BF16),
        scratch_shapes=[pltpu.VMEM((2 * tq, DIFF_V), F32)],
        compiler_params=_cparams(("parallel", "parallel", "arbitrary")),
        name="diff_attention",
    )(roped, roped, plain, lam, subln.reshape(1, DIFF_V))


def _rope_tables(pos):
    inv_freq = ROPE_THETA ** (-jnp.arange(ROPE_HALF, dtype=F32) / ROPE_HALF)
    ang = pos.astype(F32)[:, None] * inv_freq[None, :]
    cos, sin = jnp.cos(ang), jnp.sin(ang)
    rest = LANES - 2 * ROPE_HALF
    n = pos.shape[0]
    cosf = jnp.concatenate([cos, cos, jnp.ones((n, rest), F32)], axis=1)
    sinf = jnp.concatenate([-sin, sin, jnp.zeros((n, rest), F32)], axis=1)
    return cosf, sinf


def _router_weights(w_group, b_group, w_expert, b_expert):
    pad = LANES - MOE_GROUPS - MOE_EXPERTS
    wr = jnp.concatenate([w_group, w_expert, jnp.zeros((w_group.shape[0], pad), F32)], axis=1)
    br = jnp.concatenate([b_group, b_expert, jnp.zeros((pad,), F32)]).reshape(1, LANES)
    w_hi = wr.astype(BF16)
    w_lo = (wr - w_hi.astype(F32)).astype(BF16)
    return jnp.concatenate([w_hi, w_lo], axis=1), br


def _even_layer(h, hn, batch, seq, w_in, conv_w, conv_b, dt_bias, a_log, d_skip, gate_norm, sc_w, w_out,
                g_ffn, wr, br):
    i = SSD_INNER
    conv_ch = i + 2 * SSD_GROUPS * SSD_STATE
    o_dt = i + conv_ch
    o_sc = o_dt + SSD_HEADS
    w_main = jnp.concatenate([w_in[:, :o_dt], w_in[:, o_sc:]], axis=1).astype(BF16)
    w_dt = jnp.pad(w_in[:, o_dt:o_sc], ((0, 0), (0, LANES - SSD_HEADS))).astype(BF16)
    proj = _matmul(hn, w_main, F32, 1024, 1024)
    mix = _even_core(proj, hn, w_dt, batch, seq, conv_w, conv_b, dt_bias, a_log, d_skip, gate_norm, sc_w)
    return _outproj([mix], [w_out.astype(BF16)], h, g_ffn, wr, br)


def _odd_layer(h, hn, batch, seq, w_in, cmp_pos, cmp_w1, cmp_w2, lam, subln, w_out, lambda_init, g_ffn, wr, br):
    hd, kvw = HEAD_DIM, NSA_GROUPS * HEAD_DIM
    o = [0]
    for wdt in (NSA_HEADS * hd,) + (kvw,) * 6 + (3 * NSA_HEADS,) + (DIFF_HEADS * 2 * hd,) * 2 + (DIFF_HEADS * DIFF_V,):
        o.append(o[-1] + wdt)
    col = lambda k: w_in[:, o[k]:o[k + 1]]
    q, k_cmp, v_cmp, k_sel, v_sel, k_win, v_win, gates, dq, dk, dv = [col(k) for k in range(11)]
    gcols = []
    for g in range(NSA_GROUPS):
        per = 3 * NSA_REP
        gcols.append(jnp.pad(gates[:, g * per:(g + 1) * per], ((0, 0), (0, LANES - per))))
    w_rope = jnp.concatenate([q, k_sel, k_win, dq, dk], axis=1).astype(BF16)
    w_plain = jnp.concatenate([v_sel, v_win, dv], axis=1).astype(BF16)
    w_cmp_gates = jnp.concatenate([k_cmp, v_cmp] + gcols, axis=1).astype(BF16)
    pos = jnp.arange(seq)
    roped = _matmul(hn, w_rope, BF16, 512, 512, rope=_rope_tables(pos), seq=seq)
    plain = _matmul(hn, w_plain, BF16, 512, 512)
    kv_cmp = _matmul(hn, w_cmp_gates, F32, 512, w_cmp_gates.shape[1])
    ncmp = seq // CMP_STRIDE
    cmp_end = jnp.arange(ncmp) * CMP_STRIDE + CMP_CHUNKS * CMP_STRIDE - 1
    cos_c, sin_c = _rope_tables(cmp_end)
    cmp_kv = _compress(kv_cmp, batch, seq, cmp_pos, cmp_w1, cmp_w2, cos_c, sin_c)
    n_idx = jnp.arange(ncmp)[:, None]
    j_idx = jnp.arange(LANES)[None, :]
    per_sel = SEL_BLOCK // CMP_STRIDE
    pool = ((n_idx >= per_sel * j_idx - 1) & (n_idx <= per_sel * j_idx + per_sel - 1)
            & (n_idx < ncmp - 1) & (j_idx < seq // SEL_BLOCK)).astype(F32)
    negexp = jnp.where((jnp.arange(seq)[:, None] // SEL_BLOCK) == jnp.arange(LANES)[None, :], NEG_INF, 0.0).astype(BF16)
    o_nsa = _nsa(roped, plain, cmp_kv, kv_cmp, batch, seq, pool, negexp)
    o_diff = _diff_attn(roped, plain, lam, subln, batch, seq, lambda_init)
    n_nsa = NSA_HEADS * hd
    w_o = w_out.astype(BF16)
    return _outproj([o_nsa, o_diff], [w_o[:n_nsa], w_o[n_nsa:]], h, g_ffn, wr, br)


def kernel(x, p, norm_mix, norm_ffn, norm_ple, norm_final, ev_w_in, ev_conv_w, ev_conv_b, ev_dt_bias, ev_a_log, ev_d_skip, ev_gate_norm, ev_sc_w, ev_w_out, od_w_in, od_cmp_pos, od_cmp_w1, od_cmp_w2, od_lambda, od_subln, od_w_out, moe_w_group, moe_b_group, moe_w_expert, moe_b_expert, moe_w_gate, moe_w_up, moe_w_down, ple_gate, ple_proj):
    batch, seq, d = x.shape
    n = batch * seq
    depth = p.shape[0]
    h = x.reshape(n, d)
    hn = _rmsnorm(h, norm_mix[0], BF16)
    for i in range(depth):
        j = i // 2
        wr, br = _router_weights(moe_w_group[i], moe_b_group[i], moe_w_expert[i], moe_b_expert[i])
        if i % 2 == 0:
            h, hn2, logits = _even_layer(h, hn, batch, seq, ev_w_in[j], ev_conv_w[j], ev_conv_b[j], ev_dt_bias[j],
                                         ev_a_log[j], ev_d_skip[j], ev_gate_norm[j], ev_sc_w[j], ev_w_out[j],
                                         norm_ffn[i], wr, br)
        else:
            lambda_init = 0.8 - 0.6 * math.exp(-0.3 * i)
            h, hn2, logits = _odd_layer(h, hn, batch, seq, od_w_in[j], od_cmp_pos[j], od_cmp_w1[j], od_cmp_w2[j],
                                        od_lambda[j], od_subln[j], od_w_out[j], lambda_init, norm_ffn[i], wr, br)
        yb, slot1, slot2, routed = _moe(hn2, logits, i, moe_w_gate, moe_w_up, moe_w_down)
        last = i == depth - 1
        g_next = norm_final if last else norm_mix[i + 1]
        h, hn = _ple(h, yb, slot1, slot2, routed, p.reshape(depth * n, PLE_DIM), i, norm_ple[i],
                     ple_gate[i].astype(BF16), ple_proj[i].astype(BF16), g_next, F32 if last else BF16)
    return hn.reshape(batch, seq, d)
```

```python
import functools
import math

import jax
import jax.numpy as jnp
from jax import lax
from jax.experimental import pallas as pl
from jax.experimental.pallas import tpu as pltpu

F32 = jnp.float32
BF16 = jnp.bfloat16
HIGHEST = lax.Precision.HIGHEST

LANES = 128
D_MODEL = 2048
NORM_EPS = 1e-6
ROPE_THETA = 500000.0
ROPE_HALF = 16
NEG_INF = -1e30

SSD_HEADS = 32
SSD_HEAD_DIM = 64
SSD_GROUPS = 8
SSD_STATE = 128
SSD_CHUNK = 128
SSD_INNER = 2048
CONV_HALO = 8

HEAD_DIM = 128
NSA_HEADS = 8
NSA_GROUPS = 2
NSA_REP = 4
CMP_STRIDE = 16
CMP_CHUNKS = 2
SEL_BLOCK = 64
SEL_TOPK = 16
SEL_FORCE = 1e4
WINDOW = 512
DIFF_HEADS = 4
DIFF_V = 256

MOE_GROUPS = 4
MOE_PER_GROUP = 8
MOE_EXPERTS = 32
MOE_FF = 1024
MOE_ROWS = 256
ROUTER_TILE = 512
PLE_DIM = 256

VMEM_LIMIT = 56 * 1024 * 1024


def _cparams(sem, vmem=VMEM_LIMIT):
    return pltpu.CompilerParams(dimension_semantics=sem, vmem_limit_bytes=vmem)


def _nt_dot(a, b):
    return lax.dot_general(a, b, (((1,), (1,)), ((), ())), preferred_element_type=F32)


def _silu(x):
    return x * jax.nn.sigmoid(x)


def _rmsnorm_kernel(x_ref, g_ref, o_ref):
    x = x_ref[...]
    ms = jnp.mean(x * x, axis=-1, keepdims=True)
    o_ref[...] = (x * lax.rsqrt(ms + NORM_EPS) * g_ref[...]).astype(o_ref.dtype)


def _rmsnorm(x, g, out_dtype, tm=512):
    n, d = x.shape
    return pl.pallas_call(
        _rmsnorm_kernel,
        grid=(n // tm,),
        in_specs=[pl.BlockSpec((tm, d), lambda i: (i, 0)), pl.BlockSpec((1, d), lambda i: (0, 0))],
        out_specs=pl.BlockSpec((tm, d), lambda i: (i, 0)),
        out_shape=jax.ShapeDtypeStruct((n, d), out_dtype),
        compiler_params=_cparams(("parallel",)),
        name="rmsnorm",
    )(x, g.reshape(1, d))


def _rope_apply(x, cosf, sinf, lane):
    swapped = jnp.where(lane < ROPE_HALF, pltpu.roll(x, LANES - ROPE_HALF, 1), pltpu.roll(x, ROPE_HALF, 1))
    return x * cosf + swapped * sinf


def _mm_kernel(a_ref, w_ref, o_ref):
    o_ref[...] = jnp.dot(a_ref[...], w_ref[...], preferred_element_type=F32).astype(o_ref.dtype)


def _mm_rope_kernel(a_ref, w_ref, cos_ref, sin_ref, o_ref):
    acc = jnp.dot(a_ref[...], w_ref[...], preferred_element_type=F32)
    tm, tn = acc.shape
    cosf = cos_ref[...]
    sinf = sin_ref[...]
    lane = lax.broadcasted_iota(jnp.int32, (tm, LANES), 1)
    for h in range(tn // LANES):
        sl = slice(h * LANES, (h + 1) * LANES)
        o_ref[:, sl] = _rope_apply(acc[:, sl], cosf, sinf, lane).astype(o_ref.dtype)


def _matmul(a, w, out_dtype, tm, tn, rope=None, seq=None):
    m, k = a.shape
    n = w.shape[1]
    in_specs = [pl.BlockSpec((tm, k), lambda i, j: (i, 0)), pl.BlockSpec((k, tn), lambda i, j: (0, j))]
    args = [a, w]
    body = _mm_kernel
    if rope is not None:
        per_seq = seq // tm
        in_specs += [pl.BlockSpec((tm, LANES), lambda i, j: (i % per_seq, 0))] * 2
        args += list(rope)
        body = _mm_rope_kernel
    return pl.pallas_call(
        body,
        grid=(m // tm, n // tn),
        in_specs=in_specs,
        out_specs=pl.BlockSpec((tm, tn), lambda i, j: (i, j)),
        out_shape=jax.ShapeDtypeStruct((m, n), out_dtype),
        compiler_params=_cparams(("parallel", "parallel")),
        name="proj_rope" if rope is not None else "proj",
    )(*args)


def _even_core_kernel(z_ref, xs_ref, bc_ref, scb_ref, scc_ref, sch_ref,
                      xs_h_ref, bc_h_ref, scc_h_ref, sch_h_ref, hn_ref, wdt_ref,
                      cwx_ref, cwbc_ref, cbx_ref, cbbc_ref, dtb_ref, alog_ref, dskip_ref, gnorm_ref, scw_ref,
                      o_ref, state_ref, xsc_ref, bcc_ref, y_ref):
    q = SSD_CHUNK
    first = pl.program_id(1) == 0
    keep = jnp.where(first, 0.0, 1.0).astype(F32)

    @pl.when(first)
    def _():
        state_ref[...] = jnp.zeros_like(state_ref)

    strip = 512

    def causal_conv(load_main, load_halo, w_ref, width, finish):
        for c0 in range(0, SSD_INNER, strip):
            cs = slice(c0, c0 + strip)
            ext = jnp.concatenate([load_halo(cs) * keep, load_main(cs)], axis=0)
            acc = w_ref[width - 1:width, cs] * ext[CONV_HALO:]
            for s in range(1, width):
                acc = acc + w_ref[width - 1 - s:width - s, cs] * pltpu.roll(ext, s, 0)[CONV_HALO:]
            finish(cs, acc)

    def fin_xs(cs, acc):
        xsc_ref[:, cs] = _silu(acc + cbx_ref[:, cs])

    def fin_bc(cs, acc):
        bcc_ref[:, cs] = _silu(acc + cbbc_ref[:, cs])

    causal_conv(lambda cs: xs_ref[:, cs], lambda cs: xs_h_ref[:, cs], cwx_ref, 4, fin_xs)
    causal_conv(lambda cs: bc_ref[:, cs], lambda cs: bc_h_ref[:, cs], cwbc_ref, 4, fin_bc)

    def fin_sc(cs, acc):
        o_ref[:, SSD_INNER + cs.start:SSD_INNER + cs.stop] = (scb_ref[:, cs] * acc).astype(o_ref.dtype)

    causal_conv(lambda cs: scc_ref[:, cs] * sch_ref[:, cs], lambda cs: scc_h_ref[:, cs] * sch_h_ref[:, cs],
                scw_ref, 3, fin_sc)

    dt_raw = jnp.dot(hn_ref[...], wdt_ref[...], preferred_element_type=F32)
    dt = jax.nn.softplus(dt_raw + dtb_ref[...])
    d_a = dt * (-jnp.exp(alog_ref[...]))
    row = lax.broadcasted_iota(jnp.int32, (q, q), 0)
    col = lax.broadcasted_iota(jnp.int32, (q, q), 1)
    causal = row >= col
    tril = jnp.where(causal, 1.0, 0.0).astype(F32)
    a_cum = jnp.dot(tril, d_a, preferred_element_type=F32, precision=HIGHEST)
    a_last = a_cum[q - 1:q, :]
    decay_end = jnp.exp(a_last - a_cum)
    chunk_decay = jnp.exp(a_last)
    exp_acum = jnp.exp(a_cum)
    a_cum_t = a_cum.T
    lane = lax.broadcasted_iota(jnp.int32, (q, LANES), 1)
    lo = lane < SSD_HEAD_DIM
    lane_row = lax.broadcasted_iota(jnp.int32, (1, LANES), 1)
    lo_row = lane_row < SSD_HEAD_DIM

    def pair_cols(mat, h0):
        return jnp.where(lo, mat[:, h0:h0 + 1], mat[:, h0 + 1:h0 + 2])

    for g in range(SSD_GROUPS):
        gs = slice(g * SSD_STATE, (g + 1) * SSD_STATE)
        b_g = bcc_ref[:, gs]
        c_g = bcc_ref[:, SSD_GROUPS * SSD_STATE + g * SSD_STATE:SSD_GROUPS * SSD_STATE + (g + 1) * SSD_STATE]
        b_gt = b_g.T.astype(BF16)
        c_gb = c_g.astype(BF16)
        cb = jnp.dot(c_gb, b_gt, preferred_element_type=F32)
        for hp in range(2):
            pr = g * 2 + hp
            h0 = 2 * pr
            ps = slice(pr * LANES, (pr + 1) * LANES)
            xp = xsc_ref[:, ps]
            xdt = xp * pair_cols(dt, h0)
            mats = []
            for h in (h0, h0 + 1):
                seg = a_cum[:, h:h + 1] - a_cum_t[h:h + 1, :]
                dec = jnp.where(causal, jnp.exp(jnp.minimum(seg, 0.0)), 0.0)
                mats.append((cb * dec).astype(BF16))
            lhs = jnp.concatenate(mats, axis=1)
            rhs = jnp.concatenate([jnp.where(lo, xdt, 0.0), jnp.where(lo, 0.0, xdt)], axis=0).astype(BF16)
            y = jnp.dot(lhs, rhs, preferred_element_type=F32)
            st = state_ref[pr]
            y = y + jnp.dot(c_gb, st.astype(BF16), preferred_element_type=F32) * pair_cols(exp_acum, h0)
            xw = (xdt * pair_cols(decay_end, h0)).astype(BF16)
            cd = jnp.where(lo_row, chunk_decay[:, h0:h0 + 1], chunk_decay[:, h0 + 1:h0 + 2])
            state_ref[pr] = st * cd + jnp.dot(b_gt, xw, preferred_element_type=F32)
            y_ref[:, ps] = y + dskip_ref[:, ps] * xp

    gw = SSD_INNER // SSD_GROUPS
    for g in range(SSD_GROUPS):
        cs = slice(g * gw, (g + 1) * gw)
        yg = y_ref[:, cs] * _silu(z_ref[:, cs])
        ms = jnp.mean(yg * yg, axis=-1, keepdims=True)
        o_ref[:, cs] = (yg * lax.rsqrt(ms + NORM_EPS) * gnorm_ref[:, cs]).astype(o_ref.dtype)


def _even_core(proj, hn, w_dt, batch, seq, conv_w, conv_b, dt_bias, a_log, d_skip, gate_norm, sc_w):
    q = SSD_CHUNK
    nc = seq // q
    w = SSD_INNER
    hb = q // CONV_HALO

    def main(cb):
        return pl.BlockSpec((q, w), lambda b, c: (b * nc + c, cb))

    def halo(cb):
        return pl.BlockSpec((CONV_HALO, w), lambda b, c: (jnp.maximum((b * nc + c) * hb - 1, 0), cb))

    def full(shape):
        return pl.BlockSpec(shape, lambda b, c: (0, 0))

    pad = LANES - SSD_HEADS
    dtb = jnp.pad(dt_bias, (0, pad)).reshape(1, LANES)
    alog = jnp.pad(a_log, (0, pad)).reshape(1, LANES)
    dskip = jnp.repeat(d_skip, SSD_HEAD_DIM).reshape(1, w)
    in_specs = [main(0), main(1), main(2), main(3), main(4), main(5),
                halo(1), halo(2), halo(4), halo(5),
                pl.BlockSpec((q, hn.shape[1]), lambda b, c: (b * nc + c, 0)), full(w_dt.shape),
                full((4, w)), full((4, w)), full((1, w)), full((1, w)),
                full((1, LANES)), full((1, LANES)), full((1, w)), full((1, w)), full((3, w))]
    return pl.pallas_call(
        _even_core_kernel,
        grid=(batch, nc),
        in_specs=in_specs,
        out_specs=pl.BlockSpec((q, 2 * w), lambda b, c: (b * nc + c, 0)),
        out_shape=jax.ShapeDtypeStruct((batch * seq, 2 * w), BF16),
        scratch_shapes=[pltpu.VMEM((SSD_HEADS // 2, SSD_STATE, LANES), F32),
                        pltpu.VMEM((q, w), F32), pltpu.VMEM((q, w), F32), pltpu.VMEM((q, w), F32)],
        compiler_params=_cparams(("parallel", "arbitrary")),
        name="ssd_conv_core",
    )(proj, proj, proj, proj, proj, proj, proj, proj, proj, proj, hn, w_dt,
      conv_w[:, :w], conv_w[:, w:], conv_b[:w].reshape(1, w), conv_b[w:].reshape(1, w),
      dtb, alog, dskip, gate_norm.reshape(1, w), sc_w)


def _outproj_kernel(*refs, n_in):
    mix = refs[:n_in]
    ws = refs[n_in:2 * n_in]
    h_ref, g_ref, wr_ref, br_ref, hnew_ref, hn_ref, lg_ref = refs[2 * n_in:]
    acc = h_ref[...]
    for m_ref, w_ref in zip(mix, ws):
        acc = acc + jnp.dot(m_ref[...], w_ref[...], preferred_element_type=F32)
    hnew_ref[...] = acc
    ms = jnp.mean(acc * acc, axis=-1, keepdims=True)
    hn = acc * lax.rsqrt(ms + NORM_EPS) * g_ref[...]
    hn_ref[...] = hn
    hi = hn.astype(BF16)
    lo = (hn - hi.astype(F32)).astype(BF16)
    wr = wr_ref[...]
    t = jnp.dot(hi, wr, preferred_element_type=F32)
    u = jnp.dot(lo, wr[:, :LANES], preferred_element_type=F32)
    lg_ref[...] = t[:, :LANES] + t[:, LANES:] + u + br_ref[...]


def _outproj(mixes, ws, h, g, wr, br, tm=256):
    n, d = h.shape
    n_in = len(mixes)
    in_specs = ([pl.BlockSpec((tm, m.shape[1]), lambda i: (i, 0)) for m in mixes]
                + [pl.BlockSpec(w.shape, lambda i: (0, 0)) for w in ws]
                + [pl.BlockSpec((tm, d), lambda i: (i, 0)), pl.BlockSpec((1, d), lambda i: (0, 0)),
                   pl.BlockSpec((d, 2 * LANES), lambda i: (0, 0)), pl.BlockSpec((1, LANES), lambda i: (0, 0))])
    return pl.pallas_call(
        functools.partial(_outproj_kernel, n_in=n_in),
        grid=(n // tm,),
        in_specs=in_specs,
        out_specs=[pl.BlockSpec((tm, d), lambda i: (i, 0)), pl.BlockSpec((tm, d), lambda i: (i, 0)),
                   pl.BlockSpec((tm, LANES), lambda i: (i, 0))],
        out_shape=[jax.ShapeDtypeStruct((n, d), F32), jax.ShapeDtypeStruct((n, d), F32),
                   jax.ShapeDtypeStruct((n, LANES), F32)],
        compiler_params=_cparams(("parallel",)),
        name="outproj_norm_router",
    )(*mixes, *ws, h, g.reshape(1, d), wr, br)


ROUTE_E1, ROUTE_E2, ROUTE_W1, ROUTE_W2, ROUTE_R1, ROUTE_R2 = range(6)


def _router_kernel(lg_ref, out_ref, cnt_ref, run_ref):
    @pl.when(pl.program_id(0) == 0)
    def _():
        run_ref[...] = jnp.zeros_like(run_ref)

    x = lg_ref[...]
    t = x.shape[0]
    lane = lax.broadcasted_iota(jnp.int32, (t, LANES), 1).astype(F32)
    far = float(LANES)
    gmask = lane < MOE_GROUPS
    gl = jnp.where(gmask, x, NEG_INF)
    gmax = jnp.max(gl, axis=-1, keepdims=True)
    g_top = jnp.min(jnp.where(gl == gmax, lane, far), axis=-1, keepdims=True)
    gsum = jnp.sum(jnp.where(gmask, jnp.exp(gl - gmax), 0.0), axis=-1, keepdims=True)
    g_prob = 1.0 / gsum
    lo = MOE_GROUPS + MOE_PER_GROUP * g_top
    emask = (lane >= lo) & (lane < lo + MOE_PER_GROUP)
    el = jnp.where(emask, x, NEG_INF)
    emax = jnp.max(el, axis=-1, keepdims=True)
    ee = jnp.where(emask, jnp.exp(el - emax), 0.0)
    ep = jnp.where(emask, ee / jnp.sum(ee, axis=-1, keepdims=True), -1.0)
    p1 = jnp.max(ep, axis=-1, keepdims=True)
    i1 = jnp.min(jnp.where(ep == p1, lane, far), axis=-1, keepdims=True)
    ep2 = jnp.where(lane == i1, -1.0, ep)
    p2 = jnp.max(ep2, axis=-1, keepdims=True)
    i2 = jnp.min(jnp.where(ep2 == p2, lane, far), axis=-1, keepdims=True)
    den = p1 + p2
    w1 = g_prob * p1 / den
    w2 = g_prob * p2 / den
    oh1 = jnp.where(lane == i1, 1.0, 0.0)
    oh2 = jnp.where(lane == i2, 1.0, 0.0)
    oh = oh1 + oh2
    row = lax.broadcasted_iota(jnp.int32, (t, t), 0)
    col = lax.broadcasted_iota(jnp.int32, (t, t), 1)
    strict = jnp.where(row > col, 1.0, 0.0).astype(BF16)
    before = jnp.dot(strict, oh.astype(BF16), preferred_element_type=F32) + run_ref[...]
    r1 = jnp.sum(oh1 * before, axis=-1, keepdims=True)
    r2 = jnp.sum(oh2 * before, axis=-1, keepdims=True)
    run_ref[...] = run_ref[...] + jnp.sum(oh, axis=0, keepdims=True)
    cnt_ref[...] = run_ref[...]
    packed = jnp.zeros((t, LANES), F32)
    for k, v in ((ROUTE_E1, i1 - MOE_GROUPS), (ROUTE_E2, i2 - MOE_GROUPS), (ROUTE_W1, w1), (ROUTE_W2, w2),
                 (ROUTE_R1, r1), (ROUTE_R2, r2)):
        packed = jnp.where(lane == k, v, packed)
    out_ref[...] = packed


def _router(logits, tt=ROUTER_TILE):
    n = logits.shape[0]
    return pl.pallas_call(
        _router_kernel,
        grid=(n // tt,),
        in_specs=[pl.BlockSpec((tt, LANES), lambda i: (i, 0))],
        out_specs=[pl.BlockSpec((tt, LANES), lambda i: (i, 0)), pl.BlockSpec((1, LANES), lambda i: (0, 0))],
        out_shape=[jax.ShapeDtypeStruct((n, LANES), F32), jax.ShapeDtypeStruct((1, LANES), F32)],
        scratch_shapes=[pltpu.VMEM((1, LANES), F32)],
        compiler_params=_cparams(("arbitrary",)),
        name="router_rank",
    )(logits)


SLOT_1, SLOT_2 = 0, 1
BLK_HALVES = 2


def _slot_kernel(rt_ref, cnt_ref, slot_ref, blk_ref):
    lane_row = lax.broadcasted_iota(jnp.int32, (1, LANES), 1)
    is_exp = (lane_row >= MOE_GROUPS) & (lane_row < MOE_GROUPS + MOE_EXPERTS)
    padded = jnp.where(is_exp, jnp.floor((cnt_ref[...] + (MOE_ROWS - 1)) * (1.0 / MOE_ROWS)) * MOE_ROWS, 0.0)
    r = lax.broadcasted_iota(jnp.int32, (LANES, LANES), 0)
    c = lax.broadcasted_iota(jnp.int32, (LANES, LANES), 1)
    incl = jnp.where(r <= c, 1.0, 0.0).astype(F32)
    pad_end = jnp.dot(jnp.broadcast_to(padded, (8, LANES)), incl, preferred_element_type=F32,
                      precision=HIGHEST)[0:1]
    pad_start = pad_end - padded
    rt = rt_ref[...]
    lane = lax.broadcasted_iota(jnp.int32, rt.shape, 1).astype(F32)

    def slot(e, rank):
        return jnp.sum(jnp.where(lane == e + MOE_GROUPS, pad_start, 0.0), axis=-1, keepdims=True) + rank

    s1 = slot(rt[:, ROUTE_E1:ROUTE_E1 + 1], rt[:, ROUTE_R1:ROUTE_R1 + 1])
    s2 = slot(rt[:, ROUTE_E2:ROUTE_E2 + 1], rt[:, ROUTE_R2:ROUTE_R2 + 1])
    slot_ref[...] = jnp.where(lane == SLOT_1, s1, jnp.where(lane == SLOT_2, s2, 0.0))

    end_col = jnp.broadcast_to(pad_end, (LANES, LANES)).T
    exp_col = (r >= MOE_GROUPS) & (r < MOE_GROUPS + MOE_EXPERTS)
    last = MOE_GROUPS + MOE_EXPERTS - 1
    total = pad_end[:, last:last + 1]

    def experts_done(start):
        return jnp.sum(jnp.where(exp_col & (end_col <= start), 1.0, 0.0), axis=0, keepdims=True)

    idle = experts_done(total - MOE_ROWS)
    rows = []
    for h in range(BLK_HALVES):
        start = (lane_row + h * LANES).astype(F32) * MOE_ROWS
        valid = start < total
        rows += [jnp.where(valid, experts_done(start), idle), jnp.where(valid, 1.0, 0.0)]
    tail = jnp.where(is_exp & (padded > 0.0), pad_end - MOE_ROWS, -1.0)
    tail = jnp.where(lane_row == MOE_GROUPS + MOE_EXPERTS, total, tail)
    blk_ref[...] = jnp.concatenate(rows + [tail, jnp.zeros((8 - 2 * BLK_HALVES - 1, LANES), F32)], axis=0)


def _slots(routed, counts, tt=ROUTER_TILE):
    n = routed.shape[0]
    return pl.pallas_call(
        _slot_kernel,
        grid=(n // tt,),
        in_specs=[pl.BlockSpec((tt, LANES), lambda i: (i, 0)), pl.BlockSpec((1, LANES), lambda i: (0, 0))],
        out_specs=[pl.BlockSpec((tt, LANES), lambda i: (i, 0)), pl.BlockSpec((8, LANES), lambda i: (0, 0))],
        out_shape=[jax.ShapeDtypeStruct((n, LANES), F32), jax.ShapeDtypeStruct((8, LANES), F32)],
        compiler_params=_cparams(("arbitrary",)),
        name="moe_slots",
    )(routed, counts)


def _moe_kernel(be_ref, bv_ref, x_ref, wg_ref, wu_ref, wd_ref, o_ref):
    i = pl.program_id(0)

    @pl.when(bv_ref[i] > 0)
    def _():
        x = x_ref[...].astype(BF16)
        step = 256
        for n0 in range(0, MOE_FF, step):
            gate = jnp.dot(x, wg_ref[:, n0:n0 + step].astype(BF16), preferred_element_type=F32)
            up = jnp.dot(x, wu_ref[:, n0:n0 + step].astype(BF16), preferred_element_type=F32)
            hid = (_silu(gate) * up).astype(BF16)
            part = jnp.dot(hid, wd_ref[n0:n0 + step, :].astype(BF16), preferred_element_type=F32)
            if n0 == 0:
                o_ref[...] = part
            else:
                o_ref[...] += part

    @pl.when(bv_ref[i] == 0)
    def _():
        o_ref[...] = jnp.zeros_like(o_ref)


def _moe_ffn(xs, blk_expert, blk_valid, layer, wg, wu, wd):
    cap, d = xs.shape
    nb = cap // MOE_ROWS
    ff = wg.shape[3]
    grid_spec = pltpu.PrefetchScalarGridSpec(
        num_scalar_prefetch=2,
        grid=(nb,),
        in_specs=[pl.BlockSpec((MOE_ROWS, d), lambda i, be, bv: (i * bv[i], 0)),
                  pl.BlockSpec((None, None, d, ff), lambda i, be, bv: (layer, be[i], 0, 0)),
                  pl.BlockSpec((None, None, d, ff), lambda i, be, bv: (layer, be[i], 0, 0)),
                  pl.BlockSpec((None, None, ff, d), lambda i, be, bv: (layer, be[i], 0, 0),
                               pipeline_mode=pl.Buffered(1))],
        out_specs=pl.BlockSpec((MOE_ROWS, d), lambda i, be, bv: (i, 0)),
    )
    return pl.pallas_call(
        _moe_kernel,
        grid_spec=grid_spec,
        out_shape=jax.ShapeDtypeStruct((cap, d), F32),
        compiler_params=_cparams(("arbitrary",)),
        name="moe_ffn",
    )(blk_expert, blk_valid, xs, wg, wu, wd)


DISPATCH_TILE = 256
ROW_DMA_UNROLL = 8


def _row_copy(src_ref, src_row, dst_ref, dst_row, sem):
    return pltpu.make_async_copy(src_ref.at[pl.ds(src_row, 1), :], dst_ref.at[pl.ds(dst_row, 1), :], sem)


def _dispatch_kernel(s1_ref, s2_ref, tail_ref, hn_ref, xs_ref, zero_ref, sem, zsem):
    t = hn_ref.shape[0]
    base = pl.program_id(0) * t

    @pl.when(pl.program_id(0) == 0)
    def _():
        zero_ref[...] = jnp.zeros_like(zero_ref)

        def fill(e):
            start = pl.multiple_of(jnp.maximum(tail_ref[e], 0), MOE_ROWS)
            return pltpu.make_async_copy(zero_ref, xs_ref.at[pl.ds(start, MOE_ROWS), :], zsem)

        for e in range(MOE_EXPERTS):
            @pl.when(tail_ref[e] >= 0)
            def _():
                fill(e).start()
        for e in range(MOE_EXPERTS):
            @pl.when(tail_ref[e] >= 0)
            def _():
                fill(e).wait()

        used = tail_ref[MOE_EXPERTS]
        idle = (xs_ref.shape[0] - used) // MOE_ROWS

        def idle_fill(j):
            start = pl.multiple_of(used + j * MOE_ROWS, MOE_ROWS)
            return pltpu.make_async_copy(zero_ref, xs_ref.at[pl.ds(start, MOE_ROWS), :], zsem)

        def idle_start(j, carry):
            idle_fill(j).start()
            return carry

        def idle_wait(j, carry):
            idle_fill(j).wait()
            return carry

        lax.fori_loop(0, idle, idle_start, 0)
        lax.fori_loop(0, idle, idle_wait, 0)

    def issue(r, carry):
        _row_copy(hn_ref, r, xs_ref, s1_ref[base + r], sem.at[0]).start(priority=0)
        _row_copy(hn_ref, r, xs_ref, s2_ref[base + r], sem.at[1]).start(priority=1)
        return carry

    lax.fori_loop(0, t, issue, 0, unroll=ROW_DMA_UNROLL)

    for k in range(2):
        pltpu.make_async_copy(hn_ref, xs_ref.at[pl.ds(0, t), :], sem.at[k]).wait()


def _dispatch(hn, slot1, slot2, tail, cap, tt=DISPATCH_TILE):
    n, d = hn.shape
    grid_spec = pltpu.PrefetchScalarGridSpec(
        num_scalar_prefetch=3,
        grid=(n // tt,),
        in_specs=[pl.BlockSpec((tt, d), lambda i, s1, s2, tl: (i, 0))],
        out_specs=pl.BlockSpec(memory_space=pl.ANY),
        scratch_shapes=[pltpu.VMEM((MOE_ROWS, d), F32), pltpu.SemaphoreType.DMA((2,)), pltpu.SemaphoreType.DMA(())],
    )
    return pl.pallas_call(
        _dispatch_kernel,
        grid_spec=grid_spec,
        out_shape=jax.ShapeDtypeStruct((cap, d), F32),
        compiler_params=_cparams(("arbitrary",)),
        name="moe_dispatch",
    )(slot1, slot2, tail, hn)


def _moe(hn, logits, layer, wg, wu, wd):
    n, d = hn.shape
    routed, counts = _router(logits)
    slots, blk = _slots(routed, counts)
    slot1 = slots[:, SLOT_1].astype(jnp.int32)
    slot2 = slots[:, SLOT_2].astype(jnp.int32)
    nb = -(-(2 * n + MOE_EXPERTS * (MOE_ROWS - 1)) // MOE_ROWS)
    assert nb <= BLK_HALVES * LANES
    cap = nb * MOE_ROWS
    blk_expert = blk[0:2 * BLK_HALVES:2].reshape(-1)[:nb].astype(jnp.int32)
    blk_valid = blk[1:2 * BLK_HALVES:2].reshape(-1)[:nb].astype(jnp.int32)
    tail = blk[2 * BLK_HALVES, MOE_GROUPS:MOE_GROUPS + MOE_EXPERTS + 1].astype(jnp.int32)
    xs = _dispatch(hn, slot1, slot2, tail, cap)
    yb = _moe_ffn(xs, blk_expert, blk_valid, layer, wg, wu, wd)
    return yb, slot1, slot2, routed


def _ple_kernel(s1_ref, s2_ref, h_ref, yb_ref, rt_ref, p_ref, gp_ref, wg_ref, wp_ref, gn_ref, hout_ref, nout_ref,
                ybuf_ref, sem):
    tm = h_ref.shape[0]
    i = pl.program_id(0)
    cur = i % 2

    def gather(step, buf):
        base = step * tm

        def one(r, carry):
            for k, s_ref in enumerate((s1_ref, s2_ref)):
                _row_copy(yb_ref, s_ref[base + r], ybuf_ref.at[buf, k], r, sem.at[buf, k]).start(priority=k)
            return carry

        lax.fori_loop(0, tm, one, 0, unroll=ROW_DMA_UNROLL)

    @pl.when(i == 0)
    def _():
        gather(0, 0)

    @pl.when(i + 1 < pl.num_programs(0))
    def _():
        gather(i + 1, 1 - cur)

    for k in range(2):
        pltpu.make_async_copy(yb_ref.at[pl.ds(0, tm), :], ybuf_ref.at[cur, k], sem.at[cur, k]).wait()
    rt = rt_ref[...]
    moe = ybuf_ref[cur, 0] * rt[:, ROUTE_W1:ROUTE_W1 + 1] + ybuf_ref[cur, 1] * rt[:, ROUTE_W2:ROUTE_W2 + 1]
    h = h_ref[...] + moe
    ms = jnp.mean(h * h, axis=-1, keepdims=True)
    hn = (h * lax.rsqrt(ms + NORM_EPS) * gp_ref[...]).astype(BF16)
    gate = jax.nn.sigmoid(jnp.dot(hn, wg_ref[...], preferred_element_type=F32))
    emb = jnp.dot(p_ref[...].astype(BF16), wp_ref[...], preferred_element_type=F32)
    h = h + gate * emb
    hout_ref[...] = h
    ms = jnp.mean(h * h, axis=-1, keepdims=True)
    nout_ref[...] = (h * lax.rsqrt(ms + NORM_EPS) * gn_ref[...]).astype(nout_ref.dtype)


def _ple(h, yb, slot1, slot2, routed, p, layer, g_ple, w_gate, w_proj, g_next, next_dtype, tm=256):
    n, d = h.shape
    row = lambda i, s1, s2: (i, 0)
    fixed = lambda i, s1, s2: (0, 0)
    grid_spec = pltpu.PrefetchScalarGridSpec(
        num_scalar_prefetch=2,
        grid=(n // tm,),
        in_specs=[pl.BlockSpec((tm, d), row), pl.BlockSpec(memory_space=pl.ANY),
                  pl.BlockSpec((tm, LANES), row),
                  pl.BlockSpec((tm, PLE_DIM), lambda i, s1, s2: (layer * (n // tm) + i, 0)),
                  pl.BlockSpec((1, d), fixed), pl.BlockSpec((d, d), fixed), pl.BlockSpec((PLE_DIM, d), fixed),
                  pl.BlockSpec((1, d), fixed)],
        out_specs=[pl.BlockSpec((tm, d), row), pl.BlockSpec((tm, d), row)],
        scratch_shapes=[pltpu.VMEM((2, 2, tm, d), F32), pltpu.SemaphoreType.DMA((2, 2))],
    )
    return pl.pallas_call(
        _ple_kernel,
        grid_spec=grid_spec,
        out_shape=[jax.ShapeDtypeStruct((n, d), F32), jax.ShapeDtypeStruct((n, d), next_dtype)],
        compiler_params=_cparams(("arbitrary",)),
        name="ple_norm",
    )(slot1, slot2, h, yb, routed, p, g_ple.reshape(1, d), w_gate, w_proj, g_next.reshape(1, d))


def _compress_kernel(*refs):
    x_refs = refs[:2 * NSA_GROUPS]
    pos_ref, w1_ref, w2_ref, cos_ref, sin_ref, o_ref = refs[2 * NSA_GROUPS:]
    nchunk = x_refs[0].shape[0] // CMP_STRIDE
    lane = lax.broadcasted_iota(jnp.int32, (nchunk, LANES), 1)
    for kv in range(2):
        for g in range(NSA_GROUPS):
            acc = [jnp.zeros((nchunk, HEAD_DIM), F32) for _ in range(CMP_CHUNKS)]
            x_ref = x_refs[kv * NSA_GROUPS + g]
            for half in range(CMP_CHUNKS):
                for j in range(CMP_STRIDE):
                    jj = half * CMP_STRIDE + j
                    rows = x_ref[pl.ds(j, nchunk, stride=CMP_STRIDE), :]
                    blk = (rows + pos_ref[kv, jj:jj + 1, :]).astype(BF16)
                    acc[half] = acc[half] + jnp.dot(blk, w1_ref[kv, jj * HEAD_DIM:(jj + 1) * HEAD_DIM, :],
                                                    preferred_element_type=F32)
            pre = acc[0] + pltpu.roll(acc[1], nchunk - 1, 0)
            out = jnp.dot(_silu(pre).astype(BF16), w2_ref[kv], preferred_element_type=F32)
            if kv == 0:
                out = _rope_apply(out, cos_ref[...], sin_ref[...], lane)
            o_ref[kv * NSA_GROUPS + g] = out.astype(o_ref.dtype)


def _compress(kv_cmp, batch, seq, cmp_pos, cmp_w1, cmp_w2, cos_c, sin_c):
    nchunk = seq // CMP_STRIDE
    return pl.pallas_call(
        _compress_kernel,
        grid=(batch,),
        in_specs=[pl.BlockSpec((seq, HEAD_DIM), functools.partial(lambda b, cb: (b, cb), cb=cb))
                  for cb in range(2 * NSA_GROUPS)] + [
                  pl.BlockSpec(cmp_pos.shape, lambda b: (0, 0, 0)),
                  pl.BlockSpec(cmp_w1.shape, lambda b: (0, 0, 0)),
                  pl.BlockSpec(cmp_w2.shape, lambda b: (0, 0, 0)),
                  pl.BlockSpec((nchunk, LANES), lambda b: (0, 0)),
                  pl.BlockSpec((nchunk, LANES), lambda b: (0, 0))],
        out_specs=pl.BlockSpec((None, 2 * NSA_GROUPS, nchunk, HEAD_DIM), lambda b: (b, 0, 0, 0)),
        out_shape=jax.ShapeDtypeStruct((batch, 2 * NSA_GROUPS, nchunk, HEAD_DIM), BF16),
        compiler_params=_cparams(("parallel",)),
        name="nsa_compress",
    )(*([kv_cmp] * (2 * NSA_GROUPS)), cmp_pos, cmp_w1.astype(BF16), cmp_w2.astype(BF16), cos_c, sin_c)


NSA_TQ = 256
NSA_TK = 512
EXP2_SCALE = HEAD_DIM ** -0.5 * math.log2(math.e)


def _softmax_tile(s, m_old, l_old):
    m_new = jnp.maximum(m_old, jnp.max(s, axis=-1, keepdims=True))
    alpha = jnp.exp2(EXP2_SCALE * (m_old - m_new))
    p = jnp.exp2(EXP2_SCALE * (s - jnp.concatenate([m_new] * (s.shape[1] // LANES), axis=1)))
    l_new = alpha * l_old + jnp.sum(p, axis=-1, keepdims=True)
    return m_new, l_new, alpha, p


def _nsa_kernel(q_ref, kc_ref, vc_ref, ksel_ref, vsel_ref, kwin_ref, vwin_ref, gate_ref, pool_ref, negexp_ref,
                o_ref, acc_ref):
    tq = NSA_TQ
    rep = NSA_REP
    q0 = pl.program_id(2) * tq
    q = q_ref[...]
    qs = jnp.concatenate([q[:, r * HEAD_DIM:(r + 1) * HEAD_DIM] for r in range(rep)], axis=0)
    tpos = q0 + lax.broadcasted_iota(jnp.int32, (tq, 1), 0)
    heads = [slice(r * tq, (r + 1) * tq) for r in range(rep)]

    ncmp = kc_ref.shape[0]
    cend = lax.broadcasted_iota(jnp.int32, (1, ncmp), 1) * CMP_STRIDE + (CMP_CHUNKS * CMP_STRIDE - 1)
    cvalid = cend <= tpos
    cbias = jnp.where(cvalid, 0.0, NEG_INF).astype(F32)
    s_all = _nt_dot(qs, kc_ref[...])
    probs = []
    for hs in heads:
        s = s_all[hs] + cbias
        m = jnp.max(s, axis=-1, keepdims=True)
        p = jnp.where(cvalid, jnp.exp2(EXP2_SCALE * (s - m)), 0.0)
        l = jnp.sum(p, axis=-1, keepdims=True)
        probs.append(p * jnp.where(l > 0.0, 1.0 / l, 0.0))
    o_cmp = jnp.dot(jnp.concatenate(probs, axis=0).astype(BF16), vc_ref[...], preferred_element_type=F32)
    imp = probs[0]
    for p in probs[1:]:
        imp = imp + p
    p_sel = jnp.dot(imp, pool_ref[...], preferred_element_type=F32, precision=HIGHEST)

    nsel = ksel_ref.shape[0] // SEL_BLOCK
    blk = lax.broadcasted_iota(jnp.int32, (tq, LANES), 1)
    cur = tpos // SEL_BLOCK
    forced = (blk == 0) | (blk == cur) | (blk == cur - 1)
    valid = blk * SEL_BLOCK <= tpos
    score = jnp.where(forced, SEL_FORCE, jnp.where(valid, p_sel, NEG_INF))
    score = jnp.where(blk < nsel, score, 3.0 * NEG_INF)
    st = score.T
    sub = 8
    chunks = [st[c * sub:(c + 1) * sub, :] for c in range(nsel // sub)]
    cnt = [jnp.zeros((sub, tq), F32) for _ in chunks]
    jrow = lax.broadcasted_iota(jnp.int32, (sub, tq), 0)
    for i in range(nsel):
        ci, ii = divmod(i, sub)
        si = chunks[ci][ii:ii + 1, :]
        for c in range(len(chunks)):
            if c > ci:
                beats = si >= chunks[c]
            elif c < ci:
                beats = si > chunks[c]
            else:
                beats = jnp.where(jrow > ii, jnp.where(si >= chunks[c], 1.0, 0.0), jnp.where(si > chunks[c], 1.0, 0.0)) > 0.5
            cnt[c] = cnt[c] + jnp.where(beats, 1.0, 0.0)
    tpos_row = q0 + lax.broadcasted_iota(jnp.int32, (sub, tq), 1)
    dropped = []
    for c, cn in enumerate(cnt):
        keep_blk = (cn < min(SEL_TOPK, nsel)) & ((jrow + c * sub) * SEL_BLOCK <= tpos_row)
        dropped.append(jnp.where(keep_blk, 0.0, 1.0))
    dropped_t = jnp.concatenate(dropped + [jnp.zeros((LANES - nsel, tq), F32)], axis=0)
    dropped_q = dropped_t.T.astype(BF16)
    q_aug = jnp.concatenate([qs, jnp.concatenate([dropped_q] * rep, axis=0)], axis=1)

    tk = NSA_TK
    last = q0 // tk
    acc_ref[...] = jnp.zeros_like(acc_ref)

    def sel_tile(kt, carry, diagonal):
        m_i, l_i = carry
        k0 = pl.multiple_of(kt * tk, tk)
        k_aug = jnp.concatenate([ksel_ref[pl.ds(k0, tk), :], negexp_ref[pl.ds(k0, tk), :]], axis=1)
        v_t = vsel_ref[pl.ds(k0, tk), :]
        s_t = _nt_dot(q_aug, k_aug)
        if diagonal:
            kpos = k0 + lax.broadcasted_iota(jnp.int32, (1, tk), 1)
            causal = jnp.where(kpos <= tpos, 0.0, NEG_INF).astype(F32)
        m_out, l_out, alphas, ps = [], [], [], []
        for hs in heads:
            s = s_t[hs] + causal if diagonal else s_t[hs]
            m_new, l_new, alpha, p = _softmax_tile(s, m_i[hs], l_i[hs])
            m_out.append(m_new)
            l_out.append(l_new)
            alphas.append(alpha)
            ps.append(p.astype(BF16))
        pv = jnp.dot(jnp.concatenate(ps, axis=0), v_t, preferred_element_type=F32)
        acc_ref[...] = jnp.concatenate(alphas, axis=0) * acc_ref[...] + pv
        return jnp.concatenate(m_out, axis=0), jnp.concatenate(l_out, axis=0)

    init = (jnp.full((rep * tq, LANES), NEG_INF, F32), jnp.zeros((rep * tq, LANES), F32))
    carry = lax.fori_loop(0, last, functools.partial(sel_tile, diagonal=False), init)
    _, l_s = sel_tile(last, carry, True)
    o_sel = acc_ref[...] / l_s

    wk = WINDOW + tq
    w0 = pl.multiple_of(jnp.maximum(q0 - WINDOW, 0), tq)
    kw = kwin_ref[pl.ds(w0, wk), :]
    vw = vwin_ref[pl.ds(w0, wk), :]
    kpos = w0 + lax.broadcasted_iota(jnp.int32, (1, wk), 1)
    wbias = jnp.where((kpos <= tpos) & (kpos > tpos - WINDOW), 0.0, NEG_INF).astype(F32)
    sw_all = _nt_dot(qs, kw)
    pws, lws = [], []
    for hs in heads:
        sw = sw_all[hs] + wbias
        pw = jnp.exp2(EXP2_SCALE * (sw - jnp.max(sw, axis=-1, keepdims=True)))
        lws.append(jnp.sum(pw, axis=-1, keepdims=True))
        pws.append(pw.astype(BF16))
    o_win = jnp.dot(jnp.concatenate(pws, axis=0), vw, preferred_element_type=F32) / jnp.concatenate(lws, axis=0)

    gate = jax.nn.sigmoid(gate_ref[...])
    for r, rs in enumerate(heads):
        o = (gate[:, 3 * r:3 * r + 1] * o_cmp[rs] + gate[:, 3 * r + 1:3 * r + 2] * o_sel[rs]
             + gate[:, 3 * r + 2:3 * r + 3] * o_win[rs])
        o_ref[:, r * HEAD_DIM:(r + 1) * HEAD_DIM] = o.astype(o_ref.dtype)


def _nsa(roped, plain, cmp_kv, gates, batch, seq, pool, negexp):
    tq = NSA_TQ
    nq = seq // tq
    gw = NSA_REP * HEAD_DIM
    ksel_cb = NSA_HEADS
    kwin_cb = NSA_HEADS + NSA_GROUPS
    gate_cb = 2 * NSA_GROUPS
    ncmp = seq // CMP_STRIDE
    seq_blk = lambda cb: pl.BlockSpec((seq, HEAD_DIM), functools.partial(lambda b, g, t, cb: (b, cb + g), cb=cb))
    return pl.pallas_call(
        _nsa_kernel,
        grid=(batch, NSA_GROUPS, nq),
        in_specs=[pl.BlockSpec((tq, gw), lambda b, g, t: (b * nq + t, g)),
                  pl.BlockSpec((None, None, ncmp, HEAD_DIM), lambda b, g, t: (b, g, 0, 0)),
                  pl.BlockSpec((None, None, ncmp, HEAD_DIM), lambda b, g, t: (b, NSA_GROUPS + g, 0, 0)),
                  seq_blk(ksel_cb), seq_blk(0), seq_blk(kwin_cb), seq_blk(NSA_GROUPS),
                  pl.BlockSpec((tq, LANES), lambda b, g, t: (b * nq + t, gate_cb + g)),
                  pl.BlockSpec(pool.shape, lambda b, g, t: (0, 0)),
                  pl.BlockSpec(negexp.shape, lambda b, g, t: (0, 0))],
        out_specs=pl.BlockSpec((tq, gw), lambda b, g, t: (b * nq + t, g)),
        out_shape=jax.ShapeDtypeStruct((batch * seq, NSA_HEADS * HEAD_DIM), BF16),
        scratch_shapes=[pltpu.VMEM((NSA_REP * tq, HEAD_DIM), F32)],
        compiler_params=_cparams(("parallel", "parallel", "arbitrary")),
        name="nsa_attention",
    )(roped, cmp_kv, cmp_kv, roped, plain, roped, plain, gates, pool, negexp)


DIFF_TQ = 512
DIFF_TK = 512


def _diff_kernel(q_ref, k_ref, v_ref, lam_ref, subln_ref, o_ref, acc_ref, *, lambda_init):
    tq, tk = DIFF_TQ, DIFF_TK
    q0 = pl.program_id(2) * tq
    last = q0 // tk
    q = q_ref[...]
    qm = (q[:, :HEAD_DIM], q[:, HEAD_DIM:])
    tpos = q0 + lax.broadcasted_iota(jnp.int32, (tq, 1), 0)
    maps = (slice(0, tq), slice(tq, 2 * tq))
    acc_ref[...] = jnp.zeros_like(acc_ref)

    def tile(kt, carry, diagonal):
        m_i, l_i = carry
        k0 = pl.multiple_of(kt * tk, tk)
        k = k_ref[pl.ds(k0, tk), :]
        v = v_ref[pl.ds(k0, tk), :]
        if diagonal:
            kpos = k0 + lax.broadcasted_iota(jnp.int32, (1, tk), 1)
            causal = jnp.where(kpos <= tpos, 0.0, NEG_INF).astype(F32)
        m_out, l_out, alphas, ps = [], [], [], []
        for i, ms in enumerate(maps):
            s = _nt_dot(qm[i], k[:, i * HEAD_DIM:(i + 1) * HEAD_DIM])
            if diagonal:
                s = s + causal
            m_new, l_new, alpha, p = _softmax_tile(s, m_i[ms], l_i[ms])
            m_out.append(m_new)
            l_out.append(l_new)
            alphas.append(jnp.concatenate([alpha] * (DIFF_V // LANES), axis=1))
            ps.append(p.astype(BF16))
        pv = jnp.dot(jnp.concatenate(ps, axis=0), v, preferred_element_type=F32)
        acc_ref[...] = jnp.concatenate(alphas, axis=0) * acc_ref[...] + pv
        return jnp.concatenate(m_out, axis=0), jnp.concatenate(l_out, axis=0)

    init = (jnp.full((2 * tq, LANES), NEG_INF, F32), jnp.zeros((2 * tq, LANES), F32))
    carry = lax.fori_loop(0, last, functools.partial(tile, diagonal=False), init)
    _, l_f = tile(last, carry, True)
    o = acc_ref[...] / jnp.concatenate([l_f] * (DIFF_V // LANES), axis=1)
    lam = lam_ref[...]
    lam_full = (jnp.exp(jnp.sum(lam[0:1] * lam[1:2], axis=-1, keepdims=True))
                - jnp.exp(jnp.sum(lam[2:3] * lam[3:4], axis=-1, keepdims=True)) + lambda_init)
    od = o[:tq] - lam_full * o[tq:]
    ms = jnp.mean(od * od, axis=-1, keepdims=True)
    o_ref[...] = (od * lax.rsqrt(ms + NORM_EPS) * subln_ref[...] * (1.0 - lambda_init)).astype(o_ref.dtype)


def _diff_attn(roped, plain, lam, subln, batch, seq, lambda_init):
    tq = DIFF_TQ
    nq = seq // tq
    pair = 2 * HEAD_DIM
    dq_cb = (NSA_HEADS + 2 * NSA_GROUPS) * HEAD_DIM // pair
    dk_cb = dq_cb + DIFF_HEADS
    dv_cb = 2 * NSA_GROUPS * HEAD_DIM // DIFF_V
    return pl.pallas_call(
        functools.partial(_diff_kernel, lambda_init=lambda_init),
        grid=(batch, DIFF_HEADS, nq),
        in_specs=[pl.BlockSpec((tq, pair), lambda b, h, t: (b * nq + t, dq_cb + h)),
                  pl.BlockSpec((seq, pair), lambda b, h, t: (b, dk_cb + h)),
                  pl.BlockSpec((seq, DIFF_V), lambda b, h, t: (b, dv_cb + h)),
                  pl.BlockSpec(lam.shape, lambda b, h, t: (0, 0)),
                  pl.BlockSpec((1, DIFF_V), lambda b, h, t: (0, 0))],
        out_specs=pl.BlockSpec((tq, DIFF_V), lambda b, h, t: (b * nq + t, h)),
        out_shape=jax.ShapeDtypeStruct((batch * seq, DIFF_HEADS * DIFF_V), BF16),
        scratch_shapes=[pltpu.VMEM((2 * tq, DIFF_V), F32)],
        compiler_params=_cparams(("parallel", "parallel", "arbitrary")),
        name="diff_attention",
    )(roped, roped, plain, lam, subln.reshape(1, DIFF_V))


def _rope_tables(pos):
    inv_freq = ROPE_THETA ** (-jnp.arange(ROPE_HALF, dtype=F32) / ROPE_HALF)
    ang = pos.astype(F32)[:, None] * inv_freq[None, :]
    cos, sin = jnp.cos(ang), jnp.sin(ang)
    rest = LANES - 2 * ROPE_HALF
    n = pos.shape[0]
    cosf = jnp.concatenate([cos, cos, jnp.ones((n, rest), F32)], axis=1)
    sinf = jnp.concatenate([-sin, sin, jnp.zeros((n, rest), F32)], axis=1)
    return cosf, sinf


def _router_weights(w_group, b_group, w_expert, b_expert):
    pad = LANES - MOE_GROUPS - MOE_EXPERTS
    wr = jnp.concatenate([w_group, w_expert, jnp.zeros((w_group.shape[0], pad), F32)], axis=1)
    br = jnp.concatenate([b_group, b_expert, jnp.zeros((pad,), F32)]).reshape(1, LANES)
    w_hi = wr.astype(BF16)
    w_lo = (wr - w_hi.astype(F32)).astype(BF16)
    return jnp.concatenate([w_hi, w_lo], axis=1), br


def _even_layer(h, hn, batch, seq, w_in, conv_w, conv_b, dt_bias, a_log, d_skip, gate_norm, sc_w, w_out,
                g_ffn, wr, br):
    i = SSD_INNER
    conv_ch = i + 2 * SSD_GROUPS * SSD_STATE
    o_dt = i + conv_ch
    o_sc = o_dt + SSD_HEADS
    w_main = jnp.concatenate([w_in[:, :o_dt], w_in[:, o_sc:]], axis=1).astype(BF16)
    w_dt = jnp.pad(w_in[:, o_dt:o_sc], ((0, 0), (0, LANES - SSD_HEADS))).astype(BF16)
    proj = _matmul(hn, w_main, F32, 1024, 1024)
    mix = _even_core(proj, hn, w_dt, batch, seq, conv_w, conv_b, dt_bias, a_log, d_skip, gate_norm, sc_w)
    return _outproj([mix], [w_out.astype(BF16)], h, g_ffn, wr, br)


def _odd_layer(h, hn, batch, seq, w_in, cmp_pos, cmp_w1, cmp_w2, lam, subln, w_out, lambda_init, g_ffn, wr, br):
    hd, kvw = HEAD_DIM, NSA_GROUPS * HEAD_DIM
    o = [0]
    for wdt in (NSA_HEADS * hd,) + (kvw,) * 6 + (3 * NSA_HEADS,) + (DIFF_HEADS * 2 * hd,) * 2 + (DIFF_HEADS * DIFF_V,):
        o.append(o[-1] + wdt)
    col = lambda k: w_in[:, o[k]:o[k + 1]]
    q, k_cmp, v_cmp, k_sel, v_sel, k_win, v_win, gates, dq, dk, dv = [col(k) for k in range(11)]
    gcols = []
    for g in range(NSA_GROUPS):
        per = 3 * NSA_REP
        gcols.append(jnp.pad(gates[:, g * per:(g + 1) * per], ((0, 0), (0, LANES - per))))
    w_rope = jnp.concatenate([q, k_sel, k_win, dq, dk], axis=1).astype(BF16)
    w_plain = jnp.concatenate([v_sel, v_win, dv], axis=1).astype(BF16)
    w_cmp_gates = jnp.concatenate([k_cmp, v_cmp] + gcols, axis=1).astype(BF16)
    pos = jnp.arange(seq)
    roped = _matmul(hn, w_rope, BF16, 512, 512, rope=_rope_tables(pos), seq=seq)
    plain = _matmul(hn, w_plain, BF16, 512, 512)
    kv_cmp = _matmul(hn, w_cmp_gates, F32, 512, w_cmp_gates.shape[1])
    ncmp = seq // CMP_STRIDE
    cmp_end = jnp.arange(ncmp) * CMP_STRIDE + CMP_CHUNKS * CMP_STRIDE - 1
    cos_c, sin_c = _rope_tables(cmp_end)
    cmp_kv = _compress(kv_cmp, batch, seq, cmp_pos, cmp_w1, cmp_w2, cos_c, sin_c)
    n_idx = jnp.arange(ncmp)[:, None]
    j_idx = jnp.arange(LANES)[None, :]
    per_sel = SEL_BLOCK // CMP_STRIDE
    pool = ((n_idx >= per_sel * j_idx - 1) & (n_idx <= per_sel * j_idx + per_sel - 1)
            & (n_idx < ncmp - 1) & (j_idx < seq // SEL_BLOCK)).astype(F32)
    negexp = jnp.where((jnp.arange(seq)[:, None] // SEL_BLOCK) == jnp.arange(LANES)[None, :], NEG_INF, 0.0).astype(BF16)
    o_nsa = _nsa(roped, plain, cmp_kv, kv_cmp, batch, seq, pool, negexp)
    o_diff = _diff_attn(roped, plain, lam, subln, batch, seq, lambda_init)
    n_nsa = NSA_HEADS * hd
    w_o = w_out.astype(BF16)
    return _outproj([o_nsa, o_diff], [w_o[:n_nsa], w_o[n_nsa:]], h, g_ffn, wr, br)


def kernel(x, p, norm_mix, norm_ffn, norm_ple, norm_final, ev_w_in, ev_conv_w, ev_conv_b, ev_dt_bias, ev_a_log, ev_d_skip, ev_gate_norm, ev_sc_w, ev_w_out, od_w_in, od_cmp_pos, od_cmp_w1, od_cmp_w2, od_lambda, od_subln, od_w_out, moe_w_group, moe_b_group, moe_w_expert, moe_b_expert, moe_w_gate, moe_w_up, moe_w_down, ple_gate, ple_proj):
    batch, seq, d = x.shape
    n = batch * seq
    depth = p.shape[0]
    h = x.reshape(n, d)
    hn = _rmsnorm(h, norm_mix[0], BF16)
    for i in range(depth):
        j = i // 2
        wr, br = _router_weights(moe_w_group[i], moe_b_group[i], moe_w_expert[i], moe_b_expert[i])
        if i % 2 == 0:
            h, hn2, logits = _even_layer(h, hn, batch, seq, ev_w_in[j], ev_conv_w[j], ev_conv_b[j], ev_dt_bias[j],
                                         ev_a_log[j], ev_d_skip[j], ev_gate_norm[j], ev_sc_w[j], ev_w_out[j],
                                         norm_ffn[i], wr, br)
        else:
            lambda_init = 0.8 - 0.6 * math.exp(-0.3 * i)
            h, hn2, logits = _odd_layer(h, hn, batch, seq, od_w_in[j], od_cmp_pos[j], od_cmp_w1[j], od_cmp_w2[j],
                                        od_lambda[j], od_subln[j], od_w_out[j], lambda_init, norm_ffn[i], wr, br)
        yb, slot1, slot2, routed = _moe(hn2, logits, i, moe_w_gate, moe_w_up, moe_w_down)
        last = i == depth - 1
        g_next = norm_final if last else norm_mix[i + 1]
        h, hn = _ple(h, yb, slot1, slot2, routed, p.reshape(depth * n, PLE_DIM), i, norm_ple[i],
                     ple_gate[i].astype(BF16), ple_proj[i].astype(BF16), g_next, F32 if last else BF16)
    return hn.reshape(batch, seq, d)
```
